```python
import jax
import jax.numpy as jnp
from jax import lax
import numpy as np

D_MODEL = 2048
BATCH = 4
SEQ = 4096
DEPTH = 2

GRID_W = 64
CTX_LEN = 256
N_BRANCH = 4
BRANCH_W = 512
MLA_HEADS = 4
QK_NOPE = 128
QK_ROPE = 64
V_HEAD = 128
Q_LORA = 512
KV_LORA = 256
KV_COLS = KV_LORA + QK_ROPE
ROPE_THETA = 10000.0
ATTN_BLOCK = 128
FOURIER_GROUPS = 4
CONV_W = 3
CHUNK = 128
SGU_GROUPS = 4
IN_WIDTH = KV_LORA + QK_ROPE + Q_LORA + 10 * BRANCH_W + N_BRANCH * D_MODEL
EPS = 1e-6

kernel_name = 'hybrid_parallel_mla_fnet_conv_gmlp_dit'


def rmsnorm(x, g):
    xf = x.astype(jnp.float32)
    y = xf * lax.rsqrt(jnp.mean(xf * xf, axis=-1, keepdims=True) + EPS)
    return (y * g.astype(jnp.float32)).astype(x.dtype)


def layernorm(x, g, b):
    xf = x.astype(jnp.float32)
    xc = xf - jnp.mean(xf, axis=-1, keepdims=True)
    y = xc * lax.rsqrt(jnp.mean(xc * xc, axis=-1, keepdims=True) + EPS)
    return (y * g.astype(jnp.float32) + b.astype(jnp.float32)).astype(x.dtype)


def modulate(h, shift, scale):
    return h * (1 + scale[:, None, :]) + shift[:, None, :]


def axial_rope(n, dtype):
    rows = n // GRID_W
    r, col = jnp.meshgrid(jnp.arange(rows, dtype=jnp.float32),
                          jnp.arange(GRID_W, dtype=jnp.float32), indexing='ij')
    half = QK_ROPE // 2
    inv = ROPE_THETA ** (-jnp.arange(0, half, 2, dtype=jnp.float32) / half)
    ang = jnp.concatenate([r.reshape(-1, 1) * inv, col.reshape(-1, 1) * inv], axis=-1)
    return jnp.cos(ang).astype(dtype), jnp.sin(ang).astype(dtype)


def apply_rope(x, cos, sin):
    x1, x2 = jnp.split(x, 2, axis=-1)
    return jnp.concatenate([x1 * cos - x2 * sin, x1 * sin + x2 * cos], axis=-1)


def split_proj(p):
    sizes = (KV_LORA, QK_ROPE, Q_LORA) + (BRANCH_W,) * 10
    idx = np.cumsum(sizes).tolist()
    parts = jnp.split(p, idx, axis=-1)
    names = ('ckv', 'krope', 'cq', 'zA', 'uB', 'zB', 'xC', 'bC', 'cC', 'zC', 'uD', 'vD', 'zD', 'gates')
    return dict(zip(names, parts))


def mla_keys(ckv, krope, kv_norm_g, w_ukv, cos=None, sin=None):
    b, n, _ = ckv.shape
    kv = (rmsnorm(ckv, kv_norm_g) @ w_ukv).reshape(b, n, MLA_HEADS, QK_NOPE + V_HEAD)
    k_nope, v = jnp.split(kv, [QK_NOPE], axis=-1)
    k_pe = krope if cos is None else apply_rope(krope, cos, sin)
    return k_nope, k_pe, v


def mla_queries(cq, q_norm_g, w_uq, cos=None, sin=None):
    b, n, _ = cq.shape
    q = (rmsnorm(cq, q_norm_g) @ w_uq).reshape(b, n, MLA_HEADS, QK_NOPE + QK_ROPE)
    q_nope, q_pe = jnp.split(q, [QK_NOPE], axis=-1)
    if cos is not None:
        q_pe = apply_rope(q_pe, cos[:, None, :], sin[:, None, :])
    return q_nope, q_pe


def block_attention(q_nope, q_pe, k_nope, k_pe, v):
    b, nq, h, _ = q_nope.shape
    nb = nq // ATTN_BLOCK
    scale = (QK_NOPE + QK_ROPE) ** -0.5

    def one_block(args):
        qn, qp = args
        s = jnp.einsum('bqhd,bkhd->bhqk', qn, k_nope) + jnp.einsum('bqhr,bkr->bhqk', qp, k_pe)
        p = jax.nn.softmax(s.astype(jnp.float32) * scale, axis=-1).astype(v.dtype)
        return jnp.einsum('bhqk,bkhd->bqhd', p, v)

    qn_b = q_nope.reshape(b, nb, ATTN_BLOCK, h, QK_NOPE).swapaxes(0, 1)
    qp_b = q_pe.reshape(b, nb, ATTN_BLOCK, h, QK_ROPE).swapaxes(0, 1)
    o = lax.map(one_block, (qn_b, qp_b))
    return o.swapaxes(0, 1).reshape(b, nq, h * V_HEAD)


def fourier_mix(u):
    b, n, w = u.shape
    ug = u.astype(jnp.float32).reshape(b, n, FOURIER_GROUPS, w // FOURIER_GROUPS)
    y = jnp.fft.fft2(ug, axes=(1, 3), norm='ortho').real
    return y.reshape(b, n, w).astype(u.dtype)


def short_conv(x, w, bias):
    xp = jnp.pad(x, ((0, 0), (1, 1), (0, 0)))
    return xp[:, :-2] * w[0] + xp[:, 1:-1] * w[1] + xp[:, 2:] * w[2] + bias


def spatial_gate(u, v, ln_g, ln_b, w_s, b_s):
    b, n, w = v.shape
    vc = layernorm(v, ln_g, ln_b).reshape(b, n // CHUNK, CHUNK, SGU_GROUPS, w // SGU_GROUPS)
    mixed = jnp.einsum('gpq,bcqgd->bcpgd', w_s, vc) + b_s.T[None, None, :, :, None]
    return u * mixed.reshape(b, n, w)


def mix_branches(p, attn, conv_w, conv_b, sgu_ln_g, sgu_ln_b, sgu_w, sgu_b, w_branch, w_out):
    silu = jax.nn.silu
    ya = attn * silu(p['zA'])
    yb = fourier_mix(p['uB']) * silu(p['zB'])
    yc = p['bC'] * short_conv(p['cC'] * p['xC'], conv_w, conv_b) * silu(p['zC'])
    yd = spatial_gate(p['uD'], p['vD'], sgu_ln_g, sgu_ln_b, sgu_w, sgu_b) * silu(p['zD'])
    ys = jnp.stack([ya, yb, yc, yd], axis=2)
    b, n = ys.shape[0], ys.shape[1]
    br = jnp.einsum('bngw,gwd->bngd', ys, w_branch)
    gates = jax.nn.sigmoid(p['gates'].reshape(b, n, N_BRANCH, D_MODEL))
    merged = jnp.sum(gates * br, axis=2)
    return merged @ w_out


def setup_inputs(seed: int = 0) -> dict:
    key = jax.random.key(seed)
    ks = jax.random.split(key, 24)
    f32 = jnp.float32
    nrm = lambda k, s: jax.random.normal(k, s, dtype=f32)
    L = DEPTH
    return {
        'x': nrm(ks[0], (BATCH, SEQ, D_MODEL)),
        'c': nrm(ks[1], (BATCH, D_MODEL)),
        'ctx': nrm(ks[2], (BATCH, CTX_LEN, D_MODEL)),
        'c_ctx': nrm(ks[3], (D_MODEL,)),
        'w_mod': nrm(ks[4], (L, D_MODEL, 3 * D_MODEL)) * (0.5 * D_MODEL ** -0.5),
        'b_mod': nrm(ks[5], (L, 3 * D_MODEL)) * 0.02,
        'pre_g': 1.0 + 0.02 * nrm(ks[6], (L, D_MODEL)),
        'post_g': 1.0 + 0.02 * nrm(ks[7], (L, D_MODEL)),
        'w_in': nrm(ks[8], (L, D_MODEL, IN_WIDTH)) * D_MODEL ** -0.5,
        'q_norm_g': 1.0 + 0.02 * nrm(ks[9], (L, Q_LORA)),
        'kv_norm_g': 1.0 + 0.02 * nrm(ks[10], (L, KV_LORA)),
        'w_uq': nrm(ks[11], (L, Q_LORA, MLA_HEADS * (QK_NOPE + QK_ROPE))) * Q_LORA ** -0.5,
        'w_ukv': nrm(ks[12], (L, KV_LORA, MLA_HEADS * (QK_NOPE + V_HEAD))) * KV_LORA ** -0.5,
        'conv_w': nrm(ks[13], (L, CONV_W, BRANCH_W)) * CONV_W ** -0.5,
        'conv_b': nrm(ks[14], (L, BRANCH_W)) * 0.02,
        'sgu_ln_g': 1.0 + 0.02 * nrm(ks[15], (L, BRANCH_W)),
        'sgu_ln_b': nrm(ks[16], (L, BRANCH_W)) * 0.02,
        'sgu_w': nrm(ks[17], (L, SGU_GROUPS, CHUNK, CHUNK)) * CHUNK ** -0.5,
        'sgu_b': 1.0 + 0.02 * nrm(ks[18], (L, SGU_GROUPS, CHUNK)),
        'w_branch': nrm(ks[19], (L, N_BRANCH, BRANCH_W, D_MODEL)) * BRANCH_W ** -0.5,
        'w_out': nrm(ks[20], (L, D_MODEL, D_MODEL)) * D_MODEL ** -0.5,
    }


def reference(x, c, ctx, c_ctx, w_mod, b_mod, pre_g, post_g, w_in, q_norm_g, kv_norm_g, w_uq, w_ukv,
              conv_w, conv_b, sgu_ln_g, sgu_ln_b, sgu_w, sgu_b, w_branch, w_out):
    n = x.shape[1]
    cos, sin = axial_rope(n, x.dtype)
    for l in range(DEPTH):
        last = l == DEPTH - 1
        mod_x = jax.nn.silu(c) @ w_mod[l] + b_mod[l]
        mod_c = (jax.nn.silu(c_ctx) @ w_mod[l] + b_mod[l])[None, :]
        shift_x, scale_x, gate_x = jnp.split(mod_x, 3, axis=-1)
        shift_c, scale_c, gate_c = jnp.split(mod_c, 3, axis=-1)
        hx = modulate(rmsnorm(x, pre_g[l]), shift_x, scale_x)
        hc = modulate(rmsnorm(ctx, pre_g[l]), shift_c, scale_c)

        px = split_proj(hx @ w_in[l])
        if last:
            ckv_c, krope_c = jnp.split(hc @ w_in[l][:, :KV_COLS], [KV_LORA], axis=-1)
        else:
            pc = split_proj(hc @ w_in[l])
            ckv_c, krope_c = pc['ckv'], pc['krope']

        kn_c, kp_c, v_c = mla_keys(ckv_c, krope_c, kv_norm_g[l], w_ukv[l])
        kn_x, kp_x, v_x = mla_keys(px['ckv'], px['krope'], kv_norm_g[l], w_ukv[l], cos, sin)
        qn_x, qp_x = mla_queries(px['cq'], q_norm_g[l], w_uq[l], cos, sin)
        attn_x = block_attention(qn_x, qp_x,
                                 jnp.concatenate([kn_c, kn_x], axis=1),
                                 jnp.concatenate([kp_c, kp_x], axis=1),
                                 jnp.concatenate([v_c, v_x], axis=1))
        out_x = mix_branches(px, attn_x, conv_w[l], conv_b[l], sgu_ln_g[l], sgu_ln_b[l],
                             sgu_w[l], sgu_b[l], w_branch[l], w_out[l])

        if not last:
            qn_c, qp_c = mla_queries(pc['cq'], q_norm_g[l], w_uq[l])
            attn_c = block_attention(qn_c, qp_c, kn_c, kp_c, v_c)
            out_c = mix_branches(pc, attn_c, conv_w[l], conv_b[l], sgu_ln_g[l], sgu_ln_b[l],
                                 sgu_w[l], sgu_b[l], w_branch[l], w_out[l])
            ctx = ctx + gate_c[:, None, :] * rmsnorm(out_c, post_g[l])

        x = x + gate_x[:, None, :] * rmsnorm(out_x, post_g[l])
    return x
```

```python
import functools

import jax
import jax.numpy as jnp
import numpy as np
from jax import lax
from jax.experimental import pallas as pl
from jax.experimental.pallas import tpu as pltpu

F32 = jnp.float32
BF16 = jnp.bfloat16

MLA_HEADS = 4
QK_NOPE = 128
QK_ROPE = 64
V_HEAD = 128
GRID_W = 64
ROPE_THETA = 10000.0
FOURIER_GROUPS = 4
CHUNK = 128
N_BRANCH = 4
N_BRANCH_COLS = 10
EPS = 1e-6

ROPE_OFF = 256
CQ_OFF = 512
HEAD_W = 1024
QK_W = QK_NOPE + 2 * QK_ROPE

FFT_N2 = 16

V7X_VMEM_BYTES = 64 * 2**20
VMEM_CAP_BYTES = V7X_VMEM_BYTES - 8 * 2**20


def _nbytes(shape, dtype):
    return int(np.prod(shape)) * jnp.dtype(dtype).itemsize


def _compiler_params(semantics, block_bytes, scratch_bytes=0, temp_bytes=0):
    need = 2 * block_bytes + scratch_bytes + temp_bytes + 2 * 2**20
    return pltpu.CompilerParams(dimension_semantics=semantics,
                                vmem_limit_bytes=int(min(max(need, 16 * 2**20), VMEM_CAP_BYTES)))


def _silu(v):
    return v * jax.nn.sigmoid(v)


def _rms(v, g):
    return v * lax.rsqrt(jnp.mean(v * v, axis=-1, keepdims=True) + EPS) * g


def _dot(a, b):
    return jnp.dot(a, b, preferred_element_type=F32)


def _mod_kernel(c_ref, w_ref, b_ref, o_ref):
    s = _silu(c_ref[...]).astype(BF16)
    o_ref[0] = _dot(s, w_ref[0].astype(BF16)) + b_ref[0]


def _modulation(cc, w_mod, b_mod):
    L, D, W = w_mod.shape
    tn = 768 if W % 768 == 0 else W
    blocks = _nbytes((D, tn), F32) + _nbytes((8, D), F32) + 2 * _nbytes((8, tn), F32)
    return pl.pallas_call(
        _mod_kernel,
        out_shape=jax.ShapeDtypeStruct((L, 8, W), F32),
        grid=(L, W // tn),
        in_specs=[pl.BlockSpec((8, D), lambda l, j: (0, 0)),
                  pl.BlockSpec((1, D, tn), lambda l, j: (l, 0, j)),
                  pl.BlockSpec((1, 1, tn), lambda l, j: (l, 0, j))],
        out_specs=pl.BlockSpec((1, 8, tn), lambda l, j: (l, 0, j)),
        compiler_params=_compiler_params(("parallel", "parallel"), blocks, 0, _nbytes((D, tn), BF16)),
        name="modulation",
    )(cc, w_mod, b_mod.reshape(L, 1, W))


def _inproj_kernel(x_ref, mod_ref, g_ref, w_ref, o_ref, h_ref):
    @pl.when(pl.program_id(1) == 0)
    def _():
        y = _rms(x_ref[...], g_ref[...])
        h_ref[...] = (y * (1.0 + mod_ref[0, 1:2, :]) + mod_ref[0, 0:1, :]).astype(BF16)

    o_ref[...] = _dot(h_ref[...], w_ref[...]).astype(o_ref.dtype)


def _in_projection(x2, mod3, mod_row, pre_g, w, n):
    R, D = x2.shape
    W = w.shape[1]
    tm = min(1024, n)
    tn = min(1024, W)
    blocks = (_nbytes((tm, D), F32) + _nbytes((3, D), F32) + _nbytes((D, tn), BF16)
              + _nbytes((tm, tn), BF16))
    return pl.pallas_call(
        _inproj_kernel,
        out_shape=jax.ShapeDtypeStruct((R, W), BF16),
        grid=(R // tm, W // tn),
        in_specs=[pl.BlockSpec((tm, D), lambda i, j: (i, 0)),
                  pl.BlockSpec((1, 3, D), lambda i, j: (mod_row(i * tm), 0, 0)),
                  pl.BlockSpec((1, D), lambda i, j: (0, 0)),
                  pl.BlockSpec((D, tn), lambda i, j: (0, j))],
        out_specs=pl.BlockSpec((tm, tn), lambda i, j: (i, j)),
        scratch_shapes=[pltpu.VMEM((tm, D), BF16)],
        compiler_params=_compiler_params(("parallel", "arbitrary"), blocks, _nbytes((tm, D), BF16),
                                         2 * _nbytes((tm, D), F32) + _nbytes((tm, tn), F32)),
        name="in_projection",
    )(x2, mod3, pre_g.reshape(1, D), w)


def _rope_pair(group, cs):
    r = group * cs
    return r + pltpu.roll(r, QK_ROPE, axis=1)


def _prep_kernel(p_ref, cs_ref, kvg_ref, qg_ref, wkv_ref, wq_ref, q_ref, k_ref, v_ref, *, scale):
    cs = cs_ref[...]
    kvn = _rms(p_ref[:, 0:ROPE_OFF].astype(F32), kvg_ref[...]).astype(BF16)
    kv = _dot(kvn, wkv_ref[...])
    kr = _rope_pair(p_ref[:, ROPE_OFF:ROPE_OFF + 2 * QK_ROPE].astype(F32), cs)
    lane = lax.broadcasted_iota(jnp.int32, kr.shape, 1)
    kr = jnp.where(lane < QK_ROPE, kr, 0.0).astype(BF16)
    cqn = _rms(p_ref[:, CQ_OFF:HEAD_W].astype(F32), qg_ref[...]).astype(BF16)
    q = _dot(cqn, wq_ref[...]) * scale
    kvw = QK_NOPE + V_HEAD
    for h in range(MLA_HEADS):
        k_ref[0, h, :, 0:QK_NOPE] = kv[:, h * kvw:h * kvw + QK_NOPE].astype(BF16)
        k_ref[0, h, :, QK_NOPE:QK_W] = kr
        v_ref[0, h] = kv[:, h * kvw + QK_NOPE:(h + 1) * kvw].astype(BF16)
        q_ref[0, h, :, 0:QK_NOPE] = q[:, h * QK_W:h * QK_W + QK_NOPE].astype(BF16)
        q_ref[0, h, :, QK_NOPE:QK_W] = _rope_pair(q[:, h * QK_W + QK_NOPE:(h + 1) * QK_W], cs).astype(BF16)


def _mla_prep(p, cs, kv_norm_g, q_norm_g, w_ukv, w_q2, B, n):
    tm = min(512, n)
    nt = n // tm
    H = MLA_HEADS
    kv_lora, q_lora = w_ukv.shape[0], w_q2.shape[0]
    blocks = (_nbytes((tm, HEAD_W), BF16) + _nbytes((tm, 128), F32) + _nbytes(w_ukv.shape, BF16)
              + _nbytes(w_q2.shape, BF16) + 2 * _nbytes((H, tm, QK_W), BF16) + _nbytes((H, tm, V_HEAD), BF16))
    qk_spec = pl.BlockSpec((1, H, tm, QK_W), lambda b, i: (b, 0, i, 0))
    return pl.pallas_call(
        functools.partial(_prep_kernel, scale=float(QK_NOPE + QK_ROPE) ** -0.5),
        out_shape=(jax.ShapeDtypeStruct((B, H, n, QK_W), BF16),
                   jax.ShapeDtypeStruct((B, H, n, QK_W), BF16),
                   jax.ShapeDtypeStruct((B, H, n, V_HEAD), BF16)),
        grid=(B, nt),
        in_specs=[pl.BlockSpec((tm, HEAD_W), lambda b, i: (b * nt + i, 0)),
                  pl.BlockSpec((tm, 128), lambda b, i: (i, 0)),
                  pl.BlockSpec((1, kv_lora), lambda b, i: (0, 0)),
                  pl.BlockSpec((1, q_lora), lambda b, i: (0, 0)),
                  pl.BlockSpec(w_ukv.shape, lambda b, i: (0, 0)),
                  pl.BlockSpec(w_q2.shape, lambda b, i: (0, 0))],
        out_specs=(qk_spec, qk_spec, pl.BlockSpec((1, H, tm, V_HEAD), lambda b, i: (b, 0, i, 0))),
        compiler_params=_compiler_params(("parallel", "parallel"), blocks, 0, 6 * _nbytes((tm, HEAD_W), F32)),
        name="mla_prep",
    )(p, cs, kv_norm_g.reshape(1, -1), q_norm_g.reshape(1, -1), w_ukv, w_q2)


_NT = (((1,), (1,)), ((), ()))


def _attn_kernel(*refs, with_x):
    if with_x:
        q_ref, kc_ref, vc_ref, kx_ref, vx_ref, o_ref = refs
    else:
        q_ref, kc_ref, vc_ref, o_ref = refs
    q = q_ref[0, 0]
    sc = lax.dot_general(q, kc_ref[0, 0], _NT, preferred_element_type=F32)
    m = jnp.max(sc, axis=-1, keepdims=True)
    if with_x:
        sx = lax.dot_general(q, kx_ref[0, 0], _NT, preferred_element_type=F32)
        m = jnp.maximum(m, jnp.max(sx, axis=-1, keepdims=True))
    pc = jnp.exp(sc - m)
    l = jnp.sum(pc, axis=-1, keepdims=True)
    o = _dot(pc.astype(BF16), vc_ref[0, 0])
    if with_x:
        px = jnp.exp(sx - m)
        l = l + jnp.sum(px, axis=-1, keepdims=True)
        o = o + _dot(px.astype(BF16), vx_ref[0, 0])
    o_ref[0] = (o / l).astype(o_ref.dtype)


def _attention(q, kc, vc, kx=None, vx=None):
    B, H, nq, _ = q.shape
    nc = kc.shape[2]
    with_x = kx is not None
    nk = nc + (kx.shape[2] if with_x else 0)
    tq = min(256, nq)
    q_spec = pl.BlockSpec((1, 1, tq, QK_W), lambda b, h, i: (b, h, i, 0))

    def full(arr):
        return pl.BlockSpec((1, 1) + arr.shape[2:], lambda b, h, i: (b, h, 0, 0))

    args = [q, kc, vc] + ([kx, vx] if with_x else [])
    blocks = _nbytes((tq, QK_W), BF16) + _nbytes((nk, QK_W + V_HEAD), BF16) + _nbytes((tq, V_HEAD), BF16)
    return pl.pallas_call(
        functools.partial(_attn_kernel, with_x=with_x),
        out_shape=jax.ShapeDtypeStruct((B, nq, H * V_HEAD), BF16),
        grid=(B, H, nq // tq),
        in_specs=[q_spec] + [full(a) for a in args[1:]],
        out_specs=pl.BlockSpec((1, tq, V_HEAD), lambda b, h, i: (b, i, h)),
        compiler_params=_compiler_params(("parallel", "parallel", "parallel"), blocks, 0,
                                         3 * _nbytes((tq, nk), F32)),
        name="attention",
    )(*args)


def _fft1_kernel(u_ref, f_ref, tc_ref, ts_ref, o_ref, *, n1, reps):
    g = _dot(f_ref[...], u_ref[0])
    gr, gi = g[:n1], g[n1:]
    tc = jnp.concatenate([tc_ref[0]] * reps, axis=1)
    ts = jnp.concatenate([ts_ref[0]] * reps, axis=1)
    o_ref[0, 0, 0] = (gr * tc + gi * ts).astype(o_ref.dtype)
    o_ref[0, 1, 0] = (gi * tc - gr * ts).astype(o_ref.dtype)


def _fft2_kernel(g_ref, f_ref, o_ref):
    o_ref[0] = _dot(f_ref[...], g_ref[0]).astype(o_ref.dtype)


def _dft_tables(n):
    n2 = FFT_N2
    n1 = n // n2
    a = np.arange(n1)
    ang1 = 2.0 * np.pi * ((a[:, None] * a[None, :]) % n1) / n1
    f1 = np.concatenate([np.cos(ang1), -np.sin(ang1)], axis=0)
    b = np.arange(n2)
    angt = 2.0 * np.pi * (b[:, None] * a[None, :]) / n
    tc = np.broadcast_to(np.cos(angt)[:, :, None], (n2, n1, 128))
    ts = np.broadcast_to(np.sin(angt)[:, :, None], (n2, n1, 128))
    ang2 = 2.0 * np.pi * ((b[:, None] * b[None, :]) % n2) / n2
    c2, s2 = np.cos(ang2), np.sin(ang2)
    f2 = np.block([[c2, s2], [-s2, c2]])
    return (jnp.asarray(f1, F32).astype(BF16), jnp.asarray(tc, F32), jnp.asarray(ts, F32),
            jnp.asarray(f2, F32).astype(BF16))


def _position_dft(p, col_block, C, B, n):
    n2 = FFT_N2
    n1 = n // n2
    W = p.shape[1]
    f1, tc, ts, f2 = _dft_tables(n)
    p3 = p.reshape(B, n1, n2 * W)
    cpb = W // C
    blocks = (_nbytes((n1, C), BF16) + _nbytes((2 * n1, n1), BF16) + 2 * _nbytes((n1, 128), F32)
              + 2 * _nbytes((n1, C), BF16))
    g = pl.pallas_call(
        functools.partial(_fft1_kernel, n1=n1, reps=C // 128),
        out_shape=jax.ShapeDtypeStruct((B, 2, n2, n1, C), BF16),
        grid=(B, n2),
        in_specs=[pl.BlockSpec((1, n1, C), lambda b, r: (b, 0, r * cpb + col_block)),
                  pl.BlockSpec((2 * n1, n1), lambda b, r: (0, 0)),
                  pl.BlockSpec((1, n1, 128), lambda b, r: (r, 0, 0)),
                  pl.BlockSpec((1, n1, 128), lambda b, r: (r, 0, 0))],
        out_specs=pl.BlockSpec((1, 2, 1, n1, C), lambda b, r: (b, 0, r, 0, 0)),
        compiler_params=_compiler_params(("parallel", "parallel"), blocks, 0, 8 * _nbytes((n1, C), F32)),
        name="position_dft_stage1",
    )(p3, f1, tc, ts)
    g2 = g.reshape(B, 2 * n2, n1 * C)
    tn = min(8192, n1 * C)
    blocks = 2 * _nbytes((2 * n2, tn), BF16) + _nbytes((2 * n2, 2 * n2), BF16)
    v = pl.pallas_call(
        _fft2_kernel,
        out_shape=jax.ShapeDtypeStruct((B, 2 * n2, n1 * C), BF16),
        grid=(B, n1 * C // tn),
        in_specs=[pl.BlockSpec((1, 2 * n2, tn), lambda b, j: (b, 0, j)),
                  pl.BlockSpec((2 * n2, 2 * n2), lambda b, j: (0, 0))],
        out_specs=pl.BlockSpec((1, 2 * n2, tn), lambda b, j: (b, 0, j)),
        compiler_params=_compiler_params(("parallel", "parallel"), blocks, 0, _nbytes((2 * n2, tn), F32)),
        name="position_dft_stage2",
    )(g2, f2)
    return v.reshape(B, 2, n, C)


def _mix_kernel(attn_ref, vr_ref, vi_ref, zA_ref, zB_ref, xC_ref, bC_ref, cC_ref, zC_ref, uD_ref, vD_ref,
                zD_ref, xCp_ref, cCp_ref, xCn_ref, cCn_ref, g0_ref, g1_ref, g2_ref, g3_ref, x_ref, mod_ref,
                ccb_ref, scb_ref, convw_ref, convb_ref, lng_ref, lnb_ref, ws_ref, bsb_ref, wb_ref, wo_ref,
                postg_ref, o_ref, ys_ref, acc_ref, *, n, tm, halo, fscale):
    i = pl.program_id(0)
    j = pl.program_id(1)

    @pl.when(j == 0)
    def _():
        def gate(ref):
            return _silu(ref[...].astype(F32))

        ys_ref[0] = (attn_ref[...].astype(F32) * gate(zA_ref)).astype(BF16)
        four = _dot(vr_ref[0, 0], ccb_ref[...]) + _dot(vi_ref[0, 0], scb_ref[...])
        ys_ref[1] = (four * fscale * gate(zB_ref)).astype(BF16)
        u = cC_ref[...].astype(F32) * xC_ref[...].astype(F32)
        first = (i * tm) % n == 0
        last = ((i + 1) * tm) % n == 0
        up = (cCp_ref[...].astype(F32) * xCp_ref[...].astype(F32))[halo - 1:halo, :]
        un = (cCn_ref[...].astype(F32) * xCn_ref[...].astype(F32))[0:1, :]
        up = up * jnp.where(first, 0.0, 1.0)
        un = un * jnp.where(last, 0.0, 1.0)
        row = lax.broadcasted_iota(jnp.int32, u.shape, 0)
        prev = jnp.where(row == 0, up, pltpu.roll(u, 1, axis=0))
        nxt = jnp.where(row == tm - 1, un, pltpu.roll(u, tm - 1, axis=0))
        conv = prev * convw_ref[0:1, :] + u * convw_ref[1:2, :] + nxt * convw_ref[2:3, :] + convb_ref[...]
        ys_ref[2] = (bC_ref[...].astype(F32) * conv * gate(zC_ref)).astype(BF16)
        v = vD_ref[...].astype(F32)
        vc = v - jnp.mean(v, axis=-1, keepdims=True)
        vn = vc * lax.rsqrt(jnp.mean(vc * vc, axis=-1, keepdims=True) + EPS) * lng_ref[...] + lnb_ref[...]
        vn = vn.astype(BF16)
        ug = uD_ref[...].astype(F32) * gate(zD_ref)
        gw = vn.shape[1] // ws_ref.shape[0]
        for g in range(ws_ref.shape[0]):
            for c in range(tm // CHUNK):
                rs, cs = slice(c * CHUNK, (c + 1) * CHUNK), slice(g * gw, (g + 1) * gw)
                mixed = _dot(ws_ref[g], vn[rs, cs]) + bsb_ref[g]
                ys_ref[3, rs, cs] = (ug[rs, cs] * mixed).astype(BF16)
        acc_ref[...] = jnp.zeros_like(acc_ref)

    merged = None
    for g, g_ref in enumerate((g0_ref, g1_ref, g2_ref, g3_ref)):
        term = jax.nn.sigmoid(g_ref[...].astype(F32)) * _dot(ys_ref[g], wb_ref[g])
        merged = term if merged is None else merged + term
    acc_ref[...] += _dot(merged.astype(BF16), wo_ref[...])

    @pl.when(j == pl.num_programs(1) - 1)
    def _():
        o_ref[...] = x_ref[...] + mod_ref[0, 2:3, :] * _rms(acc_ref[...], postg_ref[...])


def _mix(p, attn, v, x2, mod3, mod_row, consts, B, n):
    R, D = x2.shape
    bw = attn.shape[1]
    tm = min(512, n)
    tn = min(512, D)
    halo = 16
    nt = n // tm
    col0 = HEAD_W // bw
    gate0 = (HEAD_W + N_BRANCH_COLS * bw) // tn

    def seg(k):
        return pl.BlockSpec((tm, bw), lambda i, j: (i, col0 + k))

    def halo_prev(k):
        return pl.BlockSpec((halo, bw), lambda i, j: (jnp.maximum(i * (tm // halo) - 1, 0), col0 + k))

    def halo_next(k):
        return pl.BlockSpec((halo, bw), lambda i, j: (jnp.minimum((i + 1) * (tm // halo), R // halo - 1), col0 + k))

    def gates(g):
        return pl.BlockSpec((tm, tn), lambda i, j: (i, gate0 + g * (D // tn) + j))

    def const(arr):
        nd = arr.ndim
        return pl.BlockSpec(arr.shape, lambda i, j: (0,) * nd)

    vspec_r = pl.BlockSpec((1, 1, tm, bw), lambda i, j: (i // nt, 0, i % nt, 0))
    vspec_i = pl.BlockSpec((1, 1, tm, bw), lambda i, j: (i // nt, 1, i % nt, 0))
    in_specs = ([pl.BlockSpec((tm, bw), lambda i, j: (i, 0)), vspec_r, vspec_i]
                + [seg(k) for k in (0, 2, 3, 4, 5, 6, 7, 8, 9)]
                + [halo_prev(3), halo_prev(5), halo_next(3), halo_next(5)]
                + [gates(g) for g in range(N_BRANCH)]
                + [pl.BlockSpec((tm, D), lambda i, j: (i, 0)),
                   pl.BlockSpec((1, 3, D), lambda i, j: (mod_row(i * tm), 0, 0))]
                + [const(consts[k]) for k in ("ccb", "scb", "conv_w", "conv_b", "ln_g", "ln_b", "w_s", "b_s")]
                + [pl.BlockSpec((N_BRANCH, bw, tn), lambda i, j: (0, 0, j)),
                   pl.BlockSpec((tn, D), lambda i, j: (j, 0)),
                   const(consts["post_g"])])
    args = ([attn, v, v] + [p] * 9 + [p] * 4 + [p] * 4 + [x2, mod3]
            + [consts[k] for k in ("ccb", "scb", "conv_w", "conv_b", "ln_g", "ln_b", "w_s", "b_s")]
            + [consts["w_branch"], consts["w_out"], consts["post_g"]])
    blocks = (12 * _nbytes((tm, bw), BF16) + 4 * _nbytes((halo, bw), BF16) + N_BRANCH * _nbytes((tm, tn), BF16)
              + 2 * _nbytes((tm, D), F32) + 2 * _nbytes((bw, bw), BF16) + N_BRANCH * _nbytes((CHUNK, CHUNK), F32) * 2
              + _nbytes((N_BRANCH, bw, tn), BF16) + _nbytes((tn, D), BF16))
    scratch = _nbytes((N_BRANCH, tm, bw), BF16) + _nbytes((tm, D), F32)
    return pl.pallas_call(
        functools.partial(_mix_kernel, n=n, tm=tm, halo=halo,
                          fscale=float(n * (bw // FOURIER_GROUPS)) ** -0.5),
        out_shape=jax.ShapeDtypeStruct((R, D), F32),
        grid=(R // tm, D // tn),
        in_specs=in_specs,
        out_specs=pl.BlockSpec((tm, D), lambda i, j: (i, 0)),
        scratch_shapes=[pltpu.VMEM((N_BRANCH, tm, bw), BF16), pltpu.VMEM((tm, D), F32)],
        compiler_params=_compiler_params(("parallel", "arbitrary"), blocks, scratch,
                                         8 * _nbytes((tm, bw), F32) + 2 * _nbytes((tm, D), F32)),
        name="branch_mix",
    )(*args)


def _swap_halves(w):
    h = w.shape[-1] // 2
    return jnp.concatenate([w[..., h:], w[..., :h]], axis=-1)


def _relayout_w_in(w):
    kv_end = ROPE_OFF
    rope = w[:, kv_end:kv_end + QK_ROPE]
    pad = jnp.zeros((w.shape[0], CQ_OFF - ROPE_OFF - 2 * QK_ROPE), w.dtype)
    return jnp.concatenate([w[:, :kv_end], rope, _swap_halves(rope), pad, w[:, kv_end + QK_ROPE:]],
                           axis=1).astype(BF16)


def _relayout_w_uq(w):
    w = w.reshape(w.shape[0], MLA_HEADS, QK_NOPE + QK_ROPE)
    rope = w[..., QK_NOPE:]
    return jnp.concatenate([w[..., :QK_NOPE], rope, _swap_halves(rope)], axis=-1).reshape(w.shape[0], -1).astype(BF16)


def _rope_table(n, rotate):
    half = QK_ROPE // 2
    if rotate:
        pos = np.arange(n)
        inv = ROPE_THETA ** (-np.arange(0, half, 2, dtype=np.float64) / half)
        ang = np.concatenate([(pos // GRID_W)[:, None] * inv, (pos % GRID_W)[:, None] * inv], axis=-1)
    else:
        ang = np.zeros((n, half))
    cos, sin = np.cos(ang), np.sin(ang)
    return jnp.asarray(np.concatenate([cos, cos, -sin, sin], axis=-1), F32)


def _channel_dft(bw):
    gw = bw // FOURIER_GROUPS
    a = np.arange(gw)
    ang = 2.0 * np.pi * ((a[:, None] * a[None, :]) % gw) / gw
    eye = np.eye(FOURIER_GROUPS)
    return (jnp.asarray(np.kron(eye, np.cos(ang)), F32).astype(BF16),
            jnp.asarray(np.kron(eye, np.sin(ang)), F32).astype(BF16))


def kernel(x, c, ctx, c_ctx, w_mod, b_mod, pre_g, post_g, w_in, q_norm_g, kv_norm_g, w_uq, w_ukv,
           conv_w, conv_b, sgu_ln_g, sgu_ln_b, sgu_w, sgu_b, w_branch, w_out):
    B, n, D = x.shape
    nc = ctx.shape[1]
    depth = w_in.shape[0]
    bw = w_branch.shape[2]
    assert w_in.shape[2] == ROPE_OFF + QK_ROPE + (HEAD_W - CQ_OFF) + N_BRANCH_COLS * bw + N_BRANCH * D
    assert kv_norm_g.shape[1] == ROPE_OFF and q_norm_g.shape[1] == HEAD_W - CQ_OFF and B <= 4

    cc = jnp.zeros((8, D), F32).at[:B].set(c).at[B].set(c_ctx)
    mod = _modulation(cc, w_mod, b_mod).reshape(depth, 8, 3, D)

    cs_x, cs_c = _rope_table(n, True), _rope_table(nc, False)
    ccb, scb = _channel_dft(bw)
    x2 = x.reshape(B * n, D)
    c2 = ctx.reshape(B * nc, D)
    row_x = lambda r: r // n
    row_c = lambda r: B

    for l in range(depth):
        last = l == depth - 1
        w_in_l = _relayout_w_in(w_in[l])
        w_q2 = _relayout_w_uq(w_uq[l])
        w_ukv_l = w_ukv[l].astype(BF16)
        consts = dict(
            ccb=ccb, scb=scb, conv_w=conv_w[l], conv_b=conv_b[l].reshape(1, bw),
            ln_g=sgu_ln_g[l].reshape(1, bw), ln_b=sgu_ln_b[l].reshape(1, bw),
            w_s=sgu_w[l].astype(BF16),
            b_s=jnp.broadcast_to(sgu_b[l][:, :, None], sgu_b.shape[1:] + (bw // sgu_w.shape[1],)),
            w_branch=w_branch[l].astype(BF16), w_out=w_out[l].astype(BF16), post_g=post_g[l].reshape(1, D))

        px = _in_projection(x2, mod[l], row_x, pre_g[l], w_in_l, n)
        pc = _in_projection(c2, mod[l], row_c, pre_g[l], w_in_l[:, :HEAD_W] if last else w_in_l, nc)
        q_c, k_c, v_c = _mla_prep(pc, cs_c, kv_norm_g[l], q_norm_g[l], w_ukv_l, w_q2, B, nc)
        q_x, k_x, v_x = _mla_prep(px, cs_x, kv_norm_g[l], q_norm_g[l], w_ukv_l, w_q2, B, n)

        attn_x = _attention(q_x, k_c, v_c, k_x, v_x).reshape(B * n, -1)
        four_x = _position_dft(px, HEAD_W // bw + 1, bw, B, n)
        new_x = _mix(px, attn_x, four_x, x2, mod[l], row_x, consts, B, n)
        if not last:
            attn_c = _attention(q_c, k_c, v_c).reshape(B * nc, -1)
            four_c = _position_dft(pc, HEAD_W // bw + 1, bw, B, nc)
            c2 = _mix(pc, attn_c, four_c, c2, mod[l], row_c, consts, B, nc)
        x2 = new_x
    return x2.reshape(B, n, D)
```

```python
import functools

import jax
import jax.numpy as jnp
import numpy as np
from jax import lax
from jax.experimental import pallas as pl
from jax.experimental.pallas import tpu as pltpu

F32 = jnp.float32
BF16 = jnp.bfloat16

MLA_HEADS = 4
QK_NOPE = 128
QK_ROPE = 64
V_HEAD = 128
GRID_W = 64
ROPE_THETA = 10000.0
FOURIER_GROUPS = 4
CHUNK = 128
N_BRANCH = 4
N_BRANCH_COLS = 10
EPS = 1e-6

ROPE_OFF = 256
CQ_OFF = 512
HEAD_W = 1024
QK_W = QK_NOPE + 2 * QK_ROPE

FFT_N1 = 16

V7X_VMEM_BYTES = 64 * 2**20
VMEM_CAP_BYTES = V7X_VMEM_BYTES - 8 * 2**20


def _nbytes(shape, dtype):
    return int(np.prod(shape)) * jnp.dtype(dtype).itemsize


def _compiler_params(semantics, block_bytes, scratch_bytes=0, temp_bytes=0):
    need = 2 * block_bytes + scratch_bytes + temp_bytes + 2 * 2**20
    return pltpu.CompilerParams(dimension_semantics=semantics,
                                vmem_limit_bytes=int(min(max(need, 16 * 2**20), VMEM_CAP_BYTES)))


def _silu(v):
    return v * jax.nn.sigmoid(v)


def _rms(v, g):
    return v * lax.rsqrt(jnp.mean(v * v, axis=-1, keepdims=True) + EPS) * g


def _dot(a, b):
    return jnp.dot(a, b, preferred_element_type=F32)


def _mod_kernel(c_ref, w_ref, b_ref, o_ref):
    s = _silu(c_ref[...]).astype(BF16)
    o_ref[0] = _dot(s, w_ref[0].astype(BF16)) + b_ref[0]


def _modulation(cc, w_mod, b_mod):
    L, D, W = w_mod.shape
    tn = 768 if W % 768 == 0 else W
    blocks = _nbytes((D, tn), F32) + _nbytes((8, D), F32) + 2 * _nbytes((8, tn), F32)
    return pl.pallas_call(
        _mod_kernel,
        out_shape=jax.ShapeDtypeStruct((L, 8, W), F32),
        grid=(L, W // tn),
        in_specs=[pl.BlockSpec((8, D), lambda l, j: (0, 0)),
                  pl.BlockSpec((1, D, tn), lambda l, j: (l, 0, j)),
                  pl.BlockSpec((1, 1, tn), lambda l, j: (l, 0, j))],
        out_specs=pl.BlockSpec((1, 8, tn), lambda l, j: (l, 0, j)),
        compiler_params=_compiler_params(("parallel", "parallel"), blocks, 0, _nbytes((D, tn), BF16)),
        name="modulation",
    )(cc, w_mod, b_mod.reshape(L, 1, W))


def _inproj_kernel(*refs, with_rest):
    if with_rest:
        x_ref, mod_ref, g_ref, wh_ref, wr_ref, oh_ref, or_ref, h_ref = refs
    else:
        x_ref, mod_ref, g_ref, wh_ref, oh_ref, h_ref = refs
    j = pl.program_id(1)

    @pl.when(j == 0)
    def _():
        y = _rms(x_ref[...], g_ref[...])
        h_ref[...] = (y * (1.0 + mod_ref[0, 1:2, :]) + mod_ref[0, 0:1, :]).astype(BF16)
        oh_ref[...] = _dot(h_ref[...], wh_ref[...]).astype(oh_ref.dtype)

    if with_rest:
        @pl.when(j > 0)
        def _():
            or_ref[...] = _dot(h_ref[...], wr_ref[...]).astype(or_ref.dtype)


def _in_projection(x2, mod3, mod_row, pre_g, w_head, w_rest, tm):
    R, D = x2.shape
    tn = w_head.shape[1]
    with_rest = w_rest is not None
    nj = 1 + (w_rest.shape[1] // tn if with_rest else 0)
    rest_col = lambda j: jnp.maximum(j - 1, 0)
    in_specs = [pl.BlockSpec((tm, D), lambda i, j: (i, 0)),
                pl.BlockSpec((1, 3, D), lambda i, j: (mod_row(i * tm), 0, 0)),
                pl.BlockSpec((1, D), lambda i, j: (0, 0)),
                pl.BlockSpec((D, tn), lambda i, j: (0, 0))]
    out_specs = [pl.BlockSpec((tm, tn), lambda i, j: (i, 0))]
    out_shape = [jax.ShapeDtypeStruct((R, tn), BF16)]
    args = [x2, mod3, pre_g.reshape(1, D), w_head]
    if with_rest:
        in_specs.append(pl.BlockSpec((D, tn), lambda i, j: (0, rest_col(j))))
        out_specs.append(pl.BlockSpec((tm, tn), lambda i, j: (i, rest_col(j))))
        out_shape.append(jax.ShapeDtypeStruct((R, w_rest.shape[1]), BF16))
        args.append(w_rest)
    blocks = (_nbytes((tm, D), F32) + _nbytes((3, D), F32) + 2 * _nbytes((D, tn), BF16)
              + 2 * _nbytes((tm, tn), BF16))
    return pl.pallas_call(
        functools.partial(_inproj_kernel, with_rest=with_rest),
        out_shape=out_shape,
        grid=(R // tm, nj),
        in_specs=in_specs,
        out_specs=out_specs,
        scratch_shapes=[pltpu.VMEM((tm, D), BF16)],
        compiler_params=_compiler_params(("parallel", "arbitrary"), blocks, _nbytes((tm, D), BF16),
                                         2 * _nbytes((tm, D), F32) + _nbytes((tm, tn), F32)),
        name="in_projection",
    )(*args)


def _rope_pair(group, cs):
    r = group * cs
    return r + pltpu.roll(r, QK_ROPE, axis=1)


def _prep_kernel(p_ref, cs_ref, kvg_ref, qg_ref, wkv_ref, wq_ref, q_ref, k_ref, v_ref, *, scale):
    cs = cs_ref[...]
    kvn = _rms(p_ref[:, 0:ROPE_OFF].astype(F32), kvg_ref[...]).astype(BF16)
    kv = _dot(kvn, wkv_ref[...])
    kr = _rope_pair(p_ref[:, ROPE_OFF:ROPE_OFF + 2 * QK_ROPE].astype(F32), cs)
    lane = lax.broadcasted_iota(jnp.int32, kr.shape, 1)
    kr = jnp.where(lane < QK_ROPE, kr, 0.0).astype(BF16)
    cqn = _rms(p_ref[:, CQ_OFF:HEAD_W].astype(F32), qg_ref[...]).astype(BF16)
    q = _dot(cqn, wq_ref[...]) * scale
    kvw = QK_NOPE + V_HEAD
    for h in range(MLA_HEADS):
        k_ref[0, h, :, 0:QK_NOPE] = kv[:, h * kvw:h * kvw + QK_NOPE].astype(BF16)
        k_ref[0, h, :, QK_NOPE:QK_W] = kr
        v_ref[0, h] = kv[:, h * kvw + QK_NOPE:(h + 1) * kvw].astype(BF16)
        q_ref[0, h, :, 0:QK_NOPE] = q[:, h * QK_W:h * QK_W + QK_NOPE].astype(BF16)
        q_ref[0, h, :, QK_NOPE:QK_W] = _rope_pair(q[:, h * QK_W + QK_NOPE:(h + 1) * QK_W], cs).astype(BF16)


def _mla_prep(p, cs, kv_norm_g, q_norm_g, w_ukv, w_q2, B, n):
    tm = min(512, n)
    nt = n // tm
    H = MLA_HEADS
    kv_lora, q_lora = w_ukv.shape[0], w_q2.shape[0]
    blocks = (_nbytes((tm, HEAD_W), BF16) + _nbytes((tm, 128), F32) + _nbytes(w_ukv.shape, BF16)
              + _nbytes(w_q2.shape, BF16) + 2 * _nbytes((H, tm, QK_W), BF16) + _nbytes((H, tm, V_HEAD), BF16))
    qk_spec = pl.BlockSpec((1, H, tm, QK_W), lambda b, i: (b, 0, i, 0))
    return pl.pallas_call(
        functools.partial(_prep_kernel, scale=float(QK_NOPE + QK_ROPE) ** -0.5),
        out_shape=(jax.ShapeDtypeStruct((B, H, n, QK_W), BF16),
                   jax.ShapeDtypeStruct((B, H, n, QK_W), BF16),
                   jax.ShapeDtypeStruct((B, H, n, V_HEAD), BF16)),
        grid=(B, nt),
        in_specs=[pl.BlockSpec((tm, HEAD_W), lambda b, i: (b * nt + i, 0)),
                  pl.BlockSpec((tm, 128), lambda b, i: (i, 0)),
                  pl.BlockSpec((1, kv_lora), lambda b, i: (0, 0)),
                  pl.BlockSpec((1, q_lora), lambda b, i: (0, 0)),
                  pl.BlockSpec(w_ukv.shape, lambda b, i: (0, 0)),
                  pl.BlockSpec(w_q2.shape, lambda b, i: (0, 0))],
        out_specs=(qk_spec, qk_spec, pl.BlockSpec((1, H, tm, V_HEAD), lambda b, i: (b, 0, i, 0))),
        compiler_params=_compiler_params(("parallel", "parallel"), blocks, 0, 6 * _nbytes((tm, HEAD_W), F32)),
        name="mla_prep",
    )(p, cs, kv_norm_g.reshape(1, -1), q_norm_g.reshape(1, -1), w_ukv, w_q2)


_NT = (((1,), (1,)), ((), ()))


def _attn_kernel(*refs, with_x):
    if with_x:
        q_ref, kc_ref, vc_ref, kx_ref, vx_ref, o_ref = refs
    else:
        q_ref, kc_ref, vc_ref, o_ref = refs
    q = q_ref[0, 0]
    sc = lax.dot_general(q, kc_ref[0, 0], _NT, preferred_element_type=F32)
    m = jnp.max(sc, axis=-1, keepdims=True)
    if with_x:
        sx = lax.dot_general(q, kx_ref[0, 0], _NT, preferred_element_type=F32)
        m = jnp.maximum(m, jnp.max(sx, axis=-1, keepdims=True))
    pc = jnp.exp(sc - m)
    l = jnp.sum(pc, axis=-1, keepdims=True)
    o = _dot(pc.astype(BF16), vc_ref[0, 0])
    if with_x:
        px = jnp.exp(sx - m)
        l = l + jnp.sum(px, axis=-1, keepdims=True)
        o = o + _dot(px.astype(BF16), vx_ref[0, 0])
    o_ref[0] = (o / l).astype(o_ref.dtype)


def _attention(q, kc, vc, kx=None, vx=None):
    B, H, nq, _ = q.shape
    nc = kc.shape[2]
    with_x = kx is not None
    nk = nc + (kx.shape[2] if with_x else 0)
    tq = min(256, nq)
    q_spec = pl.BlockSpec((1, 1, tq, QK_W), lambda b, h, i: (b, h, i, 0))

    def full(arr):
        return pl.BlockSpec((1, 1) + arr.shape[2:], lambda b, h, i: (b, h, 0, 0))

    args = [q, kc, vc] + ([kx, vx] if with_x else [])
    blocks = _nbytes((tq, QK_W), BF16) + _nbytes((nk, QK_W + V_HEAD), BF16) + _nbytes((tq, V_HEAD), BF16)
    return pl.pallas_call(
        functools.partial(_attn_kernel, with_x=with_x),
        out_shape=jax.ShapeDtypeStruct((B, nq, H * V_HEAD), BF16),
        grid=(B, H, nq // tq),
        in_specs=[q_spec] + [full(a) for a in args[1:]],
        out_specs=pl.BlockSpec((1, tq, V_HEAD), lambda b, h, i: (b, i, h)),
        compiler_params=_compiler_params(("parallel", "parallel", "parallel"), blocks, 0,
                                         3 * _nbytes((tq, nk), F32)),
        name="attention",
    )(*args)


def _fft1_kernel(u_ref, m_ref, tc_ref, ts_ref, o_ref, *, reps):
    rows = FFT_N1 * FFT_N1
    u = u_ref[0, :, 0].reshape(rows, u_ref.shape[-1])
    g = _dot(m_ref[...], u)
    gr, gi = g[:rows], g[rows:]
    tc = jnp.concatenate([tc_ref[:, 0].reshape(rows, 128)] * reps, axis=1)
    ts = jnp.concatenate([ts_ref[:, 0].reshape(rows, 128)] * reps, axis=1)
    shape3 = (FFT_N1, FFT_N1, u_ref.shape[-1])
    o_ref[0, 0, :, 0] = (gr * tc + gi * ts).reshape(shape3).astype(o_ref.dtype)
    o_ref[0, 1, :, 0] = (gi * tc - gr * ts).reshape(shape3).astype(o_ref.dtype)


def _fft2_kernel(g_ref, fa_ref, fb_ref, o_ref, *, n2):
    y = _dot(fa_ref[...], g_ref[0, 0, 0]) + _dot(fb_ref[...], g_ref[0, 1, 0])
    o_ref[0, 0, 0] = y[:n2].astype(o_ref.dtype)
    o_ref[0, 1, 0] = y[n2:].astype(o_ref.dtype)


def _dft_tables(n):
    n1 = FFT_N1
    n2 = n // n1
    a = np.arange(n1)
    ang1 = 2.0 * np.pi * ((a[:, None] * a[None, :]) % n1) / n1
    eye = np.eye(n1)
    m1 = np.concatenate([np.kron(np.cos(ang1), eye), -np.kron(np.sin(ang1), eye)], axis=0)
    b = np.arange(n2)
    angt = 2.0 * np.pi * (a[:, None] * b[None, :]) / n
    shape4 = (n1, n2 // n1, n1, 128)
    tc = np.broadcast_to(np.cos(angt)[:, :, None], (n1, n2, 128)).reshape(shape4)
    ts = np.broadcast_to(np.sin(angt)[:, :, None], (n1, n2, 128)).reshape(shape4)
    ang2 = 2.0 * np.pi * ((b[:, None] * b[None, :]) % n2) / n2
    c2, s2 = np.cos(ang2), np.sin(ang2)
    fa = np.concatenate([c2, -s2], axis=0)
    fb = np.concatenate([s2, c2], axis=0)
    to_bf16 = lambda t: jnp.asarray(t, F32).astype(BF16)
    return to_bf16(m1), jnp.asarray(tc, F32), jnp.asarray(ts, F32), to_bf16(fa), to_bf16(fb)


def _position_dft(p, col_block, C, B, n):
    n1 = FFT_N1
    n2 = n // n1
    nb = n2 // n1
    m1, tc, ts, fa, fb = _dft_tables(n)
    p5 = p.reshape(B, n1, nb, n1, p.shape[1])
    blocks = (3 * _nbytes((n1 * n1, C), BF16) + _nbytes(m1.shape, BF16) + 2 * _nbytes((n1 * n1, 128), F32))
    g = pl.pallas_call(
        functools.partial(_fft1_kernel, reps=C // 128),
        out_shape=jax.ShapeDtypeStruct((B, 2, n1, nb, n1, C), BF16),
        grid=(B, nb),
        in_specs=[pl.BlockSpec((1, n1, 1, n1, C), lambda b, r: (b, 0, r, 0, col_block)),
                  pl.BlockSpec(m1.shape, lambda b, r: (0, 0)),
                  pl.BlockSpec((n1, 1, n1, 128), lambda b, r: (0, r, 0, 0)),
                  pl.BlockSpec((n1, 1, n1, 128), lambda b, r: (0, r, 0, 0))],
        out_specs=pl.BlockSpec((1, 2, n1, 1, n1, C), lambda b, r: (b, 0, 0, r, 0, 0)),
        compiler_params=_compiler_params(("parallel", "parallel"), blocks, 0, 8 * _nbytes((n1 * n1, C), F32)),
        name="position_dft_stage1",
    )(p5, m1, tc, ts)
    g = g.reshape(B, 2, n1, n2, C)
    blocks = 4 * _nbytes((n2, C), BF16) + 2 * _nbytes(fa.shape, BF16)
    return pl.pallas_call(
        functools.partial(_fft2_kernel, n2=n2),
        out_shape=jax.ShapeDtypeStruct((B, 2, n1, n2, C), BF16),
        grid=(B, n1),
        in_specs=[pl.BlockSpec((1, 2, 1, n2, C), lambda b, d: (b, 0, d, 0, 0)),
                  pl.BlockSpec(fa.shape, lambda b, d: (0, 0)),
                  pl.BlockSpec(fb.shape, lambda b, d: (0, 0))],
        out_specs=pl.BlockSpec((1, 2, 1, n2, C), lambda b, d: (b, 0, d, 0, 0)),
        compiler_params=_compiler_params(("parallel", "parallel"), blocks, 0, 4 * _nbytes((n2, C), F32)),
        name="position_dft_stage2",
    )(g, fa, fb)


def _mix_kernel(attn_ref, vr_ref, vi_ref, zA_ref, zB_ref, xC_ref, bC_ref, cC_ref, zC_ref, uD_ref, vD_ref,
                zD_ref, xCp_ref, cCp_ref, xCn_ref, cCn_ref, g0_ref, g1_ref, g2_ref, g3_ref, x_ref, mod_ref,
                ccb_ref, scb_ref, convw_ref, convb_ref, lng_ref, lnb_ref, ws_ref, bsb_ref, wb_ref, wo_ref,
                postg_ref, o_ref, ys_ref, acc_ref, fr_ref, fi_ref, *, n, tm, halo, fscale):
    i = pl.program_id(0)
    j = pl.program_id(1)

    @pl.when(j == 0)
    def _():
        def gate(ref):
            return _silu(ref[...].astype(F32))

        ys_ref[0] = (attn_ref[...].astype(F32) * gate(zA_ref)).astype(BF16)
        gwf = ccb_ref.shape[0]
        for d in range(FFT_N1):
            rows = pl.ds(d, tm // FFT_N1, stride=FFT_N1)
            vr, vi = vr_ref[0, 0, d].astype(F32), vi_ref[0, 0, d].astype(F32)
            for g in range(fr_ref.shape[0]):
                fr_ref[g, rows, :] = vr[:, g * gwf:(g + 1) * gwf]
                fi_ref[g, rows, :] = vi[:, g * gwf:(g + 1) * gwf]
        four = jnp.concatenate(
            [_dot(fr_ref[g].astype(BF16), ccb_ref[...]) + _dot(fi_ref[g].astype(BF16), scb_ref[...])
             for g in range(fr_ref.shape[0])], axis=1)
        ys_ref[1] = (four * fscale * gate(zB_ref)).astype(BF16)
        u = cC_ref[...].astype(F32) * xC_ref[...].astype(F32)
        first = (i * tm) % n == 0
        last = ((i + 1) * tm) % n == 0
        up = (cCp_ref[...].astype(F32) * xCp_ref[...].astype(F32))[halo - 1:halo, :]
        un = (cCn_ref[...].astype(F32) * xCn_ref[...].astype(F32))[0:1, :]
        up = up * jnp.where(first, 0.0, 1.0)
        un = un * jnp.where(last, 0.0, 1.0)
        row = lax.broadcasted_iota(jnp.int32, u.shape, 0)
        prev = jnp.where(row == 0, up, pltpu.roll(u, 1, axis=0))
        nxt = jnp.where(row == tm - 1, un, pltpu.roll(u, tm - 1, axis=0))
        conv = prev * convw_ref[0:1, :] + u * convw_ref[1:2, :] + nxt * convw_ref[2:3, :] + convb_ref[...]
        ys_ref[2] = (bC_ref[...].astype(F32) * conv * gate(zC_ref)).astype(BF16)
        v = vD_ref[...].astype(F32)
        vc = v - jnp.mean(v, axis=-1, keepdims=True)
        vn = vc * lax.rsqrt(jnp.mean(vc * vc, axis=-1, keepdims=True) + EPS) * lng_ref[...] + lnb_ref[...]
        vn = vn.astype(BF16)
        ug = uD_ref[...].astype(F32) * gate(zD_ref)
        gw = vn.shape[1] // ws_ref.shape[0]
        for g in range(ws_ref.shape[0]):
            for c in range(tm // CHUNK):
                rs, cs = slice(c * CHUNK, (c + 1) * CHUNK), slice(g * gw, (g + 1) * gw)
                mixed = _dot(ws_ref[g], vn[rs, cs]) + bsb_ref[g]
                ys_ref[3, rs, cs] = (ug[rs, cs] * mixed).astype(BF16)
        acc_ref[...] = jnp.zeros_like(acc_ref)

    merged = None
    for g, g_ref in enumerate((g0_ref, g1_ref, g2_ref, g3_ref)):
        term = jax.nn.sigmoid(g_ref[...].astype(F32)) * _dot(ys_ref[g], wb_ref[g])
        merged = term if merged is None else merged + term
    acc_ref[...] += _dot(merged.astype(BF16), wo_ref[...])

    @pl.when(j == pl.num_programs(1) - 1)
    def _():
        o_ref[...] = x_ref[...] + mod_ref[0, 2:3, :] * _rms(acc_ref[...], postg_ref[...])


def _mix(p, attn, v, x2, mod3, mod_row, consts, B, n):
    R, D = x2.shape
    bw = attn.shape[1]
    tm = min(512, n)
    tn = min(512, D)
    halo = 16
    nt = n // tm
    cl = tm // FFT_N1
    gate0 = N_BRANCH_COLS * bw // tn

    def seg(k):
        return pl.BlockSpec((tm, bw), lambda i, j: (i, k))

    def halo_prev(k):
        return pl.BlockSpec((halo, bw), lambda i, j: (jnp.maximum(i * (tm // halo) - 1, 0), k))

    def halo_next(k):
        return pl.BlockSpec((halo, bw), lambda i, j: (jnp.minimum((i + 1) * (tm // halo), R // halo - 1), k))

    def gates(g):
        return pl.BlockSpec((tm, tn), lambda i, j: (i, gate0 + g * (D // tn) + j))

    def const(arr):
        nd = arr.ndim
        return pl.BlockSpec(arr.shape, lambda i, j: (0,) * nd)

    vspec_r = pl.BlockSpec((1, 1, FFT_N1, cl, bw), lambda i, j: (i // nt, 0, 0, i % nt, 0))
    vspec_i = pl.BlockSpec((1, 1, FFT_N1, cl, bw), lambda i, j: (i // nt, 1, 0, i % nt, 0))
    in_specs = ([pl.BlockSpec((tm, bw), lambda i, j: (i, 0)), vspec_r, vspec_i]
                + [seg(k) for k in (0, 2, 3, 4, 5, 6, 7, 8, 9)]
                + [halo_prev(3), halo_prev(5), halo_next(3), halo_next(5)]
                + [gates(g) for g in range(N_BRANCH)]
                + [pl.BlockSpec((tm, D), lambda i, j: (i, 0)),
                   pl.BlockSpec((1, 3, D), lambda i, j: (mod_row(i * tm), 0, 0))]
                + [const(consts[k]) for k in ("ccb", "scb", "conv_w", "conv_b", "ln_g", "ln_b", "w_s", "b_s")]
                + [pl.BlockSpec((N_BRANCH, bw, tn), lambda i, j: (0, 0, j)),
                   pl.BlockSpec((tn, D), lambda i, j: (j, 0)),
                   const(consts["post_g"])])
    args = ([attn, v, v] + [p] * 9 + [p] * 4 + [p] * 4 + [x2, mod3]
            + [consts[k] for k in ("ccb", "scb", "conv_w", "conv_b", "ln_g", "ln_b", "w_s", "b_s")]
            + [consts["w_branch"], consts["w_out"], consts["post_g"]])
    blocks = (12 * _nbytes((tm, bw), BF16) + 4 * _nbytes((halo, bw), BF16) + N_BRANCH * _nbytes((tm, tn), BF16)
              + 2 * _nbytes((tm, D), F32) + 2 * _nbytes((bw, bw), BF16) + N_BRANCH * _nbytes((CHUNK, CHUNK), F32) * 2
              + _nbytes((N_BRANCH, bw, tn), BF16) + _nbytes((tn, D), BF16))
    scratch = _nbytes((N_BRANCH, tm, bw), BF16) + _nbytes((tm, D), F32) + 2 * _nbytes((tm, bw), F32)
    return pl.pallas_call(
        functools.partial(_mix_kernel, n=n, tm=tm, halo=halo,
                          fscale=float(n * (bw // FOURIER_GROUPS)) ** -0.5),
        out_shape=jax.ShapeDtypeStruct((R, D), F32),
        grid=(R // tm, D // tn),
        in_specs=in_specs,
        out_specs=pl.BlockSpec((tm, D), lambda i, j: (i, 0)),
        scratch_shapes=[pltpu.VMEM((N_BRANCH, tm, bw), BF16), pltpu.VMEM((tm, D), F32),
                        pltpu.VMEM((FOURIER_GROUPS, tm, bw // FOURIER_GROUPS), F32),
                        pltpu.VMEM((FOURIER_GROUPS, tm, bw // FOURIER_GROUPS), F32)],
        compiler_params=_compiler_params(("parallel", "arbitrary"), blocks, scratch,
                                         8 * _nbytes((tm, bw), F32) + 2 * _nbytes((tm, D), F32)),
        name="branch_mix",
    )(*args)


def _swap_halves(w):
    h = w.shape[-1] // 2
    return jnp.concatenate([w[..., h:], w[..., :h]], axis=-1)


def _relayout_w_in(w):
    rope_end = ROPE_OFF + QK_ROPE
    head_end = rope_end + HEAD_W - CQ_OFF
    rope = w[:, ROPE_OFF:rope_end]
    pad = jnp.zeros((w.shape[0], CQ_OFF - ROPE_OFF - 2 * QK_ROPE), w.dtype)
    head = jnp.concatenate([w[:, :ROPE_OFF], rope, _swap_halves(rope), pad, w[:, rope_end:head_end]], axis=1)
    return head.astype(BF16), w[:, head_end:].astype(BF16)


def _relayout_w_uq(w):
    w = w.reshape(w.shape[0], MLA_HEADS, QK_NOPE + QK_ROPE)
    rope = w[..., QK_NOPE:]
    return jnp.concatenate([w[..., :QK_NOPE], rope, _swap_halves(rope)], axis=-1).reshape(w.shape[0], -1).astype(BF16)


def _rope_table(n, rotate):
    half = QK_ROPE // 2
    if rotate:
        pos = np.arange(n)
        inv = ROPE_THETA ** (-np.arange(0, half, 2, dtype=np.float64) / half)
        ang = np.concatenate([(pos // GRID_W)[:, None] * inv, (pos % GRID_W)[:, None] * inv], axis=-1)
    else:
        ang = np.zeros((n, half))
    cos, sin = np.cos(ang), np.sin(ang)
    return jnp.asarray(np.concatenate([cos, cos, -sin, sin], axis=-1), F32)


def _channel_dft(bw):
    gw = bw // FOURIER_GROUPS
    a = np.arange(gw)
    ang = 2.0 * np.pi * ((a[:, None] * a[None, :]) % gw) / gw
    return jnp.asarray(np.cos(ang), F32).astype(BF16), jnp.asarray(np.sin(ang), F32).astype(BF16)


def kernel(x, c, ctx, c_ctx, w_mod, b_mod, pre_g, post_g, w_in, q_norm_g, kv_norm_g, w_uq, w_ukv,
           conv_w, conv_b, sgu_ln_g, sgu_ln_b, sgu_w, sgu_b, w_branch, w_out):
    B, n, D = x.shape
    nc = ctx.shape[1]
    depth = w_in.shape[0]
    bw = w_branch.shape[2]
    assert w_in.shape[2] == ROPE_OFF + QK_ROPE + (HEAD_W - CQ_OFF) + N_BRANCH_COLS * bw + N_BRANCH * D
    assert kv_norm_g.shape[1] == ROPE_OFF and q_norm_g.shape[1] == HEAD_W - CQ_OFF and B <= 4

    cc = jnp.zeros((8, D), F32).at[:B].set(c).at[B].set(c_ctx)
    mod = _modulation(cc, w_mod, b_mod).reshape(depth, 8, 3, D)

    cs_x, cs_c = _rope_table(n, True), _rope_table(nc, False)
    ccb, scb = _channel_dft(bw)
    x2 = x.reshape(B * n, D)
    c2 = ctx.reshape(B * nc, D)
    row_x = lambda r: r // n
    row_c = lambda r: B

    for l in range(depth):
        last = l == depth - 1
        w_head, w_rest = _relayout_w_in(w_in[l])
        w_q2 = _relayout_w_uq(w_uq[l])
        w_ukv_l = w_ukv[l].astype(BF16)
        consts = dict(
            ccb=ccb, scb=scb, conv_w=conv_w[l], conv_b=conv_b[l].reshape(1, bw),
            ln_g=sgu_ln_g[l].reshape(1, bw), ln_b=sgu_ln_b[l].reshape(1, bw),
            w_s=sgu_w[l].astype(BF16),
            b_s=jnp.broadcast_to(sgu_b[l][:, :, None], sgu_b.shape[1:] + (bw // sgu_w.shape[1],)),
            w_branch=w_branch[l].astype(BF16), w_out=w_out[l].astype(BF16), post_g=post_g[l].reshape(1, D))

        hx, px = _in_projection(x2, mod[l], row_x, pre_g[l], w_head, w_rest, min(1024, n))
        hc, *pc = _in_projection(c2, mod[l], row_c, pre_g[l], w_head, None if last else w_rest,
                                 min(1024, B * nc))
        q_c, k_c, v_c = _mla_prep(hc, cs_c, kv_norm_g[l], q_norm_g[l], w_ukv_l, w_q2, B, nc)
        q_x, k_x, v_x = _mla_prep(hx, cs_x, kv_norm_g[l], q_norm_g[l], w_ukv_l, w_q2, B, n)

        attn_x = _attention(q_x, k_c, v_c, k_x, v_x).reshape(B * n, -1)
        four_x = _position_dft(px, 1, bw, B, n)
        new_x = _mix(px, attn_x, four_x, x2, mod[l], row_x, consts, B, n)
        if not last:
            attn_c = _attention(q_c, k_c, v_c).reshape(B * nc, -1)
            four_c = _position_dft(pc[0], 1, bw, B, nc)
            c2 = _mix(pc[0], attn_c, four_c, c2, mod[l], row_c, consts, B, nc)
        x2 = new_x
    return x2.reshape(B, n, D)
```

```python
import functools

import jax
import jax.numpy as jnp
import numpy as np
from jax import lax
from jax.experimental import pallas as pl
from jax.experimental.pallas import tpu as pltpu

F32 = jnp.float32
BF16 = jnp.bfloat16

MLA_HEADS = 4
QK_NOPE = 128
QK_ROPE = 64
V_HEAD = 128
GRID_W = 64
ROPE_THETA = 10000.0
FOURIER_GROUPS = 4
CHUNK = 128
N_BRANCH = 4
N_BRANCH_COLS = 10
EPS = 1e-6

ROPE_OFF = 256
CQ_OFF = 512
HEAD_W = 1024
QK_W = QK_NOPE + 2 * QK_ROPE

FFT_N1 = 16

V7X_VMEM_BYTES = 64 * 2**20
VMEM_CAP_BYTES = V7X_VMEM_BYTES - 8 * 2**20


def _nbytes(shape, dtype):
    return int(np.prod(shape)) * jnp.dtype(dtype).itemsize


def _compiler_params(semantics, block_bytes, scratch_bytes=0, temp_bytes=0):
    need = 2 * block_bytes + scratch_bytes + temp_bytes + 2 * 2**20
    return pltpu.CompilerParams(dimension_semantics=semantics,
                                vmem_limit_bytes=int(min(max(need, 16 * 2**20), VMEM_CAP_BYTES)))


def _sigmoid(v):
    return 0.5 * jnp.tanh(0.5 * v) + 0.5


def _silu(v):
    return v * _sigmoid(v)


def _rms(v, g):
    return v * lax.rsqrt(jnp.mean(v * v, axis=-1, keepdims=True) + EPS) * g


def _dot(a, b):
    return jnp.dot(a, b, preferred_element_type=F32)


def _mod_kernel(c_ref, w_ref, b_ref, o_ref):
    s = _silu(c_ref[...]).astype(BF16)
    o_ref[0] = _dot(s, w_ref[0].astype(BF16)) + b_ref[0]


def _modulation(cc, w_mod, b_mod):
    L, D, W = w_mod.shape
    tn = 768 if W % 768 == 0 else W
    blocks = _nbytes((D, tn), F32) + _nbytes((8, D), F32) + 2 * _nbytes((8, tn), F32)
    return pl.pallas_call(
        _mod_kernel,
        out_shape=jax.ShapeDtypeStruct((L, 8, W), F32),
        grid=(L, W // tn),
        in_specs=[pl.BlockSpec((8, D), lambda l, j: (0, 0)),
                  pl.BlockSpec((1, D, tn), lambda l, j: (l, 0, j)),
                  pl.BlockSpec((1, 1, tn), lambda l, j: (l, 0, j))],
        out_specs=pl.BlockSpec((1, 8, tn), lambda l, j: (l, 0, j)),
        compiler_params=_compiler_params(("parallel", "parallel"), blocks, 0, _nbytes((D, tn), BF16)),
        name="modulation",
    )(cc, w_mod, b_mod.reshape(L, 1, W))


def _inproj_kernel(*refs, with_rest):
    if with_rest:
        x_ref, mod_ref, g_ref, wh_ref, wr_ref, oh_ref, or_ref, h_ref = refs
    else:
        x_ref, mod_ref, g_ref, wh_ref, oh_ref, h_ref = refs
    j = pl.program_id(1)

    @pl.when(j == 0)
    def _():
        y = _rms(x_ref[...], g_ref[...])
        h_ref[...] = (y * (1.0 + mod_ref[0, 1:2, :]) + mod_ref[0, 0:1, :]).astype(BF16)
        oh_ref[...] = _dot(h_ref[...], wh_ref[...]).astype(oh_ref.dtype)

    if with_rest:
        @pl.when(j > 0)
        def _():
            or_ref[...] = _dot(h_ref[...], wr_ref[...]).astype(or_ref.dtype)


def _in_projection(x2, mod3, mod_row, pre_g, w_head, w_rest, tm):
    R, D = x2.shape
    tn = w_head.shape[1]
    with_rest = w_rest is not None
    nj = 1 + (w_rest.shape[1] // tn if with_rest else 0)
    rest_col = lambda j: jnp.maximum(j - 1, 0)
    in_specs = [pl.BlockSpec((tm, D), lambda i, j: (i, 0)),
                pl.BlockSpec((1, 3, D), lambda i, j: (mod_row(i * tm), 0, 0)),
                pl.BlockSpec((1, D), lambda i, j: (0, 0)),
                pl.BlockSpec((D, tn), lambda i, j: (0, 0))]
    out_specs = [pl.BlockSpec((tm, tn), lambda i, j: (i, 0))]
    out_shape = [jax.ShapeDtypeStruct((R, tn), BF16)]
    args = [x2, mod3, pre_g.reshape(1, D), w_head]
    if with_rest:
        in_specs.append(pl.BlockSpec((D, tn), lambda i, j: (0, rest_col(j))))
        out_specs.append(pl.BlockSpec((tm, tn), lambda i, j: (i, rest_col(j))))
        out_shape.append(jax.ShapeDtypeStruct((R, w_rest.shape[1]), BF16))
        args.append(w_rest)
    blocks = (_nbytes((tm, D), F32) + _nbytes((3, D), F32) + 2 * _nbytes((D, tn), BF16)
              + 2 * _nbytes((tm, tn), BF16))
    return pl.pallas_call(
        functools.partial(_inproj_kernel, with_rest=with_rest),
        out_shape=out_shape,
        grid=(R // tm, nj),
        in_specs=in_specs,
        out_specs=out_specs,
        scratch_shapes=[pltpu.VMEM((tm, D), BF16)],
        compiler_params=_compiler_params(("parallel", "arbitrary"), blocks, _nbytes((tm, D), BF16),
                                         2 * _nbytes((tm, D), F32) + _nbytes((tm, tn), F32)),
        name="in_projection",
    )(*args)


def _rope_pair(group, cs):
    r = group * cs
    return r + pltpu.roll(r, QK_ROPE, axis=1)


def _prep_kernel(p_ref, cs_ref, kvg_ref, qg_ref, wkv_ref, wq_ref, q_ref, k_ref, v_ref, *, scale):
    cs = cs_ref[...]
    kvn = _rms(p_ref[:, 0:ROPE_OFF].astype(F32), kvg_ref[...]).astype(BF16)
    kv = _dot(kvn, wkv_ref[...])
    kr = _rope_pair(p_ref[:, ROPE_OFF:ROPE_OFF + 2 * QK_ROPE].astype(F32), cs)
    lane = lax.broadcasted_iota(jnp.int32, kr.shape, 1)
    kr = jnp.where(lane < QK_ROPE, kr, 0.0).astype(BF16)
    cqn = _rms(p_ref[:, CQ_OFF:HEAD_W].astype(F32), qg_ref[...]).astype(BF16)
    q = _dot(cqn, wq_ref[...]) * scale
    kvw = QK_NOPE + V_HEAD
    for h in range(MLA_HEADS):
        k_ref[0, h, :, 0:QK_NOPE] = kv[:, h * kvw:h * kvw + QK_NOPE].astype(BF16)
        k_ref[0, h, :, QK_NOPE:QK_W] = kr
        v_ref[0, h, :, 0:V_HEAD] = kv[:, h * kvw + QK_NOPE:(h + 1) * kvw].astype(BF16)
        v_ref[0, h, :, V_HEAD:2 * V_HEAD] = jnp.ones((kv.shape[0], V_HEAD), BF16)
        q_ref[0, h, :, 0:QK_NOPE] = q[:, h * QK_W:h * QK_W + QK_NOPE].astype(BF16)
        q_ref[0, h, :, QK_NOPE:QK_W] = _rope_pair(q[:, h * QK_W + QK_NOPE:(h + 1) * QK_W], cs).astype(BF16)


def _mla_prep(p, cs, kv_norm_g, q_norm_g, w_ukv, w_q2, B, n):
    tm = min(512, n)
    nt = n // tm
    H = MLA_HEADS
    kv_lora, q_lora = w_ukv.shape[0], w_q2.shape[0]
    blocks = (_nbytes((tm, HEAD_W), BF16) + _nbytes((tm, 128), F32) + _nbytes(w_ukv.shape, BF16)
              + _nbytes(w_q2.shape, BF16) + 2 * _nbytes((H, tm, QK_W), BF16) + _nbytes((H, tm, 2 * V_HEAD), BF16))
    qk_spec = pl.BlockSpec((1, H, tm, QK_W), lambda b, i: (b, 0, i, 0))
    return pl.pallas_call(
        functools.partial(_prep_kernel, scale=float(QK_NOPE + QK_ROPE) ** -0.5),
        out_shape=(jax.ShapeDtypeStruct((B, H, n, QK_W), BF16),
                   jax.ShapeDtypeStruct((B, H, n, QK_W), BF16),
                   jax.ShapeDtypeStruct((B, H, n, 2 * V_HEAD), BF16)),
        grid=(B, nt),
        in_specs=[pl.BlockSpec((tm, HEAD_W), lambda b, i: (b * nt + i, 0)),
                  pl.BlockSpec((tm, 128), lambda b, i: (i, 0)),
                  pl.BlockSpec((1, kv_lora), lambda b, i: (0, 0)),
                  pl.BlockSpec((1, q_lora), lambda b, i: (0, 0)),
                  pl.BlockSpec(w_ukv.shape, lambda b, i: (0, 0)),
                  pl.BlockSpec(w_q2.shape, lambda b, i: (0, 0))],
        out_specs=(qk_spec, qk_spec, pl.BlockSpec((1, H, tm, 2 * V_HEAD), lambda b, i: (b, 0, i, 0))),
        compiler_params=_compiler_params(("parallel", "parallel"), blocks, 0, 6 * _nbytes((tm, HEAD_W), F32)),
        name="mla_prep",
    )(p, cs, kv_norm_g.reshape(1, -1), q_norm_g.reshape(1, -1), w_ukv, w_q2)


_NT = (((1,), (1,)), ((), ()))


def _attn_kernel(*refs, with_x, tk):
    if with_x:
        q_ref, kc_ref, vc_ref, kx_ref, vx_ref, o_ref = refs
    else:
        q_ref, kc_ref, vc_ref, o_ref = refs
    q = q_ref[0, 0]

    s = lax.dot_general(q, kc_ref[0, 0], _NT, preferred_element_type=F32)
    m = jnp.max(s, axis=-1, keepdims=True)
    acc = _dot(jnp.exp(s - m).astype(BF16), vc_ref[0, 0])
    if with_x:
        for c in range(kx_ref.shape[2] // tk):
            s = lax.dot_general(q, kx_ref[0, 0, c * tk:(c + 1) * tk, :], _NT, preferred_element_type=F32)
            m_new = jnp.maximum(m, jnp.max(s, axis=-1, keepdims=True))
            p = jnp.exp(s - m_new).astype(BF16)
            acc = acc * jnp.exp(m - m_new) + _dot(p, vx_ref[0, 0, c * tk:(c + 1) * tk, :])
            m = m_new
    o_ref[0] = (acc[:, :V_HEAD] / acc[:, V_HEAD:]).astype(o_ref.dtype)


def _attention(q, kc, vc, kx=None, vx=None):
    B, H, nq, _ = q.shape
    nc = kc.shape[2]
    with_x = kx is not None
    nk = nc + (kx.shape[2] if with_x else 0)
    tq = min(512, nq)
    tk = min(512, nq)
    q_spec = pl.BlockSpec((1, 1, tq, QK_W), lambda b, h, i: (b, h, i, 0))

    def full(arr):
        return pl.BlockSpec((1, 1) + arr.shape[2:], lambda b, h, i: (b, h, 0, 0))

    args = [q, kc, vc] + ([kx, vx] if with_x else [])
    blocks = _nbytes((tq, QK_W), BF16) + _nbytes((nk, QK_W + 2 * V_HEAD), BF16) + _nbytes((tq, V_HEAD), BF16)
    return pl.pallas_call(
        functools.partial(_attn_kernel, with_x=with_x, tk=tk),
        out_shape=jax.ShapeDtypeStruct((B, nq, H * V_HEAD), BF16),
        grid=(B, H, nq // tq),
        in_specs=[q_spec] + [full(a) for a in args[1:]],
        out_specs=pl.BlockSpec((1, tq, V_HEAD), lambda b, h, i: (b, i, h)),
        compiler_params=_compiler_params(("parallel", "parallel", "parallel"), blocks, 0,
                                         6 * _nbytes((tq, tk), F32)),
        name="attention",
    )(*args)


def _fft1_kernel(u_ref, m_ref, tc_ref, ts_ref, o_ref, *, reps):
    rows = FFT_N1 * FFT_N1
    u = u_ref[0, :, 0].reshape(rows, u_ref.shape[-1])
    g = _dot(m_ref[...], u)
    gr, gi = g[:rows], g[rows:]
    tc = jnp.concatenate([tc_ref[:, 0].reshape(rows, 128)] * reps, axis=1)
    ts = jnp.concatenate([ts_ref[:, 0].reshape(rows, 128)] * reps, axis=1)
    shape3 = (FFT_N1, FFT_N1, u_ref.shape[-1])
    o_ref[0, 0, :, 0] = (gr * tc + gi * ts).reshape(shape3).astype(o_ref.dtype)
    o_ref[0, 1, :, 0] = (gi * tc - gr * ts).reshape(shape3).astype(o_ref.dtype)


def _fft2_kernel(g_ref, fa_ref, fb_ref, o_ref, *, n2):
    y = _dot(fa_ref[...], g_ref[0, 0, 0]) + _dot(fb_ref[...], g_ref[0, 1, 0])
    o_ref[0, 0, 0] = y[:n2].astype(o_ref.dtype)
    o_ref[0, 1, 0] = y[n2:].astype(o_ref.dtype)


def _dft_tables(n):
    n1 = FFT_N1
    n2 = n // n1
    a = np.arange(n1)
    ang1 = 2.0 * np.pi * ((a[:, None] * a[None, :]) % n1) / n1
    eye = np.eye(n1)
    m1 = np.concatenate([np.kron(np.cos(ang1), eye), -np.kron(np.sin(ang1), eye)], axis=0)
    b = np.arange(n2)
    angt = 2.0 * np.pi * (a[:, None] * b[None, :]) / n
    shape4 = (n1, n2 // n1, n1, 128)
    tc = np.broadcast_to(np.cos(angt)[:, :, None], (n1, n2, 128)).reshape(shape4)
    ts = np.broadcast_to(np.sin(angt)[:, :, None], (n1, n2, 128)).reshape(shape4)
    ang2 = 2.0 * np.pi * ((b[:, None] * b[None, :]) % n2) / n2
    c2, s2 = np.cos(ang2), np.sin(ang2)
    fa = np.concatenate([c2, -s2], axis=0)
    fb = np.concatenate([s2, c2], axis=0)
    to_bf16 = lambda t: jnp.asarray(t, F32).astype(BF16)
    return to_bf16(m1), jnp.asarray(tc, F32), jnp.asarray(ts, F32), to_bf16(fa), to_bf16(fb)


def _position_dft(p, col_block, C, B, n):
    n1 = FFT_N1
    n2 = n // n1
    nb = n2 // n1
    m1, tc, ts, fa, fb = _dft_tables(n)
    p5 = p.reshape(B, n1, nb, n1, p.shape[1])
    blocks = (3 * _nbytes((n1 * n1, C), BF16) + _nbytes(m1.shape, BF16) + 2 * _nbytes((n1 * n1, 128), F32))
    g = pl.pallas_call(
        functools.partial(_fft1_kernel, reps=C // 128),
        out_shape=jax.ShapeDtypeStruct((B, 2, n1, nb, n1, C), BF16),
        grid=(B, nb),
        in_specs=[pl.BlockSpec((1, n1, 1, n1, C), lambda b, r: (b, 0, r, 0, col_block)),
                  pl.BlockSpec(m1.shape, lambda b, r: (0, 0)),
                  pl.BlockSpec((n1, 1, n1, 128), lambda b, r: (0, r, 0, 0)),
                  pl.BlockSpec((n1, 1, n1, 128), lambda b, r: (0, r, 0, 0))],
        out_specs=pl.BlockSpec((1, 2, n1, 1, n1, C), lambda b, r: (b, 0, 0, r, 0, 0)),
        compiler_params=_compiler_params(("parallel", "parallel"), blocks, 0, 8 * _nbytes((n1 * n1, C), F32)),
        name="position_dft_stage1",
    )(p5, m1, tc, ts)
    g = g.reshape(B, 2, n1, n2, C)
    blocks = 4 * _nbytes((n2, C), BF16) + 2 * _nbytes(fa.shape, BF16)
    return pl.pallas_call(
        functools.partial(_fft2_kernel, n2=n2),
        out_shape=jax.ShapeDtypeStruct((B, 2, n1, n2, C), BF16),
        grid=(B, n1),
        in_specs=[pl.BlockSpec((1, 2, 1, n2, C), lambda b, d: (b, 0, d, 0, 0)),
                  pl.BlockSpec(fa.shape, lambda b, d: (0, 0)),
                  pl.BlockSpec(fb.shape, lambda b, d: (0, 0))],
        out_specs=pl.BlockSpec((1, 2, 1, n2, C), lambda b, d: (b, 0, d, 0, 0)),
        compiler_params=_compiler_params(("parallel", "parallel"), blocks, 0, 4 * _nbytes((n2, C), F32)),
        name="position_dft_stage2",
    )(g, fa, fb)


def _mix_kernel(attn_ref, vr_ref, vi_ref, zA_ref, zB_ref, xC_ref, bC_ref, cC_ref, zC_ref, uD_ref, vD_ref,
                zD_ref, xCp_ref, cCp_ref, xCn_ref, cCn_ref, g0_ref, g1_ref, g2_ref, g3_ref, x_ref, mod_ref,
                ccb_ref, scb_ref, convw_ref, convb_ref, lng_ref, lnb_ref, ws_ref, bsb_ref, wb_ref, wo_ref,
                postg_ref, o_ref, ys_ref, acc_ref, fr_ref, fi_ref, *, n, tm, halo, fscale):
    i = pl.program_id(0)
    j = pl.program_id(1)

    @pl.when(j == 0)
    def _():
        def gate(ref):
            return _silu(ref[...].astype(F32))

        ys_ref[0] = (attn_ref[...].astype(F32) * gate(zA_ref)).astype(BF16)
        gwf = ccb_ref.shape[0]
        for d in range(FFT_N1):
            rows = pl.ds(d, tm // FFT_N1, stride=FFT_N1)
            vr, vi = vr_ref[0, 0, d].astype(F32), vi_ref[0, 0, d].astype(F32)
            for g in range(fr_ref.shape[0]):
                fr_ref[g, rows, :] = vr[:, g * gwf:(g + 1) * gwf]
                fi_ref[g, rows, :] = vi[:, g * gwf:(g + 1) * gwf]
        four = jnp.concatenate(
            [_dot(fr_ref[g].astype(BF16), ccb_ref[...]) + _dot(fi_ref[g].astype(BF16), scb_ref[...])
             for g in range(fr_ref.shape[0])], axis=1)
        ys_ref[1] = (four * fscale * gate(zB_ref)).astype(BF16)
        u = cC_ref[...].astype(F32) * xC_ref[...].astype(F32)
        first = (i * tm) % n == 0
        last = ((i + 1) * tm) % n == 0
        up = (cCp_ref[...].astype(F32) * xCp_ref[...].astype(F32))[halo - 1:halo, :]
        un = (cCn_ref[...].astype(F32) * xCn_ref[...].astype(F32))[0:1, :]
        up = up * jnp.where(first, 0.0, 1.0)
        un = un * jnp.where(last, 0.0, 1.0)
        row = lax.broadcasted_iota(jnp.int32, u.shape, 0)
        prev = jnp.where(row == 0, up, pltpu.roll(u, 1, axis=0))
        nxt = jnp.where(row == tm - 1, un, pltpu.roll(u, tm - 1, axis=0))
        conv = prev * convw_ref[0:1, :] + u * convw_ref[1:2, :] + nxt * convw_ref[2:3, :] + convb_ref[...]
        ys_ref[2] = (bC_ref[...].astype(F32) * conv * gate(zC_ref)).astype(BF16)
        v = vD_ref[...].astype(F32)
        vc = v - jnp.mean(v, axis=-1, keepdims=True)
        vn = vc * lax.rsqrt(jnp.mean(vc * vc, axis=-1, keepdims=True) + EPS) * lng_ref[...] + lnb_ref[...]
        vn = vn.astype(BF16)
        ug = uD_ref[...].astype(F32) * gate(zD_ref)
        gw = vn.shape[1] // ws_ref.shape[0]
        for g in range(ws_ref.shape[0]):
            for c in range(tm // CHUNK):
                rs, cs = slice(c * CHUNK, (c + 1) * CHUNK), slice(g * gw, (g + 1) * gw)
                mixed = _dot(ws_ref[g], vn[rs, cs]) + bsb_ref[g]
                ys_ref[3, rs, cs] = (ug[rs, cs] * mixed).astype(BF16)
        acc_ref[...] = jnp.zeros_like(acc_ref)

    merged = None
    for g, g_ref in enumerate((g0_ref, g1_ref, g2_ref, g3_ref)):
        term = _sigmoid(g_ref[...].astype(F32)) * _dot(ys_ref[g], wb_ref[g])
        merged = term if merged is None else merged + term
    acc_ref[...] += _dot(merged.astype(BF16), wo_ref[...])

    @pl.when(j == pl.num_programs(1) - 1)
    def _():
        o_ref[...] = x_ref[...] + mod_ref[0, 2:3, :] * _rms(acc_ref[...], postg_ref[...])


def _mix(p, attn, v, x2, mod3, mod_row, consts, B, n):
    R, D = x2.shape
    bw = attn.shape[1]
    tm = min(512, n)
    tn = min(512, D)
    halo = 16
    nt = n // tm
    cl = tm // FFT_N1
    gate0 = N_BRANCH_COLS * bw // tn

    def seg(k):
        return pl.BlockSpec((tm, bw), lambda i, j: (i, k))

    def halo_prev(k):
        return pl.BlockSpec((halo, bw), lambda i, j: (jnp.maximum(i * (tm // halo) - 1, 0), k))

    def halo_next(k):
        return pl.BlockSpec((halo, bw), lambda i, j: (jnp.minimum((i + 1) * (tm // halo), R // halo - 1), k))

    def gates(g):
        return pl.BlockSpec((tm, tn), lambda i, j: (i, gate0 + g * (D // tn) + j))

    def const(arr):
        nd = arr.ndim
        return pl.BlockSpec(arr.shape, lambda i, j: (0,) * nd)

    vspec_r = pl.BlockSpec((1, 1, FFT_N1, cl, bw), lambda i, j: (i // nt, 0, 0, i % nt, 0))
    vspec_i = pl.BlockSpec((1, 1, FFT_N1, cl, bw), lambda i, j: (i // nt, 1, 0, i % nt, 0))
    in_specs = ([pl.BlockSpec((tm, bw), lambda i, j: (i, 0)), vspec_r, vspec_i]
                + [seg(k) for k in (0, 2, 3, 4, 5, 6, 7, 8, 9)]
                + [halo_prev(3), halo_prev(5), halo_next(3), halo_next(5)]
                + [gates(g) for g in range(N_BRANCH)]
                + [pl.BlockSpec((tm, D), lambda i, j: (i, 0)),
                   pl.BlockSpec((1, 3, D), lambda i, j: (mod_row(i * tm), 0, 0))]
                + [const(consts[k]) for k in ("ccb", "scb", "conv_w", "conv_b", "ln_g", "ln_b", "w_s", "b_s")]
                + [pl.BlockSpec((N_BRANCH, bw, tn), lambda i, j: (0, 0, j)),
                   pl.BlockSpec((tn, D), lambda i, j: (j, 0)),
                   const(consts["post_g"])])
    args = ([attn, v, v] + [p] * 9 + [p] * 4 + [p] * 4 + [x2, mod3]
            + [consts[k] for k in ("ccb", "scb", "conv_w", "conv_b", "ln_g", "ln_b", "w_s", "b_s")]
            + [consts["w_branch"], consts["w_out"], consts["post_g"]])
    blocks = (12 * _nbytes((tm, bw), BF16) + 4 * _nbytes((halo, bw), BF16) + N_BRANCH * _nbytes((tm, tn), BF16)
              + 2 * _nbytes((tm, D), F32) + 2 * _nbytes((bw, bw), BF16) + N_BRANCH * _nbytes((CHUNK, CHUNK), F32) * 2
              + _nbytes((N_BRANCH, bw, tn), BF16) + _nbytes((tn, D), BF16))
    scratch = _nbytes((N_BRANCH, tm, bw), BF16) + _nbytes((tm, D), F32) + 2 * _nbytes((tm, bw), F32)
    return pl.pallas_call(
        functools.partial(_mix_kernel, n=n, tm=tm, halo=halo,
                          fscale=float(n * (bw // FOURIER_GROUPS)) ** -0.5),
        out_shape=jax.ShapeDtypeStruct((R, D), F32),
        grid=(R // tm, D // tn),
        in_specs=in_specs,
        out_specs=pl.BlockSpec((tm, D), lambda i, j: (i, 0)),
        scratch_shapes=[pltpu.VMEM((N_BRANCH, tm, bw), BF16), pltpu.VMEM((tm, D), F32),
                        pltpu.VMEM((FOURIER_GROUPS, tm, bw // FOURIER_GROUPS), F32),
                        pltpu.VMEM((FOURIER_GROUPS, tm, bw // FOURIER_GROUPS), F32)],
        compiler_params=_compiler_params(("parallel", "arbitrary"), blocks, scratch,
                                         8 * _nbytes((tm, bw), F32) + 2 * _nbytes((tm, D), F32)),
        name="branch_mix",
    )(*args)


def _swap_halves(w):
    h = w.shape[-1] // 2
    return jnp.concatenate([w[..., h:], w[..., :h]], axis=-1)


def _relayout_w_in(w):
    rope_end = ROPE_OFF + QK_ROPE
    head_end = rope_end + HEAD_W - CQ_OFF
    rope = w[:, ROPE_OFF:rope_end]
    pad = jnp.zeros((w.shape[0], CQ_OFF - ROPE_OFF - 2 * QK_ROPE), w.dtype)
    head = jnp.concatenate([w[:, :ROPE_OFF], rope, _swap_halves(rope), pad, w[:, rope_end:head_end]], axis=1)
    return head.astype(BF16), w[:, head_end:].astype(BF16)


def _relayout_w_uq(w):
    w = w.reshape(w.shape[0], MLA_HEADS, QK_NOPE + QK_ROPE)
    rope = w[..., QK_NOPE:]
    return jnp.concatenate([w[..., :QK_NOPE], rope, _swap_halves(rope)], axis=-1).reshape(w.shape[0], -1).astype(BF16)


def _rope_table(n, rotate):
    half = QK_ROPE // 2
    if rotate:
        pos = np.arange(n)
        inv = ROPE_THETA ** (-np.arange(0, half, 2, dtype=np.float64) / half)
        ang = np.concatenate([(pos // GRID_W)[:, None] * inv, (pos % GRID_W)[:, None] * inv], axis=-1)
    else:
        ang = np.zeros((n, half))
    cos, sin = np.cos(ang), np.sin(ang)
    return jnp.asarray(np.concatenate([cos, cos, -sin, sin], axis=-1), F32)


def _channel_dft(bw):
    gw = bw // FOURIER_GROUPS
    a = np.arange(gw)
    ang = 2.0 * np.pi * ((a[:, None] * a[None, :]) % gw) / gw
    return jnp.asarray(np.cos(ang), F32).astype(BF16), jnp.asarray(np.sin(ang), F32).astype(BF16)


def kernel(x, c, ctx, c_ctx, w_mod, b_mod, pre_g, post_g, w_in, q_norm_g, kv_norm_g, w_uq, w_ukv,
           conv_w, conv_b, sgu_ln_g, sgu_ln_b, sgu_w, sgu_b, w_branch, w_out):
    B, n, D = x.shape
    nc = ctx.shape[1]
    depth = w_in.shape[0]
    bw = w_branch.shape[2]
    assert w_in.shape[2] == ROPE_OFF + QK_ROPE + (HEAD_W - CQ_OFF) + N_BRANCH_COLS * bw + N_BRANCH * D
    assert kv_norm_g.shape[1] == ROPE_OFF and q_norm_g.shape[1] == HEAD_W - CQ_OFF and B <= 4

    cc = jnp.zeros((8, D), F32).at[:B].set(c).at[B].set(c_ctx)
    mod = _modulation(cc, w_mod, b_mod).reshape(depth, 8, 3, D)

    cs_x, cs_c = _rope_table(n, True), _rope_table(nc, False)
    ccb, scb = _channel_dft(bw)
    x2 = x.reshape(B * n, D)
    c2 = ctx.reshape(B * nc, D)
    row_x = lambda r: r // n
    row_c = lambda r: B

    for l in range(depth):
        last = l == depth - 1
        w_head, w_rest = _relayout_w_in(w_in[l])
        w_q2 = _relayout_w_uq(w_uq[l])
        w_ukv_l = w_ukv[l].astype(BF16)
        consts = dict(
            ccb=ccb, scb=scb, conv_w=conv_w[l], conv_b=conv_b[l].reshape(1, bw),
            ln_g=sgu_ln_g[l].reshape(1, bw), ln_b=sgu_ln_b[l].reshape(1, bw),
            w_s=sgu_w[l].astype(BF16),
            b_s=jnp.broadcast_to(sgu_b[l][:, :, None], sgu_b.shape[1:] + (bw // sgu_w.shape[1],)),
            w_branch=w_branch[l].astype(BF16), w_out=w_out[l].astype(BF16), post_g=post_g[l].reshape(1, D))

        hx, px = _in_projection(x2, mod[l], row_x, pre_g[l], w_head, w_rest, min(1024, n))
        hc, *pc = _in_projection(c2, mod[l], row_c, pre_g[l], w_head, None if last else w_rest,
                                 min(1024, B * nc))
        q_c, k_c, v_c = _mla_prep(hc, cs_c, kv_norm_g[l], q_norm_g[l], w_ukv_l, w_q2, B, nc)
        q_x, k_x, v_x = _mla_prep(hx, cs_x, kv_norm_g[l], q_norm_g[l], w_ukv_l, w_q2, B, n)

        attn_x = _attention(q_x, k_c, v_c, k_x, v_x).reshape(B * n, -1)
        four_x = _position_dft(px, 1, bw, B, n)
        new_x = _mix(px, attn_x, four_x, x2, mod[l], row_x, consts, B, n)
        if not last:
            attn_c = _attention(q_c, k_c, v_c).reshape(B * nc, -1)
            four_c = _position_dft(pc[0], 1, bw, B, nc)
            c2 = _mix(pc[0], attn_c, four_c, c2, mod[l], row_c, consts, B, nc)
        x2 = new_x
    return x2.reshape(B, n, D)
```

```python
import functools

import jax
import jax.numpy as jnp
import numpy as np
from jax import lax
from jax.experimental import pallas as pl
from jax.experimental.pallas import tpu as pltpu

F32 = jnp.float32
BF16 = jnp.bfloat16

MLA_HEADS = 4
QK_NOPE = 128
QK_ROPE = 64
V_HEAD = 128
GRID_W = 64
ROPE_THETA = 10000.0
FOURIER_GROUPS = 4
CHUNK = 128
N_BRANCH = 4
N_BRANCH_COLS = 10
EPS = 1e-6

ROPE_OFF = 256
CQ_OFF = 512
HEAD_W = 1024
QK_W = QK_NOPE + 2 * QK_ROPE

FFT_N1 = 16

V7X_VMEM_BYTES = 64 * 2**20
VMEM_CAP_BYTES = V7X_VMEM_BYTES - 8 * 2**20


def _nbytes(shape, dtype):
    return int(np.prod(shape)) * jnp.dtype(dtype).itemsize


def _compiler_params(semantics, block_bytes, scratch_bytes=0, temp_bytes=0):
    need = 2 * block_bytes + scratch_bytes + temp_bytes + 2 * 2**20
    return pltpu.CompilerParams(dimension_semantics=semantics,
                                vmem_limit_bytes=int(min(max(need, 16 * 2**20), VMEM_CAP_BYTES)))


def _sigmoid(v):
    return 0.5 * jnp.tanh(0.5 * v) + 0.5


def _silu(v):
    return v * _sigmoid(v)


def _rms(v, g):
    return v * lax.rsqrt(jnp.mean(v * v, axis=-1, keepdims=True) + EPS) * g


def _dot(a, b):
    return jnp.dot(a, b, preferred_element_type=F32)


def _mod_kernel(c_ref, w_ref, b_ref, o_ref):
    s = _silu(c_ref[...]).astype(BF16)
    o_ref[0] = _dot(s, w_ref[0].astype(BF16)) + b_ref[0]


def _modulation(cc, w_mod, b_mod):
    L, D, W = w_mod.shape
    tn = 768 if W % 768 == 0 else W
    blocks = _nbytes((D, tn), F32) + _nbytes((8, D), F32) + 2 * _nbytes((8, tn), F32)
    return pl.pallas_call(
        _mod_kernel,
        out_shape=jax.ShapeDtypeStruct((L, 8, W), F32),
        grid=(L, W // tn),
        in_specs=[pl.BlockSpec((8, D), lambda l, j: (0, 0)),
                  pl.BlockSpec((1, D, tn), lambda l, j: (l, 0, j)),
                  pl.BlockSpec((1, 1, tn), lambda l, j: (l, 0, j))],
        out_specs=pl.BlockSpec((1, 8, tn), lambda l, j: (l, 0, j)),
        compiler_params=_compiler_params(("parallel", "parallel"), blocks, 0, _nbytes((D, tn), BF16)),
        name="modulation",
    )(cc, w_mod, b_mod.reshape(L, 1, W))


def _inproj_kernel(*refs, with_rest):
    if with_rest:
        x_ref, mod_ref, g_ref, wh_ref, wr_ref, oh_ref, or_ref, h_ref = refs
    else:
        x_ref, mod_ref, g_ref, wh_ref, oh_ref, h_ref = refs
    j = pl.program_id(1)

    @pl.when(j == 0)
    def _():
        y = _rms(x_ref[...], g_ref[...])
        h_ref[...] = (y * (1.0 + mod_ref[0, 1:2, :]) + mod_ref[0, 0:1, :]).astype(BF16)
        oh_ref[...] = _dot(h_ref[...], wh_ref[...]).astype(oh_ref.dtype)

    if with_rest:
        @pl.when(j > 0)
        def _():
            or_ref[...] = _dot(h_ref[...], wr_ref[...]).astype(or_ref.dtype)


def _in_projection(x2, mod3, mod_row, pre_g, w_head, w_rest, tm):
    R, D = x2.shape
    tn = w_head.shape[1]
    with_rest = w_rest is not None
    nj = 1 + (w_rest.shape[1] // tn if with_rest else 0)
    rest_col = lambda j: jnp.maximum(j - 1, 0)
    in_specs = [pl.BlockSpec((tm, D), lambda i, j: (i, 0)),
                pl.BlockSpec((1, 3, D), lambda i, j: (mod_row(i * tm), 0, 0)),
                pl.BlockSpec((1, D), lambda i, j: (0, 0)),
                pl.BlockSpec((D, tn), lambda i, j: (0, 0))]
    out_specs = [pl.BlockSpec((tm, tn), lambda i, j: (i, 0))]
    out_shape = [jax.ShapeDtypeStruct((R, tn), BF16)]
    args = [x2, mod3, pre_g.reshape(1, D), w_head]
    if with_rest:
        in_specs.append(pl.BlockSpec((D, tn), lambda i, j: (0, rest_col(j))))
        out_specs.append(pl.BlockSpec((tm, tn), lambda i, j: (i, rest_col(j))))
        out_shape.append(jax.ShapeDtypeStruct((R, w_rest.shape[1]), BF16))
        args.append(w_rest)
    blocks = (_nbytes((tm, D), F32) + _nbytes((3, D), F32) + 2 * _nbytes((D, tn), BF16)
              + 2 * _nbytes((tm, tn), BF16))
    return pl.pallas_call(
        functools.partial(_inproj_kernel, with_rest=with_rest),
        out_shape=out_shape,
        grid=(R // tm, nj),
        in_specs=in_specs,
        out_specs=out_specs,
        scratch_shapes=[pltpu.VMEM((tm, D), BF16)],
        compiler_params=_compiler_params(("parallel", "arbitrary"), blocks, _nbytes((tm, D), BF16),
                                         2 * _nbytes((tm, D), F32) + _nbytes((tm, tn), F32)),
        name="in_projection",
    )(*args)


def _rope_pair(group, cs):
    r = group * cs
    return r + pltpu.roll(r, QK_ROPE, axis=1)


def _prep_kernel(p_ref, cs_ref, kvg_ref, qg_ref, wkv_ref, wq_ref, q_ref, k_ref, v_ref, *, scale):
    cs = cs_ref[...]
    kvn = _rms(p_ref[:, 0:ROPE_OFF].astype(F32), kvg_ref[...]).astype(BF16)
    kv = _dot(kvn, wkv_ref[...])
    kr = _rope_pair(p_ref[:, ROPE_OFF:ROPE_OFF + 2 * QK_ROPE].astype(F32), cs)
    lane = lax.broadcasted_iota(jnp.int32, kr.shape, 1)
    kr = jnp.where(lane < QK_ROPE, kr, 0.0).astype(BF16)
    cqn = _rms(p_ref[:, CQ_OFF:HEAD_W].astype(F32), qg_ref[...]).astype(BF16)
    q = _dot(cqn, wq_ref[...]) * scale
    kvw = QK_NOPE + V_HEAD
    for h in range(MLA_HEADS):
        k_ref[0, h, :, 0:QK_NOPE] = kv[:, h * kvw:h * kvw + QK_NOPE].astype(BF16)
        k_ref[0, h, :, QK_NOPE:QK_W] = kr
        v_ref[0, h, :, 0:V_HEAD] = kv[:, h * kvw + QK_NOPE:(h + 1) * kvw].astype(BF16)
        v_ref[0, h, :, V_HEAD:2 * V_HEAD] = jnp.ones((kv.shape[0], V_HEAD), BF16)
        q_ref[0, h, :, 0:QK_NOPE] = q[:, h * QK_W:h * QK_W + QK_NOPE].astype(BF16)
        q_ref[0, h, :, QK_NOPE:QK_W] = _rope_pair(q[:, h * QK_W + QK_NOPE:(h + 1) * QK_W], cs).astype(BF16)


def _mla_prep(p, cs, kv_norm_g, q_norm_g, w_ukv, w_q2, B, n):
    tm = min(512, n)
    nt = n // tm
    H = MLA_HEADS
    kv_lora, q_lora = w_ukv.shape[0], w_q2.shape[0]
    blocks = (_nbytes((tm, HEAD_W), BF16) + _nbytes((tm, 128), F32) + _nbytes(w_ukv.shape, BF16)
              + _nbytes(w_q2.shape, BF16) + 2 * _nbytes((H, tm, QK_W), BF16) + _nbytes((H, tm, 2 * V_HEAD), BF16))
    qk_spec = pl.BlockSpec((1, H, tm, QK_W), lambda b, i: (b, 0, i, 0))
    return pl.pallas_call(
        functools.partial(_prep_kernel, scale=float(QK_NOPE + QK_ROPE) ** -0.5),
        out_shape=(jax.ShapeDtypeStruct((B, H, n, QK_W), BF16),
                   jax.ShapeDtypeStruct((B, H, n, QK_W), BF16),
                   jax.ShapeDtypeStruct((B, H, n, 2 * V_HEAD), BF16)),
        grid=(B, nt),
        in_specs=[pl.BlockSpec((tm, HEAD_W), lambda b, i: (b * nt + i, 0)),
                  pl.BlockSpec((tm, 128), lambda b, i: (i, 0)),
                  pl.BlockSpec((1, kv_lora), lambda b, i: (0, 0)),
                  pl.BlockSpec((1, q_lora), lambda b, i: (0, 0)),
                  pl.BlockSpec(w_ukv.shape, lambda b, i: (0, 0)),
                  pl.BlockSpec(w_q2.shape, lambda b, i: (0, 0))],
        out_specs=(qk_spec, qk_spec, pl.BlockSpec((1, H, tm, 2 * V_HEAD), lambda b, i: (b, 0, i, 0))),
        compiler_params=_compiler_params(("parallel", "parallel"), blocks, 0, 6 * _nbytes((tm, HEAD_W), F32)),
        name="mla_prep",
    )(p, cs, kv_norm_g.reshape(1, -1), q_norm_g.reshape(1, -1), w_ukv, w_q2)


_NT = (((1,), (1,)), ((), ()))


def _attn_kernel(*refs, with_x, tk):
    if with_x:
        q_ref, kc_ref, vc_ref, kx_ref, vx_ref, o_ref = refs
    else:
        q_ref, kc_ref, vc_ref, o_ref = refs
    q = q_ref[0, 0]

    s = lax.dot_general(q, kc_ref[0, 0], _NT, preferred_element_type=F32)
    m = jnp.max(s, axis=-1, keepdims=True)
    acc = _dot(jnp.exp(s - m).astype(BF16), vc_ref[0, 0])
    if with_x:
        for c in range(kx_ref.shape[2] // tk):
            s = lax.dot_general(q, kx_ref[0, 0, c * tk:(c + 1) * tk, :], _NT, preferred_element_type=F32)
            m_new = jnp.maximum(m, jnp.max(s, axis=-1, keepdims=True))
            p = jnp.exp(s - m_new).astype(BF16)
            acc = acc * jnp.exp(m - m_new) + _dot(p, vx_ref[0, 0, c * tk:(c + 1) * tk, :])
            m = m_new
    o_ref[0] = (acc[:, :V_HEAD] / acc[:, V_HEAD:]).astype(o_ref.dtype)


def _attention(q, kc, vc, kx=None, vx=None):
    B, H, nq, _ = q.shape
    nc = kc.shape[2]
    with_x = kx is not None
    nk = nc + (kx.shape[2] if with_x else 0)
    tq = min(512, nq)
    tk = min(512, nq)
    q_spec = pl.BlockSpec((1, 1, tq, QK_W), lambda b, h, i: (b, h, i, 0))

    def full(arr):
        return pl.BlockSpec((1, 1) + arr.shape[2:], lambda b, h, i: (b, h, 0, 0))

    args = [q, kc, vc] + ([kx, vx] if with_x else [])
    blocks = _nbytes((tq, QK_W), BF16) + _nbytes((nk, QK_W + 2 * V_HEAD), BF16) + _nbytes((tq, V_HEAD), BF16)
    return pl.pallas_call(
        functools.partial(_attn_kernel, with_x=with_x, tk=tk),
        out_shape=jax.ShapeDtypeStruct((B, nq, H * V_HEAD), BF16),
        grid=(B, H, nq // tq),
        in_specs=[q_spec] + [full(a) for a in args[1:]],
        out_specs=pl.BlockSpec((1, tq, V_HEAD), lambda b, h, i: (b, i, h)),
        compiler_params=_compiler_params(("parallel", "parallel", "parallel"), blocks, 0,
                                         6 * _nbytes((tq, tk), F32)),
        name="attention",
    )(*args)


def _fft1_kernel(u_ref, m_ref, tc_ref, ts_ref, o_ref, *, reps):
    rows = FFT_N1 * FFT_N1
    u = u_ref[0, :, 0].reshape(rows, u_ref.shape[-1])
    g = _dot(m_ref[...], u)
    gr, gi = g[:rows], g[rows:]
    tc = jnp.concatenate([tc_ref[:, 0].reshape(rows, 128)] * reps, axis=1)
    ts = jnp.concatenate([ts_ref[:, 0].reshape(rows, 128)] * reps, axis=1)
    shape3 = (FFT_N1, FFT_N1, u_ref.shape[-1])
    o_ref[0, 0, :, 0] = (gr * tc + gi * ts).reshape(shape3).astype(o_ref.dtype)
    o_ref[0, 1, :, 0] = (gi * tc - gr * ts).reshape(shape3).astype(o_ref.dtype)


def _fft2_kernel(g_ref, fa_ref, fb_ref, o_ref, *, n2):
    y = _dot(fa_ref[...], g_ref[0, 0, 0]) + _dot(fb_ref[...], g_ref[0, 1, 0])
    o_ref[0, 0, 0] = y[:n2].astype(o_ref.dtype)
    o_ref[0, 1, 0] = y[n2:].astype(o_ref.dtype)


def _dft_tables(n):
    n1 = FFT_N1
    n2 = n // n1
    a = np.arange(n1)
    ang1 = 2.0 * np.pi * ((a[:, None] * a[None, :]) % n1) / n1
    eye = np.eye(n1)
    m1 = np.concatenate([np.kron(np.cos(ang1), eye), -np.kron(np.sin(ang1), eye)], axis=0)
    b = np.arange(n2)
    angt = 2.0 * np.pi * (a[:, None] * b[None, :]) / n
    shape4 = (n1, n2 // n1, n1, 128)
    tc = np.broadcast_to(np.cos(angt)[:, :, None], (n1, n2, 128)).reshape(shape4)
    ts = np.broadcast_to(np.sin(angt)[:, :, None], (n1, n2, 128)).reshape(shape4)
    ang2 = 2.0 * np.pi * ((b[:, None] * b[None, :]) % n2) / n2
    c2, s2 = np.cos(ang2), np.sin(ang2)
    fa = np.concatenate([c2, -s2], axis=0)
    fb = np.concatenate([s2, c2], axis=0)
    to_bf16 = lambda t: jnp.asarray(t, F32).astype(BF16)
    return to_bf16(m1), jnp.asarray(tc, F32), jnp.asarray(ts, F32), to_bf16(fa), to_bf16(fb)


def _position_dft(p, col_block, C, B, n):
    n1 = FFT_N1
    n2 = n // n1
    nb = n2 // n1
    m1, tc, ts, fa, fb = _dft_tables(n)
    p5 = p.reshape(B, n1, nb, n1, p.shape[1])
    blocks = (3 * _nbytes((n1 * n1, C), BF16) + _nbytes(m1.shape, BF16) + 2 * _nbytes((n1 * n1, 128), F32))
    g = pl.pallas_call(
        functools.partial(_fft1_kernel, reps=C // 128),
        out_shape=jax.ShapeDtypeStruct((B, 2, n1, nb, n1, C), BF16),
        grid=(B, nb),
        in_specs=[pl.BlockSpec((1, n1, 1, n1, C), lambda b, r: (b, 0, r, 0, col_block)),
                  pl.BlockSpec(m1.shape, lambda b, r: (0, 0)),
                  pl.BlockSpec((n1, 1, n1, 128), lambda b, r: (0, r, 0, 0)),
                  pl.BlockSpec((n1, 1, n1, 128), lambda b, r: (0, r, 0, 0))],
        out_specs=pl.BlockSpec((1, 2, n1, 1, n1, C), lambda b, r: (b, 0, 0, r, 0, 0)),
        compiler_params=_compiler_params(("parallel", "parallel"), blocks, 0, 8 * _nbytes((n1 * n1, C), F32)),
        name="position_dft_stage1",
    )(p5, m1, tc, ts)
    g = g.reshape(B, 2, n1, n2, C)
    blocks = 4 * _nbytes((n2, C), BF16) + 2 * _nbytes(fa.shape, BF16)
    return pl.pallas_call(
        functools.partial(_fft2_kernel, n2=n2),
        out_shape=jax.ShapeDtypeStruct((B, 2, n1, n2, C), BF16),
        grid=(B, n1),
        in_specs=[pl.BlockSpec((1, 2, 1, n2, C), lambda b, d: (b, 0, d, 0, 0)),
                  pl.BlockSpec(fa.shape, lambda b, d: (0, 0)),
                  pl.BlockSpec(fb.shape, lambda b, d: (0, 0))],
        out_specs=pl.BlockSpec((1, 2, 1, n2, C), lambda b, d: (b, 0, d, 0, 0)),
        compiler_params=_compiler_params(("parallel", "parallel"), blocks, 0, 4 * _nbytes((n2, C), F32)),
        name="position_dft_stage2",
    )(g, fa, fb)


def _branch_kernel(attn_ref, vr_ref, vi_ref, zA_ref, zB_ref, xC_ref, bC_ref, cC_ref, zC_ref, uD_ref, vD_ref,
                   zD_ref, xCp_ref, cCp_ref, xCn_ref, cCn_ref, ccb_ref, scb_ref, convw_ref, convb_ref, lng_ref,
                   lnb_ref, ws_ref, bsb_ref, ys_ref, fr_ref, fi_ref, *, n, tm, halo, fscale):
    i = pl.program_id(0)

    def gate(ref):
        return _silu(ref[...].astype(F32))

    ys_ref[0] = (attn_ref[...].astype(F32) * gate(zA_ref)).astype(BF16)
    gwf = ccb_ref.shape[0]
    for d in range(FFT_N1):
        rows = pl.ds(d, tm // FFT_N1, stride=FFT_N1)
        vr, vi = vr_ref[0, 0, d].astype(F32), vi_ref[0, 0, d].astype(F32)
        for g in range(fr_ref.shape[0]):
            fr_ref[g, rows, :] = vr[:, g * gwf:(g + 1) * gwf]
            fi_ref[g, rows, :] = vi[:, g * gwf:(g + 1) * gwf]
    four = jnp.concatenate(
        [_dot(fr_ref[g].astype(BF16), ccb_ref[...]) + _dot(fi_ref[g].astype(BF16), scb_ref[...])
         for g in range(fr_ref.shape[0])], axis=1)
    ys_ref[1] = (four * fscale * gate(zB_ref)).astype(BF16)
    u = cC_ref[...].astype(F32) * xC_ref[...].astype(F32)
    first = (i * tm) % n == 0
    last = ((i + 1) * tm) % n == 0
    up = (cCp_ref[...].astype(F32) * xCp_ref[...].astype(F32))[halo - 1:halo, :]
    un = (cCn_ref[...].astype(F32) * xCn_ref[...].astype(F32))[0:1, :]
    up = up * jnp.where(first, 0.0, 1.0)
    un = un * jnp.where(last, 0.0, 1.0)
    row = lax.broadcasted_iota(jnp.int32, u.shape, 0)
    prev = jnp.where(row == 0, up, pltpu.roll(u, 1, axis=0))
    nxt = jnp.where(row == tm - 1, un, pltpu.roll(u, tm - 1, axis=0))
    conv = prev * convw_ref[0:1, :] + u * convw_ref[1:2, :] + nxt * convw_ref[2:3, :] + convb_ref[...]
    ys_ref[2] = (bC_ref[...].astype(F32) * conv * gate(zC_ref)).astype(BF16)
    v = vD_ref[...].astype(F32)
    vc = v - jnp.mean(v, axis=-1, keepdims=True)
    vn = vc * lax.rsqrt(jnp.mean(vc * vc, axis=-1, keepdims=True) + EPS) * lng_ref[...] + lnb_ref[...]
    vn = vn.astype(BF16)
    ug = uD_ref[...].astype(F32) * gate(zD_ref)
    gw = vn.shape[1] // ws_ref.shape[0]
    for g in range(ws_ref.shape[0]):
        for c in range(tm // CHUNK):
            rs, cs = slice(c * CHUNK, (c + 1) * CHUNK), slice(g * gw, (g + 1) * gw)
            mixed = _dot(ws_ref[g], vn[rs, cs]) + bsb_ref[g]
            ys_ref[3, rs, cs] = (ug[rs, cs] * mixed).astype(BF16)


def _mix_kernel(ys_ref, g0_ref, g1_ref, g2_ref, g3_ref, wb_ref, wo_ref, x_ref, mod_ref, postg_ref, o_ref,
                acc_ref, rs_ref, *, nj, te):
    j = pl.program_id(1)
    nc = acc_ref.shape[0]

    @pl.when(j < nj)
    def _():
        merged = None
        for g, g_ref in enumerate((g0_ref, g1_ref, g2_ref, g3_ref)):
            term = _sigmoid(g_ref[...].astype(F32)) * _dot(ys_ref[g], wb_ref[g])
            merged = term if merged is None else merged + term
        part = _dot(merged.astype(BF16), wo_ref[...])

        @pl.when(j == 0)
        def _():
            for c in range(nc):
                acc_ref[c] = part[:, c * te:(c + 1) * te]

        @pl.when(j > 0)
        def _():
            for c in range(nc):
                acc_ref[c] += part[:, c * te:(c + 1) * te]

    @pl.when(j == nj - 1)
    def _():
        ss = None
        for c in range(nc):
            a = acc_ref[c]
            s = jnp.sum(a * a, axis=-1, keepdims=True)
            ss = s if ss is None else ss + s
        rs_ref[...] = jnp.broadcast_to(lax.rsqrt(ss / (nc * te) + EPS), rs_ref.shape)

    @pl.when(j >= nj)
    def _():
        y = acc_ref[j - nj] * rs_ref[:, 0:1] * postg_ref[...]
        o_ref[...] = x_ref[...] + mod_ref[0, 2:3, :] * y


def _mix(p, attn, v, x2, mod3, mod_row, consts, n, tile_rows):
    R, D = x2.shape
    bw = attn.shape[1]

    tb = min(512, n)
    halo = 16
    nt = n // tb
    cl = tb // FFT_N1

    def seg(k):
        return pl.BlockSpec((tb, bw), lambda i: (i, k))

    def halo_prev(k):
        return pl.BlockSpec((halo, bw), lambda i: (jnp.maximum(i * (tb // halo) - 1, 0), k))

    def halo_next(k):
        return pl.BlockSpec((halo, bw), lambda i: (jnp.minimum((i + 1) * (tb // halo), R // halo - 1), k))

    def const(arr):
        nd = arr.ndim
        return pl.BlockSpec(arr.shape, lambda i: (0,) * nd)

    branch_consts = [consts[k] for k in ("ccb", "scb", "conv_w", "conv_b", "ln_g", "ln_b", "w_s", "b_s")]
    gwf = bw // FOURIER_GROUPS
    blocks = (12 * _nbytes((tb, bw), BF16) + 4 * _nbytes((halo, bw), BF16) + N_BRANCH * _nbytes((tb, bw), BF16)
              + sum(_nbytes(a.shape, a.dtype) for a in branch_consts))
    ys = pl.pallas_call(
        functools.partial(_branch_kernel, n=n, tm=tb, halo=halo, fscale=float(n * gwf) ** -0.5),
        out_shape=jax.ShapeDtypeStruct((N_BRANCH, R, bw), BF16),
        grid=(R // tb,),
        in_specs=([pl.BlockSpec((tb, bw), lambda i: (i, 0)),
                   pl.BlockSpec((1, 1, FFT_N1, cl, bw), lambda i: (i // nt, 0, 0, i % nt, 0)),
                   pl.BlockSpec((1, 1, FFT_N1, cl, bw), lambda i: (i // nt, 1, 0, i % nt, 0))]
                  + [seg(k) for k in (0, 2, 3, 4, 5, 6, 7, 8, 9)]
                  + [halo_prev(3), halo_prev(5), halo_next(3), halo_next(5)]
                  + [const(a) for a in branch_consts]),
        out_specs=pl.BlockSpec((N_BRANCH, tb, bw), lambda i: (0, i, 0)),
        scratch_shapes=[pltpu.VMEM((FOURIER_GROUPS, tb, gwf), F32), pltpu.VMEM((FOURIER_GROUPS, tb, gwf), F32)],
        compiler_params=_compiler_params(("parallel",), blocks, 2 * _nbytes((tb, bw), F32),
                                         12 * _nbytes((tb, bw), F32)),
        name="branch_outputs",
    )(attn, v, v, *([p] * 9), *([p] * 4), *branch_consts)

    tm = min(1024, tile_rows)
    tn = min(512, D)
    te = min(512, D)
    nj, ne = D // tn, D // te
    gate0 = N_BRANCH_COLS * bw // tn
    jm = lambda j: jnp.minimum(j, nj - 1)
    je = lambda j: jnp.maximum(j - nj, 0)

    def gates(g):
        return pl.BlockSpec((tm, tn), lambda i, j: (i, gate0 + g * nj + jm(j)))

    blocks = (_nbytes((N_BRANCH, tm, bw), BF16) + N_BRANCH * _nbytes((tm, tn), BF16)
              + _nbytes((N_BRANCH, bw, tn), BF16) + _nbytes((tn, D), BF16) + 2 * _nbytes((tm, te), F32))
    scratch = _nbytes((tm, D), F32) + _nbytes((tm, 128), F32)
    return pl.pallas_call(
        functools.partial(_mix_kernel, nj=nj, te=te),
        out_shape=jax.ShapeDtypeStruct((R, D), F32),
        grid=(R // tm, nj + ne),
        in_specs=([pl.BlockSpec((N_BRANCH, tm, bw), lambda i, j: (0, i, 0))]
                  + [gates(g) for g in range(N_BRANCH)]
                  + [pl.BlockSpec((N_BRANCH, bw, tn), lambda i, j: (0, 0, jm(j))),
                     pl.BlockSpec((tn, D), lambda i, j: (jm(j), 0)),
                     pl.BlockSpec((tm, te), lambda i, j: (i, je(j))),
                     pl.BlockSpec((1, 3, te), lambda i, j: (mod_row(i * tm), 0, je(j))),
                     pl.BlockSpec((1, te), lambda i, j: (0, je(j)))]),
        out_specs=pl.BlockSpec((tm, te), lambda i, j: (i, je(j))),
        scratch_shapes=[pltpu.VMEM((ne, tm, te), F32), pltpu.VMEM((tm, 128), F32)],
        compiler_params=_compiler_params(("parallel", "arbitrary"), blocks, scratch,
                                         _nbytes((tm, D), F32) + 4 * _nbytes((tm, tn), F32)),
        name="branch_mix",
    )(ys, p, p, p, p, consts["w_branch"], consts["w_out"], x2, mod3, consts["post_g"])


def _swap_halves(w):
    h = w.shape[-1] // 2
    return jnp.concatenate([w[..., h:], w[..., :h]], axis=-1)


def _relayout_w_in(w):
    rope_end = ROPE_OFF + QK_ROPE
    head_end = rope_end + HEAD_W - CQ_OFF
    rope = w[:, ROPE_OFF:rope_end]
    pad = jnp.zeros((w.shape[0], CQ_OFF - ROPE_OFF - 2 * QK_ROPE), w.dtype)
    head = jnp.concatenate([w[:, :ROPE_OFF], rope, _swap_halves(rope), pad, w[:, rope_end:head_end]], axis=1)
    return head.astype(BF16), w[:, head_end:].astype(BF16)


def _relayout_w_uq(w):
    w = w.reshape(w.shape[0], MLA_HEADS, QK_NOPE + QK_ROPE)
    rope = w[..., QK_NOPE:]
    return jnp.concatenate([w[..., :QK_NOPE], rope, _swap_halves(rope)], axis=-1).reshape(w.shape[0], -1).astype(BF16)


def _rope_table(n, rotate):
    half = QK_ROPE // 2
    if rotate:
        pos = np.arange(n)
        inv = ROPE_THETA ** (-np.arange(0, half, 2, dtype=np.float64) / half)
        ang = np.concatenate([(pos // GRID_W)[:, None] * inv, (pos % GRID_W)[:, None] * inv], axis=-1)
    else:
        ang = np.zeros((n, half))
    cos, sin = np.cos(ang), np.sin(ang)
    return jnp.asarray(np.concatenate([cos, cos, -sin, sin], axis=-1), F32)


def _channel_dft(bw):
    gw = bw // FOURIER_GROUPS
    a = np.arange(gw)
    ang = 2.0 * np.pi * ((a[:, None] * a[None, :]) % gw) / gw
    return jnp.asarray(np.cos(ang), F32).astype(BF16), jnp.asarray(np.sin(ang), F32).astype(BF16)


def kernel(x, c, ctx, c_ctx, w_mod, b_mod, pre_g, post_g, w_in, q_norm_g, kv_norm_g, w_uq, w_ukv,
           conv_w, conv_b, sgu_ln_g, sgu_ln_b, sgu_w, sgu_b, w_branch, w_out):
    B, n, D = x.shape
    nc = ctx.shape[1]
    depth = w_in.shape[0]
    bw = w_branch.shape[2]
    assert w_in.shape[2] == ROPE_OFF + QK_ROPE + (HEAD_W - CQ_OFF) + N_BRANCH_COLS * bw + N_BRANCH * D
    assert kv_norm_g.shape[1] == ROPE_OFF and q_norm_g.shape[1] == HEAD_W - CQ_OFF and B <= 4

    cc = jnp.zeros((8, D), F32).at[:B].set(c).at[B].set(c_ctx)
    mod = _modulation(cc, w_mod, b_mod).reshape(depth, 8, 3, D)

    cs_x, cs_c = _rope_table(n, True), _rope_table(nc, False)
    ccb, scb = _channel_dft(bw)
    x2 = x.reshape(B * n, D)
    c2 = ctx.reshape(B * nc, D)
    row_x = lambda r: r // n
    row_c = lambda r: B

    for l in range(depth):
        last = l == depth - 1
        w_head, w_rest = _relayout_w_in(w_in[l])
        w_q2 = _relayout_w_uq(w_uq[l])
        w_ukv_l = w_ukv[l].astype(BF16)
        consts = dict(
            ccb=ccb, scb=scb, conv_w=conv_w[l], conv_b=conv_b[l].reshape(1, bw),
            ln_g=sgu_ln_g[l].reshape(1, bw), ln_b=sgu_ln_b[l].reshape(1, bw),
            w_s=sgu_w[l].astype(BF16),
            b_s=jnp.broadcast_to(sgu_b[l][:, :, None], sgu_b.shape[1:] + (bw // sgu_w.shape[1],)),
            w_branch=w_branch[l].astype(BF16), w_out=w_out[l].astype(BF16), post_g=post_g[l].reshape(1, D))

        hx, px = _in_projection(x2, mod[l], row_x, pre_g[l], w_head, w_rest, min(1024, n))
        hc, *pc = _in_projection(c2, mod[l], row_c, pre_g[l], w_head, None if last else w_rest,
                                 min(1024, B * nc))
        q_c, k_c, v_c = _mla_prep(hc, cs_c, kv_norm_g[l], q_norm_g[l], w_ukv_l, w_q2, B, nc)
        q_x, k_x, v_x = _mla_prep(hx, cs_x, kv_norm_g[l], q_norm_g[l], w_ukv_l, w_q2, B, n)

        attn_x = _attention(q_x, k_c, v_c, k_x, v_x).reshape(B * n, -1)
        four_x = _position_dft(px, 1, bw, B, n)
        new_x = _mix(px, attn_x, four_x, x2, mod[l], row_x, consts, n, n)
        if not last:
            attn_c = _attention(q_c, k_c, v_c).reshape(B * nc, -1)
            four_c = _position_dft(pc[0], 1, bw, B, nc)
            c2 = _mix(pc[0], attn_c, four_c, c2, mod[l], row_c, consts, nc, B * nc)
        x2 = new_x
    return x2.reshape(B, n, D)
```

```python
import functools

import jax
import jax.numpy as jnp
import numpy as np
from jax import lax
from jax.experimental import pallas as pl
from jax.experimental.pallas import tpu as pltpu

F32 = jnp.float32
BF16 = jnp.bfloat16

MLA_HEADS = 4
QK_NOPE = 128
QK_ROPE = 64
V_HEAD = 128
GRID_W = 64
ROPE_THETA = 10000.0
FOURIER_GROUPS = 4
CHUNK = 128
N_BRANCH = 4
N_BRANCH_COLS = 10
EPS = 1e-6

ROPE_OFF = 256
CQ_OFF = 512
HEAD_W = 1024
QK_W = QK_NOPE + 2 * QK_ROPE

FFT_N1 = 16

V7X_VMEM_BYTES = 64 * 2**20
VMEM_CAP_BYTES = V7X_VMEM_BYTES - 8 * 2**20


def _nbytes(shape, dtype):
    return int(np.prod(shape)) * jnp.dtype(dtype).itemsize


def _compiler_params(semantics, block_bytes, scratch_bytes=0, temp_bytes=0):
    need = 2 * block_bytes + scratch_bytes + temp_bytes + 2 * 2**20
    return pltpu.CompilerParams(dimension_semantics=semantics,
                                vmem_limit_bytes=int(min(max(need, 16 * 2**20), VMEM_CAP_BYTES)))


def _sigmoid(v):
    return 0.5 * jnp.tanh(0.5 * v) + 0.5


def _silu(v):
    return v * _sigmoid(v)


def _rms(v, g):
    return v * lax.rsqrt(jnp.mean(v * v, axis=-1, keepdims=True) + EPS) * g


def _dot(a, b):
    return jnp.dot(a, b, preferred_element_type=F32)


def _mod_kernel(c_ref, w_ref, b_ref, o_ref):
    s = _silu(c_ref[...]).astype(BF16)
    o_ref[0] = _dot(s, w_ref[0].astype(BF16)) + b_ref[0]


def _modulation(cc, w_mod, b_mod):
    L, D, W = w_mod.shape
    tn = 768 if W % 768 == 0 else W
    blocks = _nbytes((D, tn), F32) + _nbytes((8, D), F32) + 2 * _nbytes((8, tn), F32)
    return pl.pallas_call(
        _mod_kernel,
        out_shape=jax.ShapeDtypeStruct((L, 8, W), F32),
        grid=(L, W // tn),
        in_specs=[pl.BlockSpec((8, D), lambda l, j: (0, 0)),
                  pl.BlockSpec((1, D, tn), lambda l, j: (l, 0, j)),
                  pl.BlockSpec((1, 1, tn), lambda l, j: (l, 0, j))],
        out_specs=pl.BlockSpec((1, 8, tn), lambda l, j: (l, 0, j)),
        compiler_params=_compiler_params(("parallel", "parallel"), blocks, 0, _nbytes((D, tn), BF16)),
        name="modulation",
    )(cc, w_mod, b_mod.reshape(L, 1, W))


def _inproj_kernel(*refs, with_rest):
    if with_rest:
        x_ref, mod_ref, g_ref, wh_ref, wr_ref, oh_ref, or_ref, h_ref = refs
    else:
        x_ref, mod_ref, g_ref, wh_ref, oh_ref, h_ref = refs
    j = pl.program_id(1)

    @pl.when(j == 0)
    def _():
        y = _rms(x_ref[...], g_ref[...])
        h_ref[...] = (y * (1.0 + mod_ref[0, 1:2, :]) + mod_ref[0, 0:1, :]).astype(BF16)
        oh_ref[...] = _dot(h_ref[...], wh_ref[...]).astype(oh_ref.dtype)

    if with_rest:
        @pl.when(j > 0)
        def _():
            or_ref[...] = _dot(h_ref[...], wr_ref[...]).astype(or_ref.dtype)


def _in_projection(x2, mod3, mod_row, pre_g, w_head, w_rest, tm):
    R, D = x2.shape
    tn = w_head.shape[1]
    with_rest = w_rest is not None
    nj = 1 + (w_rest.shape[1] // tn if with_rest else 0)
    rest_col = lambda j: jnp.maximum(j - 1, 0)
    in_specs = [pl.BlockSpec((tm, D), lambda i, j: (i, 0)),
                pl.BlockSpec((1, 3, D), lambda i, j: (mod_row(i * tm), 0, 0)),
                pl.BlockSpec((1, D), lambda i, j: (0, 0)),
                pl.BlockSpec((D, tn), lambda i, j: (0, 0))]
    out_specs = [pl.BlockSpec((tm, tn), lambda i, j: (i, 0))]
    out_shape = [jax.ShapeDtypeStruct((R, tn), BF16)]
    args = [x2, mod3, pre_g.reshape(1, D), w_head]
    if with_rest:
        in_specs.append(pl.BlockSpec((D, tn), lambda i, j: (0, rest_col(j))))
        out_specs.append(pl.BlockSpec((tm, tn), lambda i, j: (i, rest_col(j))))
        out_shape.append(jax.ShapeDtypeStruct((R, w_rest.shape[1]), BF16))
        args.append(w_rest)
    blocks = (_nbytes((tm, D), F32) + _nbytes((3, D), F32) + 2 * _nbytes((D, tn), BF16)
              + 2 * _nbytes((tm, tn), BF16))
    return pl.pallas_call(
        functools.partial(_inproj_kernel, with_rest=with_rest),
        out_shape=out_shape,
        grid=(R // tm, nj),
        in_specs=in_specs,
        out_specs=out_specs,
        scratch_shapes=[pltpu.VMEM((tm, D), BF16)],
        compiler_params=_compiler_params(("parallel", "arbitrary"), blocks, _nbytes((tm, D), BF16),
                                         2 * _nbytes((tm, D), F32) + _nbytes((tm, tn), F32)),
        name="in_projection",
    )(*args)


def _rope_pair(group, cs):
    r = group * cs
    return r + pltpu.roll(r, QK_ROPE, axis=1)


def _prep_kernel(p_ref, cs_ref, kvg_ref, qg_ref, wkv_ref, wq_ref, q_ref, k_ref, v_ref, *, scale):
    cs = cs_ref[...]
    kvn = _rms(p_ref[:, 0:ROPE_OFF].astype(F32), kvg_ref[...]).astype(BF16)
    kv = _dot(kvn, wkv_ref[...])
    kr = _rope_pair(p_ref[:, ROPE_OFF:ROPE_OFF + 2 * QK_ROPE].astype(F32), cs)
    lane = lax.broadcasted_iota(jnp.int32, kr.shape, 1)
    kr = jnp.where(lane < QK_ROPE, kr, 0.0).astype(BF16)
    cqn = _rms(p_ref[:, CQ_OFF:HEAD_W].astype(F32), qg_ref[...]).astype(BF16)
    q = _dot(cqn, wq_ref[...]) * scale
    kvw = QK_NOPE + V_HEAD
    for h in range(MLA_HEADS):
        k_ref[0, h, :, 0:QK_NOPE] = kv[:, h * kvw:h * kvw + QK_NOPE].astype(BF16)
        k_ref[0, h, :, QK_NOPE:QK_W] = kr
        v_ref[0, h, :, 0:V_HEAD] = kv[:, h * kvw + QK_NOPE:(h + 1) * kvw].astype(BF16)
        v_ref[0, h, :, V_HEAD:2 * V_HEAD] = jnp.ones((kv.shape[0], V_HEAD), BF16)
        q_ref[0, h, :, 0:QK_NOPE] = q[:, h * QK_W:h * QK_W + QK_NOPE].astype(BF16)
        q_ref[0, h, :, QK_NOPE:QK_W] = _rope_pair(q[:, h * QK_W + QK_NOPE:(h + 1) * QK_W], cs).astype(BF16)


def _mla_prep(p, cs, kv_norm_g, q_norm_g, w_ukv, w_q2, B, n):
    tm = min(512, n)
    nt = n // tm
    H = MLA_HEADS
    kv_lora, q_lora = w_ukv.shape[0], w_q2.shape[0]
    blocks = (_nbytes((tm, HEAD_W), BF16) + _nbytes((tm, 128), F32) + _nbytes(w_ukv.shape, BF16)
              + _nbytes(w_q2.shape, BF16) + 2 * _nbytes((H, tm, QK_W), BF16) + _nbytes((H, tm, 2 * V_HEAD), BF16))
    qk_spec = pl.BlockSpec((1, H, tm, QK_W), lambda b, i: (b, 0, i, 0))
    return pl.pallas_call(
        functools.partial(_prep_kernel, scale=float(QK_NOPE + QK_ROPE) ** -0.5),
        out_shape=(jax.ShapeDtypeStruct((B, H, n, QK_W), BF16),
                   jax.ShapeDtypeStruct((B, H, n, QK_W), BF16),
                   jax.ShapeDtypeStruct((B, H, n, 2 * V_HEAD), BF16)),
        grid=(B, nt),
        in_specs=[pl.BlockSpec((tm, HEAD_W), lambda b, i: (b * nt + i, 0)),
                  pl.BlockSpec((tm, 128), lambda b, i: (i, 0)),
                  pl.BlockSpec((1, kv_lora), lambda b, i: (0, 0)),
                  pl.BlockSpec((1, q_lora), lambda b, i: (0, 0)),
                  pl.BlockSpec(w_ukv.shape, lambda b, i: (0, 0)),
                  pl.BlockSpec(w_q2.shape, lambda b, i: (0, 0))],
        out_specs=(qk_spec, qk_spec, pl.BlockSpec((1, H, tm, 2 * V_HEAD), lambda b, i: (b, 0, i, 0))),
        compiler_params=_compiler_params(("parallel", "parallel"), blocks, 0, 6 * _nbytes((tm, HEAD_W), F32)),
        name="mla_prep",
    )(p, cs, kv_norm_g.reshape(1, -1), q_norm_g.reshape(1, -1), w_ukv, w_q2)


_NT = (((1,), (1,)), ((), ()))


def _attn_kernel(*refs, with_x, tk):
    if with_x:
        q_ref, kc_ref, vc_ref, kx_ref, vx_ref, o_ref = refs
    else:
        q_ref, kc_ref, vc_ref, o_ref = refs
    q = q_ref[0, 0]

    s = lax.dot_general(q, kc_ref[0, 0], _NT, preferred_element_type=F32)
    m = jnp.max(s, axis=-1, keepdims=True)
    acc = _dot(jnp.exp(s - m).astype(BF16), vc_ref[0, 0])
    if with_x:
        for c in range(kx_ref.shape[2] // tk):
            s = lax.dot_general(q, kx_ref[0, 0, c * tk:(c + 1) * tk, :], _NT, preferred_element_type=F32)
            m_new = jnp.maximum(m, jnp.max(s, axis=-1, keepdims=True))
            p = jnp.exp(s - m_new).astype(BF16)
            acc = acc * jnp.exp(m - m_new) + _dot(p, vx_ref[0, 0, c * tk:(c + 1) * tk, :])
            m = m_new
    o_ref[0] = (acc[:, :V_HEAD] / acc[:, V_HEAD:]).astype(o_ref.dtype)


def _attention(q, kc, vc, kx=None, vx=None):
    B, H, nq, _ = q.shape
    nc = kc.shape[2]
    with_x = kx is not None
    nk = nc + (kx.shape[2] if with_x else 0)
    tq = min(512, nq)
    tk = min(512, nq)
    q_spec = pl.BlockSpec((1, 1, tq, QK_W), lambda b, h, i: (b, h, i, 0))

    def full(arr):
        return pl.BlockSpec((1, 1) + arr.shape[2:], lambda b, h, i: (b, h, 0, 0))

    args = [q, kc, vc] + ([kx, vx] if with_x else [])
    blocks = _nbytes((tq, QK_W), BF16) + _nbytes((nk, QK_W + 2 * V_HEAD), BF16) + _nbytes((tq, V_HEAD), BF16)
    return pl.pallas_call(
        functools.partial(_attn_kernel, with_x=with_x, tk=tk),
        out_shape=jax.ShapeDtypeStruct((B, nq, H * V_HEAD), BF16),
        grid=(B, H, nq // tq),
        in_specs=[q_spec] + [full(a) for a in args[1:]],
        out_specs=pl.BlockSpec((1, tq, V_HEAD), lambda b, h, i: (b, i, h)),
        compiler_params=_compiler_params(("parallel", "parallel", "parallel"), blocks, 0,
                                         6 * _nbytes((tq, tk), F32)),
        name="attention",
    )(*args)


def _fft1_kernel(u_ref, m_ref, tc_ref, ts_ref, o_ref, *, reps):
    rows = FFT_N1 * FFT_N1
    u = u_ref[0, :, 0].reshape(rows, u_ref.shape[-1])
    g = _dot(m_ref[...], u)
    gr, gi = g[:rows], g[rows:]
    tc = jnp.concatenate([tc_ref[:, 0].reshape(rows, 128)] * reps, axis=1)
    ts = jnp.concatenate([ts_ref[:, 0].reshape(rows, 128)] * reps, axis=1)
    shape3 = (FFT_N1, FFT_N1, u_ref.shape[-1])
    o_ref[0, 0, :, 0] = (gr * tc + gi * ts).reshape(shape3).astype(o_ref.dtype)
    o_ref[0, 1, :, 0] = (gi * tc - gr * ts).reshape(shape3).astype(o_ref.dtype)


def _fft2_kernel(g_ref, fa_ref, fb_ref, o_ref, *, n2):
    y = _dot(fa_ref[...], g_ref[0, 0, 0]) + _dot(fb_ref[...], g_ref[0, 1, 0])
    o_ref[0, 0, 0] = y[:n2].astype(o_ref.dtype)
    o_ref[0, 1, 0] = y[n2:].astype(o_ref.dtype)


def _dft_tables(n):
    n1 = FFT_N1
    n2 = n // n1
    a = np.arange(n1)
    ang1 = 2.0 * np.pi * ((a[:, None] * a[None, :]) % n1) / n1
    eye = np.eye(n1)
    m1 = np.concatenate([np.kron(np.cos(ang1), eye), -np.kron(np.sin(ang1), eye)], axis=0)
    b = np.arange(n2)
    angt = 2.0 * np.pi * (a[:, None] * b[None, :]) / n
    shape4 = (n1, n2 // n1, n1, 128)
    tc = np.broadcast_to(np.cos(angt)[:, :, None], (n1, n2, 128)).reshape(shape4)
    ts = np.broadcast_to(np.sin(angt)[:, :, None], (n1, n2, 128)).reshape(shape4)
    ang2 = 2.0 * np.pi * ((b[:, None] * b[None, :]) % n2) / n2
    c2, s2 = np.cos(ang2), np.sin(ang2)
    fa = np.concatenate([c2, -s2], axis=0)
    fb = np.concatenate([s2, c2], axis=0)
    to_bf16 = lambda t: jnp.asarray(t, F32).astype(BF16)
    return to_bf16(m1), jnp.asarray(tc, F32), jnp.asarray(ts, F32), to_bf16(fa), to_bf16(fb)


def _position_dft(p, col_block, C, B, n):
    n1 = FFT_N1
    n2 = n // n1
    nb = n2 // n1
    m1, tc, ts, fa, fb = _dft_tables(n)
    p5 = p.reshape(B, n1, nb, n1, p.shape[1])
    blocks = (3 * _nbytes((n1 * n1, C), BF16) + _nbytes(m1.shape, BF16) + 2 * _nbytes((n1 * n1, 128), F32))
    g = pl.pallas_call(
        functools.partial(_fft1_kernel, reps=C // 128),
        out_shape=jax.ShapeDtypeStruct((B, 2, n1, nb, n1, C), BF16),
        grid=(B, nb),
        in_specs=[pl.BlockSpec((1, n1, 1, n1, C), lambda b, r: (b, 0, r, 0, col_block)),
                  pl.BlockSpec(m1.shape, lambda b, r: (0, 0)),
                  pl.BlockSpec((n1, 1, n1, 128), lambda b, r: (0, r, 0, 0)),
                  pl.BlockSpec((n1, 1, n1, 128), lambda b, r: (0, r, 0, 0))],
        out_specs=pl.BlockSpec((1, 2, n1, 1, n1, C), lambda b, r: (b, 0, 0, r, 0, 0)),
        compiler_params=_compiler_params(("parallel", "parallel"), blocks, 0, 8 * _nbytes((n1 * n1, C), F32)),
        name="position_dft_stage1",
    )(p5, m1, tc, ts)
    g = g.reshape(B, 2, n1, n2, C)
    blocks = 4 * _nbytes((n2, C), BF16) + 2 * _nbytes(fa.shape, BF16)
    return pl.pallas_call(
        functools.partial(_fft2_kernel, n2=n2),
        out_shape=jax.ShapeDtypeStruct((B, 2, n1, n2, C), BF16),
        grid=(B, n1),
        in_specs=[pl.BlockSpec((1, 2, 1, n2, C), lambda b, d: (b, 0, d, 0, 0)),
                  pl.BlockSpec(fa.shape, lambda b, d: (0, 0)),
                  pl.BlockSpec(fb.shape, lambda b, d: (0, 0))],
        out_specs=pl.BlockSpec((1, 2, 1, n2, C), lambda b, d: (b, 0, d, 0, 0)),
        compiler_params=_compiler_params(("parallel", "parallel"), blocks, 0, 4 * _nbytes((n2, C), F32)),
        name="position_dft_stage2",
    )(g, fa, fb)


def _branch_kernel(attn_ref, vr_ref, vi_ref, zA_ref, zB_ref, xC_ref, bC_ref, cC_ref, zC_ref, uD_ref, vD_ref,
                   zD_ref, xCp_ref, cCp_ref, xCn_ref, cCn_ref, ccb_ref, scb_ref, convw_ref, convb_ref, lng_ref,
                   lnb_ref, ws_ref, bsb_ref, ys_ref, fr_ref, fi_ref, *, n, tm, halo, fscale):
    i = pl.program_id(0)

    def gate(ref):
        return _silu(ref[...].astype(F32))

    ys_ref[0] = (attn_ref[...].astype(F32) * gate(zA_ref)).astype(BF16)
    gwf = ccb_ref.shape[0]
    for d in range(FFT_N1):
        rows = pl.ds(d, tm // FFT_N1, stride=FFT_N1)
        vr, vi = vr_ref[0, 0, d].astype(F32), vi_ref[0, 0, d].astype(F32)
        for g in range(fr_ref.shape[0]):
            fr_ref[g, rows, :] = vr[:, g * gwf:(g + 1) * gwf]
            fi_ref[g, rows, :] = vi[:, g * gwf:(g + 1) * gwf]
    four = jnp.concatenate(
        [_dot(fr_ref[g].astype(BF16), ccb_ref[...]) + _dot(fi_ref[g].astype(BF16), scb_ref[...])
         for g in range(fr_ref.shape[0])], axis=1)
    ys_ref[1] = (four * fscale * gate(zB_ref)).astype(BF16)
    u = cC_ref[...].astype(F32) * xC_ref[...].astype(F32)
    first = (i * tm) % n == 0
    last = ((i + 1) * tm) % n == 0
    up = (cCp_ref[...].astype(F32) * xCp_ref[...].astype(F32))[halo - 1:halo, :]
    un = (cCn_ref[...].astype(F32) * xCn_ref[...].astype(F32))[0:1, :]
    up = up * jnp.where(first, 0.0, 1.0)
    un = un * jnp.where(last, 0.0, 1.0)
    row = lax.broadcasted_iota(jnp.int32, u.shape, 0)
    prev = jnp.where(row == 0, up, pltpu.roll(u, 1, axis=0))
    nxt = jnp.where(row == tm - 1, un, pltpu.roll(u, tm - 1, axis=0))
    conv = prev * convw_ref[0:1, :] + u * convw_ref[1:2, :] + nxt * convw_ref[2:3, :] + convb_ref[...]
    ys_ref[2] = (bC_ref[...].astype(F32) * conv * gate(zC_ref)).astype(BF16)
    v = vD_ref[...].astype(F32)
    vc = v - jnp.mean(v, axis=-1, keepdims=True)
    vn = vc * lax.rsqrt(jnp.mean(vc * vc, axis=-1, keepdims=True) + EPS) * lng_ref[...] + lnb_ref[...]
    vn = vn.astype(BF16)
    ug = uD_ref[...].astype(F32) * gate(zD_ref)
    gw = vn.shape[1] // ws_ref.shape[0]
    for g in range(ws_ref.shape[0]):
        for c in range(tm // CHUNK):
            rs, cs = slice(c * CHUNK, (c + 1) * CHUNK), slice(g * gw, (g + 1) * gw)
            mixed = _dot(ws_ref[g], vn[rs, cs]) + bsb_ref[g]
            ys_ref[3, rs, cs] = (ug[rs, cs] * mixed).astype(BF16)


def _mix_kernel(ys_ref, g0_ref, g1_ref, g2_ref, g3_ref, wb_ref, wo_ref, x_ref, mod_ref, postg_ref, o_ref,
                merged_ref, acc_ref, ss_ref, *, nj, ne):
    j = pl.program_id(1)

    @pl.when(j < nj)
    def _():
        merged = None
        for g, g_ref in enumerate((g0_ref, g1_ref, g2_ref, g3_ref)):
            term = _sigmoid(g_ref[...].astype(F32)) * _dot(ys_ref[g], wb_ref[g])
            merged = term if merged is None else merged + term
        merged_ref[j] = merged.astype(BF16)

    @pl.when((j >= nj) & (j < nj + ne))
    def _():
        merged = jnp.concatenate([merged_ref[c] for c in range(nj)], axis=1)
        part = _dot(merged, wo_ref[...])
        acc_ref[j - nj] = part
        ss = jnp.broadcast_to(jnp.sum(part * part, axis=-1, keepdims=True), ss_ref.shape)

        @pl.when(j == nj)
        def _():
            ss_ref[...] = ss

        @pl.when(j > nj)
        def _():
            ss_ref[...] += ss

    @pl.when(j >= nj + ne)
    def _():
        width = acc_ref.shape[0] * acc_ref.shape[2]
        y = acc_ref[j - nj - ne] * lax.rsqrt(ss_ref[:, 0:1] / width + EPS) * postg_ref[...]
        o_ref[...] = x_ref[...] + mod_ref[0, 2:3, :] * y


def _mix(p, attn, v, x2, mod3, mod_row, consts, n, tile_rows):
    R, D = x2.shape
    bw = attn.shape[1]

    tb = min(512, n)
    halo = 16
    nt = n // tb
    cl = tb // FFT_N1

    def seg(k):
        return pl.BlockSpec((tb, bw), lambda i: (i, k))

    def halo_prev(k):
        return pl.BlockSpec((halo, bw), lambda i: (jnp.maximum(i * (tb // halo) - 1, 0), k))

    def halo_next(k):
        return pl.BlockSpec((halo, bw), lambda i: (jnp.minimum((i + 1) * (tb // halo), R // halo - 1), k))

    def const(arr):
        nd = arr.ndim
        return pl.BlockSpec(arr.shape, lambda i: (0,) * nd)

    branch_consts = [consts[k] for k in ("ccb", "scb", "conv_w", "conv_b", "ln_g", "ln_b", "w_s", "b_s")]
    gwf = bw // FOURIER_GROUPS
    blocks = (12 * _nbytes((tb, bw), BF16) + 4 * _nbytes((halo, bw), BF16) + N_BRANCH * _nbytes((tb, bw), BF16)
              + sum(_nbytes(a.shape, a.dtype) for a in branch_consts))
    ys = pl.pallas_call(
        functools.partial(_branch_kernel, n=n, tm=tb, halo=halo, fscale=float(n * gwf) ** -0.5),
        out_shape=jax.ShapeDtypeStruct((N_BRANCH, R, bw), BF16),
        grid=(R // tb,),
        in_specs=([pl.BlockSpec((tb, bw), lambda i: (i, 0)),
                   pl.BlockSpec((1, 1, FFT_N1, cl, bw), lambda i: (i // nt, 0, 0, i % nt, 0)),
                   pl.BlockSpec((1, 1, FFT_N1, cl, bw), lambda i: (i // nt, 1, 0, i % nt, 0))]
                  + [seg(k) for k in (0, 2, 3, 4, 5, 6, 7, 8, 9)]
                  + [halo_prev(3), halo_prev(5), halo_next(3), halo_next(5)]
                  + [const(a) for a in branch_consts]),
        out_specs=pl.BlockSpec((N_BRANCH, tb, bw), lambda i: (0, i, 0)),
        scratch_shapes=[pltpu.VMEM((FOURIER_GROUPS, tb, gwf), F32), pltpu.VMEM((FOURIER_GROUPS, tb, gwf), F32)],
        compiler_params=_compiler_params(("parallel",), blocks, 2 * _nbytes((tb, bw), F32),
                                         12 * _nbytes((tb, bw), F32)),
        name="branch_outputs",
    )(attn, v, v, *([p] * 9), *([p] * 4), *branch_consts)

    tm = min(1024, tile_rows)
    tn = min(512, D)
    te = min(512, D)
    nj, ne = D // tn, D // te
    ni = R // tm
    gate0 = N_BRANCH_COLS * bw // tn
    ia = lambda i, j: jnp.where(j < nj, i, jnp.minimum(i + 1, ni - 1))
    ja = lambda j: jnp.where(j < nj, j, 0)
    jb = lambda j: jnp.clip(j - nj, 0, ne - 1)
    jc = lambda j: jnp.maximum(j - nj - ne, 0)

    def gates(g):
        return pl.BlockSpec((tm, tn), lambda i, j: (ia(i, j), gate0 + g * nj + ja(j)))

    blocks = (_nbytes((N_BRANCH, tm, bw), BF16) + N_BRANCH * _nbytes((tm, tn), BF16)
              + _nbytes((N_BRANCH, bw, tn), BF16) + _nbytes((D, te), BF16) + 2 * _nbytes((tm, te), F32))
    scratch = _nbytes((tm, D), BF16) + _nbytes((tm, D), F32) + _nbytes((tm, 128), F32)
    return pl.pallas_call(
        functools.partial(_mix_kernel, nj=nj, ne=ne),
        out_shape=jax.ShapeDtypeStruct((R, D), F32),
        grid=(ni, nj + 2 * ne),
        in_specs=([pl.BlockSpec((N_BRANCH, tm, bw), lambda i, j: (0, ia(i, j), 0))]
                  + [gates(g) for g in range(N_BRANCH)]
                  + [pl.BlockSpec((N_BRANCH, bw, tn), lambda i, j: (0, 0, ja(j))),
                     pl.BlockSpec((D, te), lambda i, j: (0, jb(j))),
                     pl.BlockSpec((tm, te), lambda i, j: (i, jc(j))),
                     pl.BlockSpec((1, 3, te), lambda i, j: (mod_row(i * tm), 0, jc(j))),
                     pl.BlockSpec((1, te), lambda i, j: (0, jc(j)))]),
        out_specs=pl.BlockSpec((tm, te), lambda i, j: (i, jc(j))),
        scratch_shapes=[pltpu.VMEM((nj, tm, tn), BF16), pltpu.VMEM((ne, tm, te), F32),
                        pltpu.VMEM((tm, 128), F32)],
        compiler_params=_compiler_params(("arbitrary", "arbitrary"), blocks, scratch,
                                         _nbytes((tm, D), BF16) + 6 * _nbytes((tm, tn), F32)),
        name="branch_mix",
    )(ys, p, p, p, p, consts["w_branch"], consts["w_out"], x2, mod3, consts["post_g"])


def _swap_halves(w):
    h = w.shape[-1] // 2
    return jnp.concatenate([w[..., h:], w[..., :h]], axis=-1)


def _relayout_w_in(w):
    rope_end = ROPE_OFF + QK_ROPE
    head_end = rope_end + HEAD_W - CQ_OFF
    rope = w[:, ROPE_OFF:rope_end]
    pad = jnp.zeros((w.shape[0], CQ_OFF - ROPE_OFF - 2 * QK_ROPE), w.dtype)
    head = jnp.concatenate([w[:, :ROPE_OFF], rope, _swap_halves(rope), pad, w[:, rope_end:head_end]], axis=1)
    return head.astype(BF16), w[:, head_end:].astype(BF16)


def _relayout_w_uq(w):
    w = w.reshape(w.shape[0], MLA_HEADS, QK_NOPE + QK_ROPE)
    rope = w[..., QK_NOPE:]
    return jnp.concatenate([w[..., :QK_NOPE], rope, _swap_halves(rope)], axis=-1).reshape(w.shape[0], -1).astype(BF16)


def _rope_table(n, rotate):
    half = QK_ROPE // 2
    if rotate:
        pos = np.arange(n)
        inv = ROPE_THETA ** (-np.arange(0, half, 2, dtype=np.float64) / half)
        ang = np.concatenate([(pos // GRID_W)[:, None] * inv, (pos % GRID_W)[:, None] * inv], axis=-1)
    else:
        ang = np.zeros((n, half))
    cos, sin = np.cos(ang), np.sin(ang)
    return jnp.asarray(np.concatenate([cos, cos, -sin, sin], axis=-1), F32)


def _channel_dft(bw):
    gw = bw // FOURIER_GROUPS
    a = np.arange(gw)
    ang = 2.0 * np.pi * ((a[:, None] * a[None, :]) % gw) / gw
    return jnp.asarray(np.cos(ang), F32).astype(BF16), jnp.asarray(np.sin(ang), F32).astype(BF16)


def kernel(x, c, ctx, c_ctx, w_mod, b_mod, pre_g, post_g, w_in, q_norm_g, kv_norm_g, w_uq, w_ukv,
           conv_w, conv_b, sgu_ln_g, sgu_ln_b, sgu_w, sgu_b, w_branch, w_out):
    B, n, D = x.shape
    nc = ctx.shape[1]
    depth = w_in.shape[0]
    bw = w_branch.shape[2]
    assert w_in.shape[2] == ROPE_OFF + QK_ROPE + (HEAD_W - CQ_OFF) + N_BRANCH_COLS * bw + N_BRANCH * D
    assert kv_norm_g.shape[1] == ROPE_OFF and q_norm_g.shape[1] == HEAD_W - CQ_OFF and B <= 4

    cc = jnp.zeros((8, D), F32).at[:B].set(c).at[B].set(c_ctx)
    mod = _modulation(cc, w_mod, b_mod).reshape(depth, 8, 3, D)

    cs_x, cs_c = _rope_table(n, True), _rope_table(nc, False)
    ccb, scb = _channel_dft(bw)
    x2 = x.reshape(B * n, D)
    c2 = ctx.reshape(B * nc, D)
    row_x = lambda r: r // n
    row_c = lambda r: B

    for l in range(depth):
        last = l == depth - 1
        w_head, w_rest = _relayout_w_in(w_in[l])
        w_q2 = _relayout_w_uq(w_uq[l])
        w_ukv_l = w_ukv[l].astype(BF16)
        consts = dict(
            ccb=ccb, scb=scb, conv_w=conv_w[l], conv_b=conv_b[l].reshape(1, bw),
            ln_g=sgu_ln_g[l].reshape(1, bw), ln_b=sgu_ln_b[l].reshape(1, bw),
            w_s=sgu_w[l].astype(BF16),
            b_s=jnp.broadcast_to(sgu_b[l][:, :, None], sgu_b.shape[1:] + (bw // sgu_w.shape[1],)),
            w_branch=w_branch[l].astype(BF16), w_out=w_out[l].astype(BF16), post_g=post_g[l].reshape(1, D))

        hx, px = _in_projection(x2, mod[l], row_x, pre_g[l], w_head, w_rest, min(1024, n))
        hc, *pc = _in_projection(c2, mod[l], row_c, pre_g[l], w_head, None if last else w_rest,
                                 min(1024, B * nc))
        q_c, k_c, v_c = _mla_prep(hc, cs_c, kv_norm_g[l], q_norm_g[l], w_ukv_l, w_q2, B, nc)
        q_x, k_x, v_x = _mla_prep(hx, cs_x, kv_norm_g[l], q_norm_g[l], w_ukv_l, w_q2, B, n)

        attn_x = _attention(q_x, k_c, v_c, k_x, v_x).reshape(B * n, -1)
        four_x = _position_dft(px, 1, bw, B, n)
        new_x = _mix(px, attn_x, four_x, x2, mod[l], row_x, consts, n, n)
        if not last:
            attn_c = _attention(q_c, k_c, v_c).reshape(B * nc, -1)
            four_c = _position_dft(pc[0], 1, bw, B, nc)
            c2 = _mix(pc[0], attn_c, four_c, c2, mod[l], row_c, consts, nc, B * nc)
        x2 = new_x
    return x2.reshape(B, n, D)
```

```python
import functools

import jax
import jax.numpy as jnp
import numpy as np
from jax import lax
from jax.experimental import pallas as pl
from jax.experimental.pallas import tpu as pltpu

F32 = jnp.float32
BF16 = jnp.bfloat16

MLA_HEADS = 4
QK_NOPE = 128
QK_ROPE = 64
V_HEAD = 128
GRID_W = 64
ROPE_THETA = 10000.0
FOURIER_GROUPS = 4
CHUNK = 128
N_BRANCH = 4
N_BRANCH_COLS = 10
EPS = 1e-6

ROPE_OFF = 256
CQ_OFF = 512
HEAD_W = 1024
QK_W = QK_NOPE + 2 * QK_ROPE

FFT_N1 = 16

V7X_VMEM_BYTES = 64 * 2**20
VMEM_CAP_BYTES = V7X_VMEM_BYTES - 8 * 2**20


def _nbytes(shape, dtype):
    return int(np.prod(shape)) * jnp.dtype(dtype).itemsize


def _compiler_params(semantics, block_bytes, scratch_bytes=0, temp_bytes=0):
    need = 2 * block_bytes + scratch_bytes + temp_bytes + 2 * 2**20
    return pltpu.CompilerParams(dimension_semantics=semantics,
                                vmem_limit_bytes=int(min(max(need, 16 * 2**20), VMEM_CAP_BYTES)))


def _sigmoid(v):
    return 0.5 * jnp.tanh(0.5 * v) + 0.5


def _silu(v):
    return v * _sigmoid(v)


def _rms(v, g):
    return v * lax.rsqrt(jnp.mean(v * v, axis=-1, keepdims=True) + EPS) * g


def _dot(a, b):
    return jnp.dot(a, b, preferred_element_type=F32)


def _mod_kernel(c_ref, w_ref, b_ref, o_ref):
    s = _silu(c_ref[...]).astype(BF16)
    o_ref[0] = _dot(s, w_ref[0].astype(BF16)) + b_ref[0]


def _modulation(cc, w_mod, b_mod):
    L, D, W = w_mod.shape
    tn = 768 if W % 768 == 0 else W
    blocks = _nbytes((D, tn), F32) + _nbytes((8, D), F32) + 2 * _nbytes((8, tn), F32)
    return pl.pallas_call(
        _mod_kernel,
        out_shape=jax.ShapeDtypeStruct((L, 8, W), F32),
        grid=(L, W // tn),
        in_specs=[pl.BlockSpec((8, D), lambda l, j: (0, 0)),
                  pl.BlockSpec((1, D, tn), lambda l, j: (l, 0, j)),
                  pl.BlockSpec((1, 1, tn), lambda l, j: (l, 0, j))],
        out_specs=pl.BlockSpec((1, 8, tn), lambda l, j: (l, 0, j)),
        compiler_params=_compiler_params(("parallel", "parallel"), blocks, 0, _nbytes((D, tn), BF16)),
        name="modulation",
    )(cc, w_mod, b_mod.reshape(L, 1, W))


def _inproj_kernel(*refs, with_rest):
    if with_rest:
        x_ref, mod_ref, g_ref, wh_ref, wr_ref, oh_ref, or_ref, h_ref = refs
    else:
        x_ref, mod_ref, g_ref, wh_ref, oh_ref, h_ref = refs
    j = pl.program_id(1)

    @pl.when(j == 0)
    def _():
        y = _rms(x_ref[...], g_ref[...])
        h_ref[...] = (y * (1.0 + mod_ref[0, 1:2, :]) + mod_ref[0, 0:1, :]).astype(BF16)
        oh_ref[...] = _dot(h_ref[...], wh_ref[...]).astype(oh_ref.dtype)

    if with_rest:
        @pl.when(j > 0)
        def _():
            or_ref[...] = _dot(h_ref[...], wr_ref[...]).astype(or_ref.dtype)


def _in_projection(x2, mod3, mod_row, pre_g, w_head, w_rest, tm):
    R, D = x2.shape
    tn = w_head.shape[1]
    with_rest = w_rest is not None
    nj = 1 + (w_rest.shape[1] // tn if with_rest else 0)
    rest_col = lambda j: jnp.maximum(j - 1, 0)
    in_specs = [pl.BlockSpec((tm, D), lambda i, j: (i, 0)),
                pl.BlockSpec((1, 3, D), lambda i, j: (mod_row(i * tm), 0, 0)),
                pl.BlockSpec((1, D), lambda i, j: (0, 0)),
                pl.BlockSpec((D, tn), lambda i, j: (0, 0))]
    out_specs = [pl.BlockSpec((tm, tn), lambda i, j: (i, 0))]
    out_shape = [jax.ShapeDtypeStruct((R, tn), BF16)]
    args = [x2, mod3, pre_g.reshape(1, D), w_head]
    if with_rest:
        in_specs.append(pl.BlockSpec((D, tn), lambda i, j: (0, rest_col(j))))
        out_specs.append(pl.BlockSpec((tm, tn), lambda i, j: (i, rest_col(j))))
        out_shape.append(jax.ShapeDtypeStruct((R, w_rest.shape[1]), BF16))
        args.append(w_rest)
    blocks = (_nbytes((tm, D), F32) + _nbytes((3, D), F32) + 2 * _nbytes((D, tn), BF16)
              + 2 * _nbytes((tm, tn), BF16))
    return pl.pallas_call(
        functools.partial(_inproj_kernel, with_rest=with_rest),
        out_shape=out_shape,
        grid=(R // tm, nj),
        in_specs=in_specs,
        out_specs=out_specs,
        scratch_shapes=[pltpu.VMEM((tm, D), BF16)],
        compiler_params=_compiler_params(("parallel", "arbitrary"), blocks, _nbytes((tm, D), BF16),
                                         2 * _nbytes((tm, D), F32) + _nbytes((tm, tn), F32)),
        name="in_projection",
    )(*args)


def _rope_pair(group, cs):
    r = group * cs
    return r + pltpu.roll(r, QK_ROPE, axis=1)


def _prep_kernel(p_ref, cs_ref, kvg_ref, qg_ref, wkv_ref, wq_ref, q_ref, k_ref, v_ref, *, scale):
    cs = cs_ref[...]
    kvn = _rms(p_ref[:, 0:ROPE_OFF].astype(F32), kvg_ref[...]).astype(BF16)
    kv = _dot(kvn, wkv_ref[...])
    kr = _rope_pair(p_ref[:, ROPE_OFF:ROPE_OFF + 2 * QK_ROPE].astype(F32), cs)
    lane = lax.broadcasted_iota(jnp.int32, kr.shape, 1)
    kr = jnp.where(lane < QK_ROPE, kr, 0.0).astype(BF16)
    cqn = _rms(p_ref[:, CQ_OFF:HEAD_W].astype(F32), qg_ref[...]).astype(BF16)
    q = _dot(cqn, wq_ref[...]) * scale
    kvw = QK_NOPE + V_HEAD
    for h in range(MLA_HEADS):
        k_ref[0, h, :, 0:QK_NOPE] = kv[:, h * kvw:h * kvw + QK_NOPE].astype(BF16)
        k_ref[0, h, :, QK_NOPE:QK_W] = kr
        v_ref[0, h, :, 0:V_HEAD] = kv[:, h * kvw + QK_NOPE:(h + 1) * kvw].astype(BF16)
        v_ref[0, h, :, V_HEAD:2 * V_HEAD] = jnp.ones((kv.shape[0], V_HEAD), BF16)
        q_ref[0, h, :, 0:QK_NOPE] = q[:, h * QK_W:h * QK_W + QK_NOPE].astype(BF16)
        q_ref[0, h, :, QK_NOPE:QK_W] = _rope_pair(q[:, h * QK_W + QK_NOPE:(h + 1) * QK_W], cs).astype(BF16)


def _mla_prep(p, cs, kv_norm_g, q_norm_g, w_ukv, w_q2, B, n):
    tm = min(512, n)
    nt = n // tm
    H = MLA_HEADS
    kv_lora, q_lora = w_ukv.shape[0], w_q2.shape[0]
    blocks = (_nbytes((tm, HEAD_W), BF16) + _nbytes((tm, 128), F32) + _nbytes(w_ukv.shape, BF16)
              + _nbytes(w_q2.shape, BF16) + 2 * _nbytes((H, tm, QK_W), BF16) + _nbytes((H, tm, 2 * V_HEAD), BF16))
    qk_spec = pl.BlockSpec((1, H, tm, QK_W), lambda b, i: (b, 0, i, 0))
    return pl.pallas_call(
        functools.partial(_prep_kernel, scale=float(QK_NOPE + QK_ROPE) ** -0.5),
        out_shape=(jax.ShapeDtypeStruct((B, H, n, QK_W), BF16),
                   jax.ShapeDtypeStruct((B, H, n, QK_W), BF16),
                   jax.ShapeDtypeStruct((B, H, n, 2 * V_HEAD), BF16)),
        grid=(B, nt),
        in_specs=[pl.BlockSpec((tm, HEAD_W), lambda b, i: (b * nt + i, 0)),
                  pl.BlockSpec((tm, 128), lambda b, i: (i, 0)),
                  pl.BlockSpec((1, kv_lora), lambda b, i: (0, 0)),
                  pl.BlockSpec((1, q_lora), lambda b, i: (0, 0)),
                  pl.BlockSpec(w_ukv.shape, lambda b, i: (0, 0)),
                  pl.BlockSpec(w_q2.shape, lambda b, i: (0, 0))],
        out_specs=(qk_spec, qk_spec, pl.BlockSpec((1, H, tm, 2 * V_HEAD), lambda b, i: (b, 0, i, 0))),
        compiler_params=_compiler_params(("parallel", "parallel"), blocks, 0, 6 * _nbytes((tm, HEAD_W), F32)),
        name="mla_prep",
    )(p, cs, kv_norm_g.reshape(1, -1), q_norm_g.reshape(1, -1), w_ukv, w_q2)


_NT = (((1,), (1,)), ((), ()))


def _attn_kernel(*refs, with_x, tk):
    if with_x:
        q_ref, kc_ref, vc_ref, kx_ref, vx_ref, o_ref = refs
    else:
        q_ref, kc_ref, vc_ref, o_ref = refs
    q = q_ref[0, 0]

    s = lax.dot_general(q, kc_ref[0, 0], _NT, preferred_element_type=F32)
    m = jnp.max(s, axis=-1, keepdims=True)
    acc = _dot(jnp.exp(s - m).astype(BF16), vc_ref[0, 0])
    if with_x:
        for c in range(kx_ref.shape[2] // tk):
            s = lax.dot_general(q, kx_ref[0, 0, c * tk:(c + 1) * tk, :], _NT, preferred_element_type=F32)
            m_new = jnp.maximum(m, jnp.max(s, axis=-1, keepdims=True))
            p = jnp.exp(s - m_new).astype(BF16)
            acc = acc * jnp.exp(m - m_new) + _dot(p, vx_ref[0, 0, c * tk:(c + 1) * tk, :])
            m = m_new
    o_ref[0] = (acc[:, :V_HEAD] / acc[:, V_HEAD:]).astype(o_ref.dtype)


def _attention(q, kc, vc, kx=None, vx=None):
    B, H, nq, _ = q.shape
    nc = kc.shape[2]
    with_x = kx is not None
    nk = nc + (kx.shape[2] if with_x else 0)
    tq = min(512, nq)
    tk = min(512, nq)
    q_spec = pl.BlockSpec((1, 1, tq, QK_W), lambda b, h, i: (b, h, i, 0))

    def full(arr):
        return pl.BlockSpec((1, 1) + arr.shape[2:], lambda b, h, i: (b, h, 0, 0))

    args = [q, kc, vc] + ([kx, vx] if with_x else [])
    blocks = _nbytes((tq, QK_W), BF16) + _nbytes((nk, QK_W + 2 * V_HEAD), BF16) + _nbytes((tq, V_HEAD), BF16)
    return pl.pallas_call(
        functools.partial(_attn_kernel, with_x=with_x, tk=tk),
        out_shape=jax.ShapeDtypeStruct((B, nq, H * V_HEAD), BF16),
        grid=(B, H, nq // tq),
        in_specs=[q_spec] + [full(a) for a in args[1:]],
        out_specs=pl.BlockSpec((1, tq, V_HEAD), lambda b, h, i: (b, i, h)),
        compiler_params=_compiler_params(("parallel", "parallel", "parallel"), blocks, 0,
                                         6 * _nbytes((tq, tk), F32)),
        name="attention",
    )(*args)


def _fft1_kernel(u_ref, m_ref, tc_ref, ts_ref, o_ref, *, reps):
    rows = FFT_N1 * FFT_N1
    u = u_ref[0, :, 0].reshape(rows, u_ref.shape[-1])
    g = _dot(m_ref[...], u)
    gr, gi = g[:rows], g[rows:]
    tc = jnp.concatenate([tc_ref[:, 0].reshape(rows, 128)] * reps, axis=1)
    ts = jnp.concatenate([ts_ref[:, 0].reshape(rows, 128)] * reps, axis=1)
    shape3 = (FFT_N1, FFT_N1, u_ref.shape[-1])
    o_ref[0, 0, :, 0] = (gr * tc + gi * ts).reshape(shape3).astype(o_ref.dtype)
    o_ref[0, 1, :, 0] = (gi * tc - gr * ts).reshape(shape3).astype(o_ref.dtype)


def _fft2_kernel(g_ref, fa_ref, fb_ref, o_ref, *, n2):
    y = _dot(fa_ref[...], g_ref[0, 0, 0]) + _dot(fb_ref[...], g_ref[0, 1, 0])
    o_ref[0, 0, 0] = y[:n2].astype(o_ref.dtype)
    o_ref[0, 1, 0] = y[n2:].astype(o_ref.dtype)


def _dft_tables(n):
    n1 = FFT_N1
    n2 = n // n1
    a = np.arange(n1)
    ang1 = 2.0 * np.pi * ((a[:, None] * a[None, :]) % n1) / n1
    eye = np.eye(n1)
    m1 = np.concatenate([np.kron(np.cos(ang1), eye), -np.kron(np.sin(ang1), eye)], axis=0)
    b = np.arange(n2)
    angt = 2.0 * np.pi * (a[:, None] * b[None, :]) / n
    shape4 = (n1, n2 // n1, n1, 128)
    tc = np.broadcast_to(np.cos(angt)[:, :, None], (n1, n2, 128)).reshape(shape4)
    ts = np.broadcast_to(np.sin(angt)[:, :, None], (n1, n2, 128)).reshape(shape4)
    ang2 = 2.0 * np.pi * ((b[:, None] * b[None, :]) % n2) / n2
    c2, s2 = np.cos(ang2), np.sin(ang2)
    fa = np.concatenate([c2, -s2], axis=0)
    fb = np.concatenate([s2, c2], axis=0)
    to_bf16 = lambda t: jnp.asarray(t, F32).astype(BF16)
    return to_bf16(m1), jnp.asarray(tc, F32), jnp.asarray(ts, F32), to_bf16(fa), to_bf16(fb)


def _position_dft(p, col_block, C, B, n):
    n1 = FFT_N1
    n2 = n // n1
    nb = n2 // n1
    m1, tc, ts, fa, fb = _dft_tables(n)
    p5 = p.reshape(B, n1, nb, n1, p.shape[1])
    blocks = (3 * _nbytes((n1 * n1, C), BF16) + _nbytes(m1.shape, BF16) + 2 * _nbytes((n1 * n1, 128), F32))
    g = pl.pallas_call(
        functools.partial(_fft1_kernel, reps=C // 128),
        out_shape=jax.ShapeDtypeStruct((B, 2, n1, nb, n1, C), BF16),
        grid=(B, nb),
        in_specs=[pl.BlockSpec((1, n1, 1, n1, C), lambda b, r: (b, 0, r, 0, col_block)),
                  pl.BlockSpec(m1.shape, lambda b, r: (0, 0)),
                  pl.BlockSpec((n1, 1, n1, 128), lambda b, r: (0, r, 0, 0)),
                  pl.BlockSpec((n1, 1, n1, 128), lambda b, r: (0, r, 0, 0))],
        out_specs=pl.BlockSpec((1, 2, n1, 1, n1, C), lambda b, r: (b, 0, 0, r, 0, 0)),
        compiler_params=_compiler_params(("parallel", "parallel"), blocks, 0, 8 * _nbytes((n1 * n1, C), F32)),
        name="position_dft_stage1",
    )(p5, m1, tc, ts)
    g = g.reshape(B, 2, n1, n2, C)
    blocks = 4 * _nbytes((n2, C), BF16) + 2 * _nbytes(fa.shape, BF16)
    return pl.pallas_call(
        functools.partial(_fft2_kernel, n2=n2),
        out_shape=jax.ShapeDtypeStruct((B, 2, n1, n2, C), BF16),
        grid=(B, n1),
        in_specs=[pl.BlockSpec((1, 2, 1, n2, C), lambda b, d: (b, 0, d, 0, 0)),
                  pl.BlockSpec(fa.shape, lambda b, d: (0, 0)),
                  pl.BlockSpec(fb.shape, lambda b, d: (0, 0))],
        out_specs=pl.BlockSpec((1, 2, 1, n2, C), lambda b, d: (b, 0, d, 0, 0)),
        compiler_params=_compiler_params(("parallel", "parallel"), blocks, 0, 4 * _nbytes((n2, C), F32)),
        name="position_dft_stage2",
    )(g, fa, fb)


def _branch_kernel(attn_ref, vr_ref, vi_ref, zA_ref, zB_ref, xC_ref, bC_ref, cC_ref, zC_ref, uD_ref, vD_ref,
                   zD_ref, xCp_ref, cCp_ref, xCn_ref, cCn_ref, ccb_ref, scb_ref, convw_ref, convb_ref, lng_ref,
                   lnb_ref, ws_ref, bsb_ref, ys_ref, fr_ref, fi_ref, *, n, tm, halo, fscale):
    i = pl.program_id(0)

    def gate(ref):
        return _silu(ref[...].astype(F32))

    ys_ref[0] = (attn_ref[...].astype(F32) * gate(zA_ref)).astype(BF16)
    gwf = ccb_ref.shape[0]
    for d in range(FFT_N1):
        rows = pl.ds(d, tm // FFT_N1, stride=FFT_N1)
        vr, vi = vr_ref[0, 0, d].astype(F32), vi_ref[0, 0, d].astype(F32)
        for g in range(fr_ref.shape[0]):
            fr_ref[g, rows, :] = vr[:, g * gwf:(g + 1) * gwf]
            fi_ref[g, rows, :] = vi[:, g * gwf:(g + 1) * gwf]
    four = jnp.concatenate(
        [_dot(fr_ref[g].astype(BF16), ccb_ref[...]) + _dot(fi_ref[g].astype(BF16), scb_ref[...])
         for g in range(fr_ref.shape[0])], axis=1)
    ys_ref[1] = (four * fscale * gate(zB_ref)).astype(BF16)
    u = cC_ref[...].astype(F32) * xC_ref[...].astype(F32)
    first = (i * tm) % n == 0
    last = ((i + 1) * tm) % n == 0
    up = (cCp_ref[...].astype(F32) * xCp_ref[...].astype(F32))[halo - 1:halo, :]
    un = (cCn_ref[...].astype(F32) * xCn_ref[...].astype(F32))[0:1, :]
    up = up * jnp.where(first, 0.0, 1.0)
    un = un * jnp.where(last, 0.0, 1.0)
    row = lax.broadcasted_iota(jnp.int32, u.shape, 0)
    prev = jnp.where(row == 0, up, pltpu.roll(u, 1, axis=0))
    nxt = jnp.where(row == tm - 1, un, pltpu.roll(u, tm - 1, axis=0))
    conv = prev * convw_ref[0:1, :] + u * convw_ref[1:2, :] + nxt * convw_ref[2:3, :] + convb_ref[...]
    ys_ref[2] = (bC_ref[...].astype(F32) * conv * gate(zC_ref)).astype(BF16)
    v = vD_ref[...].astype(F32)
    vc = v - jnp.mean(v, axis=-1, keepdims=True)
    vn = vc * lax.rsqrt(jnp.mean(vc * vc, axis=-1, keepdims=True) + EPS) * lng_ref[...] + lnb_ref[...]
    vn = vn.astype(BF16)
    ug = uD_ref[...].astype(F32) * gate(zD_ref)
    gw = vn.shape[1] // ws_ref.shape[0]
    for g in range(ws_ref.shape[0]):
        for c in range(tm // CHUNK):
            rs, cs = slice(c * CHUNK, (c + 1) * CHUNK), slice(g * gw, (g + 1) * gw)
            mixed = _dot(ws_ref[g], vn[rs, cs]) + bsb_ref[g]
            ys_ref[3, rs, cs] = (ug[rs, cs] * mixed).astype(BF16)


def _mix_kernel(ys_ref, g0_ref, g1_ref, g2_ref, g3_ref, wb_ref, wo_ref, x_ref, mod_ref, postg_ref, o_ref,
                merged_ref, acc_ref, ss_ref, ssprev_ref, *, nj, ne, ni):
    i = pl.program_id(0)
    j = pl.program_id(1)

    @pl.when((j < nj) & (i < ni))
    def _():
        merged = None
        for g, g_ref in enumerate((g0_ref, g1_ref, g2_ref, g3_ref)):
            term = _sigmoid(g_ref[...].astype(F32)) * _dot(ys_ref[g], wb_ref[g])
            merged = term if merged is None else merged + term
        merged_ref[j] = merged.astype(BF16)

    @pl.when((j == nj) & (i > 0))
    def _():
        ssprev_ref[...] = ss_ref[...]

    @pl.when((j >= nj) & (i > 0))
    def _():
        width = acc_ref.shape[0] * acc_ref.shape[2]
        y = acc_ref[j - nj] * lax.rsqrt(ssprev_ref[:, 0:1] / width + EPS) * postg_ref[...]
        o_ref[...] = x_ref[...] + mod_ref[0, 2:3, :] * y

    @pl.when((j >= nj) & (i < ni))
    def _():
        merged = jnp.concatenate([merged_ref[c] for c in range(nj)], axis=1)
        part = _dot(merged, wo_ref[...])
        acc_ref[j - nj] = part
        ss = jnp.broadcast_to(jnp.sum(part * part, axis=-1, keepdims=True), ss_ref.shape)

        @pl.when(j == nj)
        def _():
            ss_ref[...] = ss

        @pl.when(j > nj)
        def _():
            ss_ref[...] += ss


def _mix(p, attn, v, x2, mod3, mod_row, consts, n, tile_rows):
    R, D = x2.shape
    bw = attn.shape[1]

    tb = min(512, n)
    halo = 16
    nt = n // tb
    cl = tb // FFT_N1

    def seg(k):
        return pl.BlockSpec((tb, bw), lambda i: (i, k))

    def halo_prev(k):
        return pl.BlockSpec((halo, bw), lambda i: (jnp.maximum(i * (tb // halo) - 1, 0), k))

    def halo_next(k):
        return pl.BlockSpec((halo, bw), lambda i: (jnp.minimum((i + 1) * (tb // halo), R // halo - 1), k))

    def const(arr):
        nd = arr.ndim
        return pl.BlockSpec(arr.shape, lambda i: (0,) * nd)

    branch_consts = [consts[k] for k in ("ccb", "scb", "conv_w", "conv_b", "ln_g", "ln_b", "w_s", "b_s")]
    gwf = bw // FOURIER_GROUPS
    blocks = (12 * _nbytes((tb, bw), BF16) + 4 * _nbytes((halo, bw), BF16) + N_BRANCH * _nbytes((tb, bw), BF16)
              + sum(_nbytes(a.shape, a.dtype) for a in branch_consts))
    ys = pl.pallas_call(
        functools.partial(_branch_kernel, n=n, tm=tb, halo=halo, fscale=float(n * gwf) ** -0.5),
        out_shape=jax.ShapeDtypeStruct((N_BRANCH, R, bw), BF16),
        grid=(R // tb,),
        in_specs=([pl.BlockSpec((tb, bw), lambda i: (i, 0)),
                   pl.BlockSpec((1, 1, FFT_N1, cl, bw), lambda i: (i // nt, 0, 0, i % nt, 0)),
                   pl.BlockSpec((1, 1, FFT_N1, cl, bw), lambda i: (i // nt, 1, 0, i % nt, 0))]
                  + [seg(k) for k in (0, 2, 3, 4, 5, 6, 7, 8, 9)]
                  + [halo_prev(3), halo_prev(5), halo_next(3), halo_next(5)]
                  + [const(a) for a in branch_consts]),
        out_specs=pl.BlockSpec((N_BRANCH, tb, bw), lambda i: (0, i, 0)),
        scratch_shapes=[pltpu.VMEM((FOURIER_GROUPS, tb, gwf), F32), pltpu.VMEM((FOURIER_GROUPS, tb, gwf), F32)],
        compiler_params=_compiler_params(("parallel",), blocks, 2 * _nbytes((tb, bw), F32),
                                         12 * _nbytes((tb, bw), F32)),
        name="branch_outputs",
    )(attn, v, v, *([p] * 9), *([p] * 4), *branch_consts)

    tm = min(1024, tile_rows)
    tn = min(512, D)
    te = min(512, D)
    nj, ne = D // tn, D // te
    ni = R // tm
    gate0 = N_BRANCH_COLS * bw // tn
    ia = lambda i, j: jnp.minimum(jnp.where(j < nj, i, i + 1), ni - 1)
    ja = lambda j: jnp.where(j < nj, j, 0)
    jb = lambda j: jnp.maximum(j - nj, 0)
    ip = lambda i: jnp.maximum(i - 1, 0)
    jo = lambda i, j: jnp.where(i == 0, 0, jb(j))

    def gates(g):
        return pl.BlockSpec((tm, tn), lambda i, j: (ia(i, j), gate0 + g * nj + ja(j)))

    blocks = (_nbytes((N_BRANCH, tm, bw), BF16) + N_BRANCH * _nbytes((tm, tn), BF16)
              + _nbytes((N_BRANCH, bw, tn), BF16) + _nbytes((D, te), BF16) + 2 * _nbytes((tm, te), F32))
    scratch = _nbytes((tm, D), BF16) + _nbytes((tm, D), F32) + 2 * _nbytes((tm, 128), F32)
    return pl.pallas_call(
        functools.partial(_mix_kernel, nj=nj, ne=ne, ni=ni),
        out_shape=jax.ShapeDtypeStruct((R, D), F32),
        grid=(ni + 1, nj + ne),
        in_specs=([pl.BlockSpec((N_BRANCH, tm, bw), lambda i, j: (0, ia(i, j), 0))]
                  + [gates(g) for g in range(N_BRANCH)]
                  + [pl.BlockSpec((N_BRANCH, bw, tn), lambda i, j: (0, 0, ja(j))),
                     pl.BlockSpec((D, te), lambda i, j: (0, jb(j))),
                     pl.BlockSpec((tm, te), lambda i, j: (ip(i), jb(j))),
                     pl.BlockSpec((1, 3, te), lambda i, j: (mod_row(ip(i) * tm), 0, jb(j))),
                     pl.BlockSpec((1, te), lambda i, j: (0, jb(j)))]),
        out_specs=pl.BlockSpec((tm, te), lambda i, j: (ip(i), jo(i, j))),
        scratch_shapes=[pltpu.VMEM((nj, tm, tn), BF16), pltpu.VMEM((ne, tm, te), F32),
                        pltpu.VMEM((tm, 128), F32), pltpu.VMEM((tm, 128), F32)],
        compiler_params=_compiler_params(("arbitrary", "arbitrary"), blocks, scratch,
                                         _nbytes((tm, D), BF16) + 6 * _nbytes((tm, tn), F32)),
        name="branch_mix",
    )(ys, p, p, p, p, consts["w_branch"], consts["w_out"], x2, mod3, consts["post_g"])


def _swap_halves(w):
    h = w.shape[-1] // 2
    return jnp.concatenate([w[..., h:], w[..., :h]], axis=-1)


def _relayout_w_in(w):
    rope_end = ROPE_OFF + QK_ROPE
    head_end = rope_end + HEAD_W - CQ_OFF
    rope = w[:, ROPE_OFF:rope_end]
    pad = jnp.zeros((w.shape[0], CQ_OFF - ROPE_OFF - 2 * QK_ROPE), w.dtype)
    head = jnp.concatenate([w[:, :ROPE_OFF], rope, _swap_halves(rope), pad, w[:, rope_end:head_end]], axis=1)
    return head.astype(BF16), w[:, head_end:].astype(BF16)


def _relayout_w_uq(w):
    w = w.reshape(w.shape[0], MLA_HEADS, QK_NOPE + QK_ROPE)
    rope = w[..., QK_NOPE:]
    return jnp.concatenate([w[..., :QK_NOPE], rope, _swap_halves(rope)], axis=-1).reshape(w.shape[0], -1).astype(BF16)


def _rope_table(n, rotate):
    half = QK_ROPE // 2
    if rotate:
        pos = np.arange(n)
        inv = ROPE_THETA ** (-np.arange(0, half, 2, dtype=np.float64) / half)
        ang = np.concatenate([(pos // GRID_W)[:, None] * inv, (pos % GRID_W)[:, None] * inv], axis=-1)
    else:
        ang = np.zeros((n, half))
    cos, sin = np.cos(ang), np.sin(ang)
    return jnp.asarray(np.concatenate([cos, cos, -sin, sin], axis=-1), F32)


def _channel_dft(bw):
    gw = bw // FOURIER_GROUPS
    a = np.arange(gw)
    ang = 2.0 * np.pi * ((a[:, None] * a[None, :]) % gw) / gw
    return jnp.asarray(np.cos(ang), F32).astype(BF16), jnp.asarray(np.sin(ang), F32).astype(BF16)


def kernel(x, c, ctx, c_ctx, w_mod, b_mod, pre_g, post_g, w_in, q_norm_g, kv_norm_g, w_uq, w_ukv,
           conv_w, conv_b, sgu_ln_g, sgu_ln_b, sgu_w, sgu_b, w_branch, w_out):
    B, n, D = x.shape
    nc = ctx.shape[1]
    depth = w_in.shape[0]
    bw = w_branch.shape[2]
    assert w_in.shape[2] == ROPE_OFF + QK_ROPE + (HEAD_W - CQ_OFF) + N_BRANCH_COLS * bw + N_BRANCH * D
    assert kv_norm_g.shape[1] == ROPE_OFF and q_norm_g.shape[1] == HEAD_W - CQ_OFF and B <= 4

    cc = jnp.zeros((8, D), F32).at[:B].set(c).at[B].set(c_ctx)
    mod = _modulation(cc, w_mod, b_mod).reshape(depth, 8, 3, D)

    cs_x, cs_c = _rope_table(n, True), _rope_table(nc, False)
    ccb, scb = _channel_dft(bw)
    x2 = x.reshape(B * n, D)
    c2 = ctx.reshape(B * nc, D)
    row_x = lambda r: r // n
    row_c = lambda r: B

    for l in range(depth):
        last = l == depth - 1
        w_head, w_rest = _relayout_w_in(w_in[l])
        w_q2 = _relayout_w_uq(w_uq[l])
        w_ukv_l = w_ukv[l].astype(BF16)
        consts = dict(
            ccb=ccb, scb=scb, conv_w=conv_w[l], conv_b=conv_b[l].reshape(1, bw),
            ln_g=sgu_ln_g[l].reshape(1, bw), ln_b=sgu_ln_b[l].reshape(1, bw),
            w_s=sgu_w[l].astype(BF16),
            b_s=jnp.broadcast_to(sgu_b[l][:, :, None], sgu_b.shape[1:] + (bw // sgu_w.shape[1],)),
            w_branch=w_branch[l].astype(BF16), w_out=w_out[l].astype(BF16), post_g=post_g[l].reshape(1, D))

        hx, px = _in_projection(x2, mod[l], row_x, pre_g[l], w_head, w_rest, min(1024, n))
        hc, *pc = _in_projection(c2, mod[l], row_c, pre_g[l], w_head, None if last else w_rest,
                                 min(1024, B * nc))
        q_c, k_c, v_c = _mla_prep(hc, cs_c, kv_norm_g[l], q_norm_g[l], w_ukv_l, w_q2, B, nc)
        q_x, k_x, v_x = _mla_prep(hx, cs_x, kv_norm_g[l], q_norm_g[l], w_ukv_l, w_q2, B, n)

        attn_x = _attention(q_x, k_c, v_c, k_x, v_x).reshape(B * n, -1)
        four_x = _position_dft(px, 1, bw, B, n)
        new_x = _mix(px, attn_x, four_x, x2, mod[l], row_x, consts, n, n)
        if not last:
            attn_c = _attention(q_c, k_c, v_c).reshape(B * nc, -1)
            four_c = _position_dft(pc[0], 1, bw, B, nc)
            c2 = _mix(pc[0], attn_c, four_c, c2, mod[l], row_c, consts, nc, B * nc)
        x2 = new_x
    return x2.reshape(B, n, D)
```

```python
import functools

import jax
import jax.numpy as jnp
import numpy as np
from jax import lax
from jax.experimental import pallas as pl
from jax.experimental.pallas import tpu as pltpu

F32 = jnp.float32
BF16 = jnp.bfloat16

MLA_HEADS = 4
QK_NOPE = 128
QK_ROPE = 64
V_HEAD = 128
GRID_W = 64
ROPE_THETA = 10000.0
FOURIER_GROUPS = 4
CHUNK = 128
N_BRANCH = 4
N_BRANCH_COLS = 10
EPS = 1e-6

ROPE_OFF = 256
CQ_OFF = 512
HEAD_W = 1024
QK_W = QK_NOPE + 2 * QK_ROPE

FFT_N1 = 16

V7X_VMEM_BYTES = 64 * 2**20
VMEM_CAP_BYTES = V7X_VMEM_BYTES - 8 * 2**20


def _nbytes(shape, dtype):
    return int(np.prod(shape)) * jnp.dtype(dtype).itemsize


def _compiler_params(semantics, block_bytes, scratch_bytes=0, temp_bytes=0):
    need = 2 * block_bytes + scratch_bytes + temp_bytes + 2 * 2**20
    return pltpu.CompilerParams(dimension_semantics=semantics,
                                vmem_limit_bytes=int(min(max(need, 16 * 2**20), VMEM_CAP_BYTES)))


def _sigmoid(v):
    return 0.5 * jnp.tanh(0.5 * v) + 0.5


def _silu(v):
    return v * _sigmoid(v)


def _rms(v, g):
    return v * lax.rsqrt(jnp.mean(v * v, axis=-1, keepdims=True) + EPS) * g


def _dot(a, b):
    return jnp.dot(a, b, preferred_element_type=F32)


def _mod_kernel(c_ref, w_ref, b_ref, o_ref):
    @pl.when(pl.program_id(1) == 0)
    def _():
        o_ref[0] = jnp.broadcast_to(b_ref[0], o_ref.shape[1:])

    s = _silu(c_ref[...]).astype(BF16)
    o_ref[0] += _dot(s, w_ref[0].astype(BF16))


def _modulation(cc, w_mod, b_mod):
    L, D, W = w_mod.shape
    tk = 256 if D % 256 == 0 else D
    blocks = _nbytes((tk, W), F32) + _nbytes((8, tk), F32) + 2 * _nbytes((8, W), F32)
    return pl.pallas_call(
        _mod_kernel,
        out_shape=jax.ShapeDtypeStruct((L, 8, W), F32),
        grid=(L, D // tk),
        in_specs=[pl.BlockSpec((8, tk), lambda l, k: (0, k)),
                  pl.BlockSpec((1, tk, W), lambda l, k: (l, k, 0)),
                  pl.BlockSpec((1, 1, W), lambda l, k: (l, 0, 0))],
        out_specs=pl.BlockSpec((1, 8, W), lambda l, k: (l, 0, 0)),
        compiler_params=_compiler_params(("parallel", "arbitrary"), blocks, 0, _nbytes((tk, W), BF16)),
        name="modulation",
    )(cc, w_mod, b_mod.reshape(L, 1, W))


def _inproj_kernel(*refs, with_rest):
    if with_rest:
        x_ref, mod_ref, g_ref, wh_ref, wr_ref, oh_ref, or_ref, h_ref = refs
    else:
        x_ref, mod_ref, g_ref, wh_ref, oh_ref, h_ref = refs
    j = pl.program_id(1)

    @pl.when(j == 0)
    def _():
        y = _rms(x_ref[...], g_ref[...])
        h_ref[...] = (y * (1.0 + mod_ref[0, 1:2, :]) + mod_ref[0, 0:1, :]).astype(BF16)
        oh_ref[...] = _dot(h_ref[...], wh_ref[...]).astype(oh_ref.dtype)

    if with_rest:
        @pl.when(j > 0)
        def _():
            or_ref[...] = _dot(h_ref[...], wr_ref[...]).astype(or_ref.dtype)


def _in_projection(x2, mod3, mod_row, pre_g, w_head, w_rest, tm):
    R, D = x2.shape
    tn = w_head.shape[1]
    with_rest = w_rest is not None
    nj = 1 + (w_rest.shape[1] // tn if with_rest else 0)
    rest_col = lambda j: jnp.maximum(j - 1, 0)
    in_specs = [pl.BlockSpec((tm, D), lambda i, j: (i, 0)),
                pl.BlockSpec((1, 3, D), lambda i, j: (mod_row(i * tm), 0, 0)),
                pl.BlockSpec((1, D), lambda i, j: (0, 0)),
                pl.BlockSpec((D, tn), lambda i, j: (0, 0))]
    out_specs = [pl.BlockSpec((tm, tn), lambda i, j: (i, 0))]
    out_shape = [jax.ShapeDtypeStruct((R, tn), BF16)]
    args = [x2, mod3, pre_g.reshape(1, D), w_head]
    if with_rest:
        in_specs.append(pl.BlockSpec((D, tn), lambda i, j: (0, rest_col(j))))
        out_specs.append(pl.BlockSpec((tm, tn), lambda i, j: (i, rest_col(j))))
        out_shape.append(jax.ShapeDtypeStruct((R, w_rest.shape[1]), BF16))
        args.append(w_rest)
    blocks = (_nbytes((tm, D), F32) + _nbytes((3, D), F32) + 2 * _nbytes((D, tn), BF16)
              + 2 * _nbytes((tm, tn), BF16))
    return pl.pallas_call(
        functools.partial(_inproj_kernel, with_rest=with_rest),
        out_shape=out_shape,
        grid=(R // tm, nj),
        in_specs=in_specs,
        out_specs=out_specs,
        scratch_shapes=[pltpu.VMEM((tm, D), BF16)],
        compiler_params=_compiler_params(("parallel", "arbitrary"), blocks, _nbytes((tm, D), BF16),
                                         2 * _nbytes((tm, D), F32) + _nbytes((tm, tn), F32)),
        name="in_projection",
    )(*args)


def _rope_pair(group, cs):
    r = group * cs
    return r + pltpu.roll(r, QK_ROPE, axis=1)


def _prep_kernel(p_ref, cs_ref, kvg_ref, qg_ref, wkv_ref, wq_ref, q_ref, k_ref, v_ref, *, scale):
    cs = cs_ref[...]
    kvn = _rms(p_ref[:, 0:ROPE_OFF].astype(F32), kvg_ref[...]).astype(BF16)
    kv = _dot(kvn, wkv_ref[...])
    kr = _rope_pair(p_ref[:, ROPE_OFF:ROPE_OFF + 2 * QK_ROPE].astype(F32), cs)
    lane = lax.broadcasted_iota(jnp.int32, kr.shape, 1)
    kr = jnp.where(lane < QK_ROPE, kr, 0.0).astype(BF16)
    cqn = _rms(p_ref[:, CQ_OFF:HEAD_W].astype(F32), qg_ref[...]).astype(BF16)
    q = _dot(cqn, wq_ref[...]) * scale
    kvw = QK_NOPE + V_HEAD
    for h in range(MLA_HEADS):
        k_ref[0, h, :, 0:QK_NOPE] = kv[:, h * kvw:h * kvw + QK_NOPE].astype(BF16)
        k_ref[0, h, :, QK_NOPE:QK_W] = kr
        v_ref[0, h, :, 0:V_HEAD] = kv[:, h * kvw + QK_NOPE:(h + 1) * kvw].astype(BF16)
        v_ref[0, h, :, V_HEAD:2 * V_HEAD] = jnp.ones((kv.shape[0], V_HEAD), BF16)
        q_ref[0, h, :, 0:QK_NOPE] = q[:, h * QK_W:h * QK_W + QK_NOPE].astype(BF16)
        q_ref[0, h, :, QK_NOPE:QK_W] = _rope_pair(q[:, h * QK_W + QK_NOPE:(h + 1) * QK_W], cs).astype(BF16)


def _mla_prep(p, cs, kv_norm_g, q_norm_g, w_ukv, w_q2, B, n):
    tm = min(512, n)
    nt = n // tm
    H = MLA_HEADS
    kv_lora, q_lora = w_ukv.shape[0], w_q2.shape[0]
    blocks = (_nbytes((tm, HEAD_W), BF16) + _nbytes((tm, 128), F32) + _nbytes(w_ukv.shape, BF16)
              + _nbytes(w_q2.shape, BF16) + 2 * _nbytes((H, tm, QK_W), BF16) + _nbytes((H, tm, 2 * V_HEAD), BF16))
    qk_spec = pl.BlockSpec((1, H, tm, QK_W), lambda b, i: (b, 0, i, 0))
    return pl.pallas_call(
        functools.partial(_prep_kernel, scale=float(QK_NOPE + QK_ROPE) ** -0.5),
        out_shape=(jax.ShapeDtypeStruct((B, H, n, QK_W), BF16),
                   jax.ShapeDtypeStruct((B, H, n, QK_W), BF16),
                   jax.ShapeDtypeStruct((B, H, n, 2 * V_HEAD), BF16)),
        grid=(B, nt),
        in_specs=[pl.BlockSpec((tm, HEAD_W), lambda b, i: (b * nt + i, 0)),
                  pl.BlockSpec((tm, 128), lambda b, i: (i, 0)),
                  pl.BlockSpec((1, kv_lora), lambda b, i: (0, 0)),
                  pl.BlockSpec((1, q_lora), lambda b, i: (0, 0)),
                  pl.BlockSpec(w_ukv.shape, lambda b, i: (0, 0)),
                  pl.BlockSpec(w_q2.shape, lambda b, i: (0, 0))],
        out_specs=(qk_spec, qk_spec, pl.BlockSpec((1, H, tm, 2 * V_HEAD), lambda b, i: (b, 0, i, 0))),
        compiler_params=_compiler_params(("parallel", "parallel"), blocks, 0, 6 * _nbytes((tm, HEAD_W), F32)),
        name="mla_prep",
    )(p, cs, kv_norm_g.reshape(1, -1), q_norm_g.reshape(1, -1), w_ukv, w_q2)


_NT = (((1,), (1,)), ((), ()))


def _attn_kernel(*refs, with_x, tk):
    if with_x:
        q_ref, kc_ref, vc_ref, kx_ref, vx_ref, o_ref = refs
    else:
        q_ref, kc_ref, vc_ref, o_ref = refs
    q = q_ref[0, 0]

    s = lax.dot_general(q, kc_ref[0, 0], _NT, preferred_element_type=F32)
    m = jnp.max(s, axis=-1, keepdims=True)
    acc = _dot(jnp.exp(s - m).astype(BF16), vc_ref[0, 0])
    if with_x:
        for c in range(kx_ref.shape[2] // tk):
            s = lax.dot_general(q, kx_ref[0, 0, c * tk:(c + 1) * tk, :], _NT, preferred_element_type=F32)
            m_new = jnp.maximum(m, jnp.max(s, axis=-1, keepdims=True))
            p = jnp.exp(s - m_new).astype(BF16)
            acc = acc * jnp.exp(m - m_new) + _dot(p, vx_ref[0, 0, c * tk:(c + 1) * tk, :])
            m = m_new
    o_ref[0] = (acc[:, :V_HEAD] / acc[:, V_HEAD:]).astype(o_ref.dtype)


def _attention(q, kc, vc, kx=None, vx=None):
    B, H, nq, _ = q.shape
    nc = kc.shape[2]
    with_x = kx is not None
    nk = nc + (kx.shape[2] if with_x else 0)
    tq = min(1024, nq)
    tk = min(512, nq)
    q_spec = pl.BlockSpec((1, 1, tq, QK_W), lambda b, h, i: (b, h, i, 0))

    def full(arr):
        return pl.BlockSpec((1, 1) + arr.shape[2:], lambda b, h, i: (b, h, 0, 0))

    args = [q, kc, vc] + ([kx, vx] if with_x else [])
    blocks = _nbytes((tq, QK_W), BF16) + _nbytes((nk, QK_W + 2 * V_HEAD), BF16) + _nbytes((tq, V_HEAD), BF16)
    return pl.pallas_call(
        functools.partial(_attn_kernel, with_x=with_x, tk=tk),
        out_shape=jax.ShapeDtypeStruct((B, nq, H * V_HEAD), BF16),
        grid=(B, H, nq // tq),
        in_specs=[q_spec] + [full(a) for a in args[1:]],
        out_specs=pl.BlockSpec((1, tq, V_HEAD), lambda b, h, i: (b, i, h)),
        compiler_params=_compiler_params(("parallel", "parallel", "parallel"), blocks, 0,
                                         6 * _nbytes((tq, tk), F32)),
        name="attention",
    )(*args)


def _fft1_kernel(u_ref, m_ref, tc_ref, ts_ref, o_ref, *, reps):
    rows = FFT_N1 * FFT_N1
    u = u_ref[0, :, 0].reshape(rows, u_ref.shape[-1])
    g = _dot(m_ref[...], u)
    gr, gi = g[:rows], g[rows:]
    tc = jnp.concatenate([tc_ref[:, 0].reshape(rows, 128)] * reps, axis=1)
    ts = jnp.concatenate([ts_ref[:, 0].reshape(rows, 128)] * reps, axis=1)
    shape3 = (FFT_N1, FFT_N1, u_ref.shape[-1])
    o_ref[0, 0, :, 0] = (gr * tc + gi * ts).reshape(shape3).astype(o_ref.dtype)
    o_ref[0, 1, :, 0] = (gi * tc - gr * ts).reshape(shape3).astype(o_ref.dtype)


def _fft2_kernel(g_ref, fa_ref, fb_ref, o_ref, *, n2):
    y = _dot(fa_ref[...], g_ref[0, 0, 0]) + _dot(fb_ref[...], g_ref[0, 1, 0])
    o_ref[0, 0, 0] = y[:n2].astype(o_ref.dtype)
    o_ref[0, 1, 0] = y[n2:].astype(o_ref.dtype)


def _dft_tables(n):
    n1 = FFT_N1
    n2 = n // n1
    a = np.arange(n1)
    ang1 = 2.0 * np.pi * ((a[:, None] * a[None, :]) % n1) / n1
    eye = np.eye(n1)
    m1 = np.concatenate([np.kron(np.cos(ang1), eye), -np.kron(np.sin(ang1), eye)], axis=0)
    b = np.arange(n2)
    angt = 2.0 * np.pi * (a[:, None] * b[None, :]) / n
    shape4 = (n1, n2 // n1, n1, 128)
    tc = np.broadcast_to(np.cos(angt)[:, :, None], (n1, n2, 128)).reshape(shape4)
    ts = np.broadcast_to(np.sin(angt)[:, :, None], (n1, n2, 128)).reshape(shape4)
    ang2 = 2.0 * np.pi * ((b[:, None] * b[None, :]) % n2) / n2
    c2, s2 = np.cos(ang2), np.sin(ang2)
    fa = np.concatenate([c2, -s2], axis=0)
    fb = np.concatenate([s2, c2], axis=0)
    to_bf16 = lambda t: jnp.asarray(t, F32).astype(BF16)
    return to_bf16(m1), jnp.asarray(tc, F32), jnp.asarray(ts, F32), to_bf16(fa), to_bf16(fb)


def _position_dft(p, col_block, C, B, n):
    n1 = FFT_N1
    n2 = n // n1
    nb = n2 // n1
    m1, tc, ts, fa, fb = _dft_tables(n)
    p5 = p.reshape(B, n1, nb, n1, p.shape[1])
    blocks = (3 * _nbytes((n1 * n1, C), BF16) + _nbytes(m1.shape, BF16) + 2 * _nbytes((n1 * n1, 128), F32))
    g = pl.pallas_call(
        functools.partial(_fft1_kernel, reps=C // 128),
        out_shape=jax.ShapeDtypeStruct((B, 2, n1, nb, n1, C), BF16),
        grid=(B, nb),
        in_specs=[pl.BlockSpec((1, n1, 1, n1, C), lambda b, r: (b, 0, r, 0, col_block)),
                  pl.BlockSpec(m1.shape, lambda b, r: (0, 0)),
                  pl.BlockSpec((n1, 1, n1, 128), lambda b, r: (0, r, 0, 0)),
                  pl.BlockSpec((n1, 1, n1, 128), lambda b, r: (0, r, 0, 0))],
        out_specs=pl.BlockSpec((1, 2, n1, 1, n1, C), lambda b, r: (b, 0, 0, r, 0, 0)),
        compiler_params=_compiler_params(("parallel", "parallel"), blocks, 0, 8 * _nbytes((n1 * n1, C), F32)),
        name="position_dft_stage1",
    )(p5, m1, tc, ts)
    g = g.reshape(B, 2, n1, n2, C)
    blocks = 4 * _nbytes((n2, C), BF16) + 2 * _nbytes(fa.shape, BF16)
    return pl.pallas_call(
        functools.partial(_fft2_kernel, n2=n2),
        out_shape=jax.ShapeDtypeStruct((B, 2, n1, n2, C), BF16),
        grid=(B, n1),
        in_specs=[pl.BlockSpec((1, 2, 1, n2, C), lambda b, d: (b, 0, d, 0, 0)),
                  pl.BlockSpec(fa.shape, lambda b, d: (0, 0)),
                  pl.BlockSpec(fb.shape, lambda b, d: (0, 0))],
        out_specs=pl.BlockSpec((1, 2, 1, n2, C), lambda b, d: (b, 0, d, 0, 0)),
        compiler_params=_compiler_params(("parallel", "parallel"), blocks, 0, 4 * _nbytes((n2, C), F32)),
        name="position_dft_stage2",
    )(g, fa, fb)


def _branch_kernel(attn_ref, vr_ref, vi_ref, zA_ref, zB_ref, xC_ref, bC_ref, cC_ref, zC_ref, uD_ref, vD_ref,
                   zD_ref, xCp_ref, cCp_ref, xCn_ref, cCn_ref, ccb_ref, scb_ref, convw_ref, convb_ref, lng_ref,
                   lnb_ref, ws_ref, bsb_ref, ys_ref, fr_ref, fi_ref, *, n, tm, halo, fscale):
    i = pl.program_id(0)

    def gate(ref):
        return _silu(ref[...].astype(F32))

    ys_ref[0] = (attn_ref[...].astype(F32) * gate(zA_ref)).astype(BF16)
    gwf = ccb_ref.shape[0]
    for d in range(FFT_N1):
        rows = pl.ds(d, tm // FFT_N1, stride=FFT_N1)
        vr, vi = vr_ref[0, 0, d].astype(F32), vi_ref[0, 0, d].astype(F32)
        for g in range(fr_ref.shape[0]):
            fr_ref[g, rows, :] = vr[:, g * gwf:(g + 1) * gwf]
            fi_ref[g, rows, :] = vi[:, g * gwf:(g + 1) * gwf]
    four = jnp.concatenate(
        [_dot(fr_ref[g].astype(BF16), ccb_ref[...]) + _dot(fi_ref[g].astype(BF16), scb_ref[...])
         for g in range(fr_ref.shape[0])], axis=1)
    ys_ref[1] = (four * fscale * gate(zB_ref)).astype(BF16)
    u = cC_ref[...].astype(F32) * xC_ref[...].astype(F32)
    first = (i * tm) % n == 0
    last = ((i + 1) * tm) % n == 0
    up = (cCp_ref[...].astype(F32) * xCp_ref[...].astype(F32))[halo - 1:halo, :]
    un = (cCn_ref[...].astype(F32) * xCn_ref[...].astype(F32))[0:1, :]
    up = up * jnp.where(first, 0.0, 1.0)
    un = un * jnp.where(last, 0.0, 1.0)
    row = lax.broadcasted_iota(jnp.int32, u.shape, 0)
    prev = jnp.where(row == 0, up, pltpu.roll(u, 1, axis=0))
    nxt = jnp.where(row == tm - 1, un, pltpu.roll(u, tm - 1, axis=0))
    conv = prev * convw_ref[0:1, :] + u * convw_ref[1:2, :] + nxt * convw_ref[2:3, :] + convb_ref[...]
    ys_ref[2] = (bC_ref[...].astype(F32) * conv * gate(zC_ref)).astype(BF16)
    v = vD_ref[...].astype(F32)
    vc = v - jnp.mean(v, axis=-1, keepdims=True)
    vn = vc * lax.rsqrt(jnp.mean(vc * vc, axis=-1, keepdims=True) + EPS) * lng_ref[...] + lnb_ref[...]
    vn = vn.astype(BF16)
    ug = uD_ref[...].astype(F32) * gate(zD_ref)
    gw = vn.shape[1] // ws_ref.shape[0]
    for g in range(ws_ref.shape[0]):
        for c in range(tm // CHUNK):
            rs, cs = slice(c * CHUNK, (c + 1) * CHUNK), slice(g * gw, (g + 1) * gw)
            mixed = _dot(ws_ref[g], vn[rs, cs]) + bsb_ref[g]
            ys_ref[3, rs, cs] = (ug[rs, cs] * mixed).astype(BF16)


def _mix_kernel(ys_ref, g0_ref, g1_ref, g2_ref, g3_ref, wb_ref, wo_ref, x_ref, mod_ref, postg_ref, o_ref,
                merged_ref, acc_ref, ss_ref, ssprev_ref, *, nj, ne, ni):
    i = pl.program_id(0)
    j = pl.program_id(1)

    @pl.when((j < nj) & (i < ni))
    def _():
        merged = None
        for g, g_ref in enumerate((g0_ref, g1_ref, g2_ref, g3_ref)):
            term = _sigmoid(g_ref[...].astype(F32)) * _dot(ys_ref[g], wb_ref[g])
            merged = term if merged is None else merged + term
        merged_ref[j] = merged.astype(BF16)

    @pl.when((j == nj) & (i > 0))
    def _():
        ssprev_ref[...] = ss_ref[...]

    @pl.when((j >= nj) & (i > 0))
    def _():
        width = acc_ref.shape[0] * acc_ref.shape[2]
        y = acc_ref[j - nj] * lax.rsqrt(ssprev_ref[:, 0:1] / width + EPS) * postg_ref[...]
        o_ref[...] = x_ref[...] + mod_ref[0, 2:3, :] * y

    @pl.when((j >= nj) & (i < ni))
    def _():
        merged = jnp.concatenate([merged_ref[c] for c in range(nj)], axis=1)
        part = _dot(merged, wo_ref[...])
        acc_ref[j - nj] = part
        ss = jnp.broadcast_to(jnp.sum(part * part, axis=-1, keepdims=True), ss_ref.shape)

        @pl.when(j == nj)
        def _():
            ss_ref[...] = ss

        @pl.when(j > nj)
        def _():
            ss_ref[...] += ss


def _mix(p, attn, v, x2, mod3, mod_row, consts, n, tile_rows):
    R, D = x2.shape
    bw = attn.shape[1]

    tb = min(512, n)
    halo = 16
    nt = n // tb
    cl = tb // FFT_N1

    def seg(k):
        return pl.BlockSpec((tb, bw), lambda i: (i, k))

    def halo_prev(k):
        return pl.BlockSpec((halo, bw), lambda i: (jnp.maximum(i * (tb // halo) - 1, 0), k))

    def halo_next(k):
        return pl.BlockSpec((halo, bw), lambda i: (jnp.minimum((i + 1) * (tb // halo), R // halo - 1), k))

    def const(arr):
        nd = arr.ndim
        return pl.BlockSpec(arr.shape, lambda i: (0,) * nd)

    branch_consts = [consts[k] for k in ("ccb", "scb", "conv_w", "conv_b", "ln_g", "ln_b", "w_s", "b_s")]
    gwf = bw // FOURIER_GROUPS
    blocks = (12 * _nbytes((tb, bw), BF16) + 4 * _nbytes((halo, bw), BF16) + N_BRANCH * _nbytes((tb, bw), BF16)
              + sum(_nbytes(a.shape, a.dtype) for a in branch_consts))
    ys = pl.pallas_call(
        functools.partial(_branch_kernel, n=n, tm=tb, halo=halo, fscale=float(n * gwf) ** -0.5),
        out_shape=jax.ShapeDtypeStruct((N_BRANCH, R, bw), BF16),
        grid=(R // tb,),
        in_specs=([pl.BlockSpec((tb, bw), lambda i: (i, 0)),
                   pl.BlockSpec((1, 1, FFT_N1, cl, bw), lambda i: (i // nt, 0, 0, i % nt, 0)),
                   pl.BlockSpec((1, 1, FFT_N1, cl, bw), lambda i: (i // nt, 1, 0, i % nt, 0))]
                  + [seg(k) for k in (0, 2, 3, 4, 5, 6, 7, 8, 9)]
                  + [halo_prev(3), halo_prev(5), halo_next(3), halo_next(5)]
                  + [const(a) for a in branch_consts]),
        out_specs=pl.BlockSpec((N_BRANCH, tb, bw), lambda i: (0, i, 0)),
        scratch_shapes=[pltpu.VMEM((FOURIER_GROUPS, tb, gwf), F32), pltpu.VMEM((FOURIER_GROUPS, tb, gwf), F32)],
        compiler_params=_compiler_params(("parallel",), blocks, 2 * _nbytes((tb, bw), F32),
                                         12 * _nbytes((tb, bw), F32)),
        name="branch_outputs",
    )(attn, v, v, *([p] * 9), *([p] * 4), *branch_consts)

    tm = min(1024, tile_rows)
    tn = min(512, D)
    te = min(512, D)
    nj, ne = D // tn, D // te
    ni = R // tm
    gate0 = N_BRANCH_COLS * bw // tn
    ia = lambda i, j: jnp.minimum(jnp.where(j < nj, i, i + 1), ni - 1)
    ja = lambda j: jnp.where(j < nj, j, 0)
    jb = lambda j: jnp.maximum(j - nj, 0)
    ip = lambda i: jnp.maximum(i - 1, 0)
    jo = lambda i, j: jnp.where(i == 0, 0, jb(j))

    def gates(g):
        return pl.BlockSpec((tm, tn), lambda i, j: (ia(i, j), gate0 + g * nj + ja(j)))

    blocks = (_nbytes((N_BRANCH, tm, bw), BF16) + N_BRANCH * _nbytes((tm, tn), BF16)
              + _nbytes((N_BRANCH, bw, tn), BF16) + _nbytes((D, te), BF16) + 2 * _nbytes((tm, te), F32))
    scratch = _nbytes((tm, D), BF16) + _nbytes((tm, D), F32) + 2 * _nbytes((tm, 128), F32)
    return pl.pallas_call(
        functools.partial(_mix_kernel, nj=nj, ne=ne, ni=ni),
        out_shape=jax.ShapeDtypeStruct((R, D), F32),
        grid=(ni + 1, nj + ne),
        in_specs=([pl.BlockSpec((N_BRANCH, tm, bw), lambda i, j: (0, ia(i, j), 0))]
                  + [gates(g) for g in range(N_BRANCH)]
                  + [pl.BlockSpec((N_BRANCH, bw, tn), lambda i, j: (0, 0, ja(j))),
                     pl.BlockSpec((D, te), lambda i, j: (0, jb(j))),
                     pl.BlockSpec((tm, te), lambda i, j: (ip(i), jb(j))),
                     pl.BlockSpec((1, 3, te), lambda i, j: (mod_row(ip(i) * tm), 0, jb(j))),
                     pl.BlockSpec((1, te), lambda i, j: (0, jb(j)))]),
        out_specs=pl.BlockSpec((tm, te), lambda i, j: (ip(i), jo(i, j))),
        scratch_shapes=[pltpu.VMEM((nj, tm, tn), BF16), pltpu.VMEM((ne, tm, te), F32),
                        pltpu.VMEM((tm, 128), F32), pltpu.VMEM((tm, 128), F32)],
        compiler_params=_compiler_params(("arbitrary", "arbitrary"), blocks, scratch,
                                         _nbytes((tm, D), BF16) + 6 * _nbytes((tm, tn), F32)),
        name="branch_mix",
    )(ys, p, p, p, p, consts["w_branch"], consts["w_out"], x2, mod3, consts["post_g"])


def _swap_halves(w):
    h = w.shape[-1] // 2
    return jnp.concatenate([w[..., h:], w[..., :h]], axis=-1)


def _relayout_w_in(w):
    rope_end = ROPE_OFF + QK_ROPE
    head_end = rope_end + HEAD_W - CQ_OFF
    rope = w[:, ROPE_OFF:rope_end]
    pad = jnp.zeros((w.shape[0], CQ_OFF - ROPE_OFF - 2 * QK_ROPE), w.dtype)
    head = jnp.concatenate([w[:, :ROPE_OFF], rope, _swap_halves(rope), pad, w[:, rope_end:head_end]], axis=1)
    return head.astype(BF16), w[:, head_end:].astype(BF16)


def _relayout_w_uq(w):
    w = w.reshape(w.shape[0], MLA_HEADS, QK_NOPE + QK_ROPE)
    rope = w[..., QK_NOPE:]
    return jnp.concatenate([w[..., :QK_NOPE], rope, _swap_halves(rope)], axis=-1).reshape(w.shape[0], -1).astype(BF16)


def _rope_table(n, rotate):
    half = QK_ROPE // 2
    if rotate:
        pos = np.arange(n)
        inv = ROPE_THETA ** (-np.arange(0, half, 2, dtype=np.float64) / half)
        ang = np.concatenate([(pos // GRID_W)[:, None] * inv, (pos % GRID_W)[:, None] * inv], axis=-1)
    else:
        ang = np.zeros((n, half))
    cos, sin = np.cos(ang), np.sin(ang)
    return jnp.asarray(np.concatenate([cos, cos, -sin, sin], axis=-1), F32)


def _channel_dft(bw):
    gw = bw // FOURIER_GROUPS
    a = np.arange(gw)
    ang = 2.0 * np.pi * ((a[:, None] * a[None, :]) % gw) / gw
    return jnp.asarray(np.cos(ang), F32).astype(BF16), jnp.asarray(np.sin(ang), F32).astype(BF16)


def kernel(x, c, ctx, c_ctx, w_mod, b_mod, pre_g, post_g, w_in, q_norm_g, kv_norm_g, w_uq, w_ukv,
           conv_w, conv_b, sgu_ln_g, sgu_ln_b, sgu_w, sgu_b, w_branch, w_out):
    B, n, D = x.shape
    nc = ctx.shape[1]
    depth = w_in.shape[0]
    bw = w_branch.shape[2]
    assert w_in.shape[2] == ROPE_OFF + QK_ROPE + (HEAD_W - CQ_OFF) + N_BRANCH_COLS * bw + N_BRANCH * D
    assert kv_norm_g.shape[1] == ROPE_OFF and q_norm_g.shape[1] == HEAD_W - CQ_OFF and B <= 4

    cc = jnp.zeros((8, D), F32).at[:B].set(c).at[B].set(c_ctx)
    mod = _modulation(cc, w_mod, b_mod).reshape(depth, 8, 3, D)

    cs_x, cs_c = _rope_table(n, True), _rope_table(nc, False)
    ccb, scb = _channel_dft(bw)
    x2 = x.reshape(B * n, D)
    c2 = ctx.reshape(B * nc, D)
    row_x = lambda r: r // n
    row_c = lambda r: B

    for l in range(depth):
        last = l == depth - 1
        w_head, w_rest = _relayout_w_in(w_in[l])
        w_q2 = _relayout_w_uq(w_uq[l])
        w_ukv_l = w_ukv[l].astype(BF16)
        consts = dict(
            ccb=ccb, scb=scb, conv_w=conv_w[l], conv_b=conv_b[l].reshape(1, bw),
            ln_g=sgu_ln_g[l].reshape(1, bw), ln_b=sgu_ln_b[l].reshape(1, bw),
            w_s=sgu_w[l].astype(BF16),
            b_s=jnp.broadcast_to(sgu_b[l][:, :, None], sgu_b.shape[1:] + (bw // sgu_w.shape[1],)),
            w_branch=w_branch[l].astype(BF16), w_out=w_out[l].astype(BF16), post_g=post_g[l].reshape(1, D))

        hx, px = _in_projection(x2, mod[l], row_x, pre_g[l], w_head, w_rest, min(1024, n))
        hc, *pc = _in_projection(c2, mod[l], row_c, pre_g[l], w_head, None if last else w_rest,
                                 min(1024, B * nc))
        q_c, k_c, v_c = _mla_prep(hc, cs_c, kv_norm_g[l], q_norm_g[l], w_ukv_l, w_q2, B, nc)
        q_x, k_x, v_x = _mla_prep(hx, cs_x, kv_norm_g[l], q_norm_g[l], w_ukv_l, w_q2, B, n)

        attn_x = _attention(q_x, k_c, v_c, k_x, v_x).reshape(B * n, -1)
        four_x = _position_dft(px, 1, bw, B, n)
        new_x = _mix(px, attn_x, four_x, x2, mod[l], row_x, consts, n, n)
        if not last:
            attn_c = _attention(q_c, k_c, v_c).reshape(B * nc, -1)
            four_c = _position_dft(pc[0], 1, bw, B, nc)
            c2 = _mix(pc[0], attn_c, four_c, c2, mod[l], row_c, consts, nc, B * nc)
        x2 = new_x
    return x2.reshape(B, n, D)
```

```python
import functools

import jax
import jax.numpy as jnp
import numpy as np
from jax import lax
from jax.experimental import pallas as pl
from jax.experimental.pallas import tpu as pltpu

F32 = jnp.float32
BF16 = jnp.bfloat16

MLA_HEADS = 4
QK_NOPE = 128
QK_ROPE = 64
V_HEAD = 128
GRID_W = 64
ROPE_THETA = 10000.0
FOURIER_GROUPS = 4
CHUNK = 128
N_BRANCH = 4
N_BRANCH_COLS = 10
EPS = 1e-6

ROPE_OFF = 256
CQ_OFF = 512
HEAD_W = 1024
QK_W = QK_NOPE + 2 * QK_ROPE

FFT_N1 = 16
FFT_STEP_CHUNKS = 4

V7X_VMEM_BYTES = 64 * 2**20
VMEM_CAP_BYTES = V7X_VMEM_BYTES - 8 * 2**20


def _nbytes(shape, dtype):
    return int(np.prod(shape)) * jnp.dtype(dtype).itemsize


def _compiler_params(semantics, block_bytes, scratch_bytes=0, temp_bytes=0):
    need = 2 * block_bytes + scratch_bytes + temp_bytes + 2 * 2**20
    return pltpu.CompilerParams(dimension_semantics=semantics,
                                vmem_limit_bytes=int(min(max(need, 16 * 2**20), VMEM_CAP_BYTES)))


def _sigmoid(v):
    return 0.5 * jnp.tanh(0.5 * v) + 0.5


def _silu(v):
    return v * _sigmoid(v)


def _rms(v, g):
    return v * lax.rsqrt(jnp.mean(v * v, axis=-1, keepdims=True) + EPS) * g


def _dot(a, b):
    return jnp.dot(a, b, preferred_element_type=F32)


def _mod_kernel(c_ref, w_ref, b_ref, o_ref):
    @pl.when(pl.program_id(1) == 0)
    def _():
        o_ref[0] = jnp.broadcast_to(b_ref[0], o_ref.shape[1:])

    s = _silu(c_ref[...]).astype(BF16)
    o_ref[0] += _dot(s, w_ref[0].astype(BF16))


def _modulation(cc, w_mod, b_mod):
    L, D, W = w_mod.shape
    tk = 256 if D % 256 == 0 else D
    blocks = _nbytes((tk, W), F32) + _nbytes((8, tk), F32) + 2 * _nbytes((8, W), F32)
    return pl.pallas_call(
        _mod_kernel,
        out_shape=jax.ShapeDtypeStruct((L, 8, W), F32),
        grid=(L, D // tk),
        in_specs=[pl.BlockSpec((8, tk), lambda l, k: (0, k)),
                  pl.BlockSpec((1, tk, W), lambda l, k: (l, k, 0)),
                  pl.BlockSpec((1, 1, W), lambda l, k: (l, 0, 0))],
        out_specs=pl.BlockSpec((1, 8, W), lambda l, k: (l, 0, 0)),
        compiler_params=_compiler_params(("parallel", "arbitrary"), blocks, 0, _nbytes((tk, W), BF16)),
        name="modulation",
    )(cc, w_mod, b_mod.reshape(L, 1, W))


def _inproj_kernel(*refs, with_rest):
    if with_rest:
        x_ref, mod_ref, g_ref, wh_ref, wr_ref, oh_ref, or_ref, h_ref = refs
    else:
        x_ref, mod_ref, g_ref, wh_ref, oh_ref, h_ref = refs
    j = pl.program_id(1)

    @pl.when(j == 0)
    def _():
        y = _rms(x_ref[...], g_ref[...])
        h_ref[...] = (y * (1.0 + mod_ref[0, 1:2, :]) + mod_ref[0, 0:1, :]).astype(BF16)
        oh_ref[...] = _dot(h_ref[...], wh_ref[...]).astype(oh_ref.dtype)

    if with_rest:
        @pl.when(j > 0)
        def _():
            or_ref[...] = _dot(h_ref[...], wr_ref[...]).astype(or_ref.dtype)


def _in_projection(x2, mod3, mod_row, pre_g, w_head, w_rest, tm):
    R, D = x2.shape
    tn = w_head.shape[1]
    with_rest = w_rest is not None
    nj = 1 + (w_rest.shape[1] // tn if with_rest else 0)
    rest_col = lambda j: jnp.maximum(j - 1, 0)
    in_specs = [pl.BlockSpec((tm, D), lambda i, j: (i, 0)),
                pl.BlockSpec((1, 3, D), lambda i, j: (mod_row(i * tm), 0, 0)),
                pl.BlockSpec((1, D), lambda i, j: (0, 0)),
                pl.BlockSpec((D, tn), lambda i, j: (0, 0))]
    out_specs = [pl.BlockSpec((tm, tn), lambda i, j: (i, 0))]
    out_shape = [jax.ShapeDtypeStruct((R, tn), BF16)]
    args = [x2, mod3, pre_g.reshape(1, D), w_head]
    if with_rest:
        in_specs.append(pl.BlockSpec((D, tn), lambda i, j: (0, rest_col(j))))
        out_specs.append(pl.BlockSpec((tm, tn), lambda i, j: (i, rest_col(j))))
        out_shape.append(jax.ShapeDtypeStruct((R, w_rest.shape[1]), BF16))
        args.append(w_rest)
    blocks = (_nbytes((tm, D), F32) + _nbytes((3, D), F32) + 2 * _nbytes((D, tn), BF16)
              + 2 * _nbytes((tm, tn), BF16))
    return pl.pallas_call(
        functools.partial(_inproj_kernel, with_rest=with_rest),
        out_shape=out_shape,
        grid=(R // tm, nj),
        in_specs=in_specs,
        out_specs=out_specs,
        scratch_shapes=[pltpu.VMEM((tm, D), BF16)],
        compiler_params=_compiler_params(("parallel", "arbitrary"), blocks, _nbytes((tm, D), BF16),
                                         2 * _nbytes((tm, D), F32) + _nbytes((tm, tn), F32)),
        name="in_projection",
    )(*args)


def _rope_pair(group, cs):
    r = group * cs
    return r + pltpu.roll(r, QK_ROPE, axis=1)


def _prep_kernel(p_ref, cs_ref, kvg_ref, qg_ref, wkv_ref, wq_ref, q_ref, k_ref, v_ref, *, scale):
    cs = cs_ref[...]
    kvn = _rms(p_ref[:, 0:ROPE_OFF].astype(F32), kvg_ref[...]).astype(BF16)
    kv = _dot(kvn, wkv_ref[...])
    kr = _rope_pair(p_ref[:, ROPE_OFF:ROPE_OFF + 2 * QK_ROPE].astype(F32), cs)
    lane = lax.broadcasted_iota(jnp.int32, kr.shape, 1)
    kr = jnp.where(lane < QK_ROPE, kr, 0.0).astype(BF16)
    cqn = _rms(p_ref[:, CQ_OFF:HEAD_W].astype(F32), qg_ref[...]).astype(BF16)
    q = _dot(cqn, wq_ref[...]) * scale
    kvw = QK_NOPE + V_HEAD
    for h in range(MLA_HEADS):
        k_ref[0, h, :, 0:QK_NOPE] = kv[:, h * kvw:h * kvw + QK_NOPE].astype(BF16)
        k_ref[0, h, :, QK_NOPE:QK_W] = kr
        v_ref[0, h, :, 0:V_HEAD] = kv[:, h * kvw + QK_NOPE:(h + 1) * kvw].astype(BF16)
        v_ref[0, h, :, V_HEAD:2 * V_HEAD] = jnp.ones((kv.shape[0], V_HEAD), BF16)
        q_ref[0, h, :, 0:QK_NOPE] = q[:, h * QK_W:h * QK_W + QK_NOPE].astype(BF16)
        q_ref[0, h, :, QK_NOPE:QK_W] = _rope_pair(q[:, h * QK_W + QK_NOPE:(h + 1) * QK_W], cs).astype(BF16)


def _mla_prep(p, cs, kv_norm_g, q_norm_g, w_ukv, w_q2, B, n):
    tm = min(512, n)
    nt = n // tm
    H = MLA_HEADS
    kv_lora, q_lora = w_ukv.shape[0], w_q2.shape[0]
    blocks = (_nbytes((tm, HEAD_W), BF16) + _nbytes((tm, 128), F32) + _nbytes(w_ukv.shape, BF16)
              + _nbytes(w_q2.shape, BF16) + 2 * _nbytes((H, tm, QK_W), BF16) + _nbytes((H, tm, 2 * V_HEAD), BF16))
    qk_spec = pl.BlockSpec((1, H, tm, QK_W), lambda b, i: (b, 0, i, 0))
    return pl.pallas_call(
        functools.partial(_prep_kernel, scale=float(QK_NOPE + QK_ROPE) ** -0.5),
        out_shape=(jax.ShapeDtypeStruct((B, H, n, QK_W), BF16),
                   jax.ShapeDtypeStruct((B, H, n, QK_W), BF16),
                   jax.ShapeDtypeStruct((B, H, n, 2 * V_HEAD), BF16)),
        grid=(B, nt),
        in_specs=[pl.BlockSpec((tm, HEAD_W), lambda b, i: (b * nt + i, 0)),
                  pl.BlockSpec((tm, 128), lambda b, i: (i, 0)),
                  pl.BlockSpec((1, kv_lora), lambda b, i: (0, 0)),
                  pl.BlockSpec((1, q_lora), lambda b, i: (0, 0)),
                  pl.BlockSpec(w_ukv.shape, lambda b, i: (0, 0)),
                  pl.BlockSpec(w_q2.shape, lambda b, i: (0, 0))],
        out_specs=(qk_spec, qk_spec, pl.BlockSpec((1, H, tm, 2 * V_HEAD), lambda b, i: (b, 0, i, 0))),
        compiler_params=_compiler_params(("parallel", "parallel"), blocks, 0, 6 * _nbytes((tm, HEAD_W), F32)),
        name="mla_prep",
    )(p, cs, kv_norm_g.reshape(1, -1), q_norm_g.reshape(1, -1), w_ukv, w_q2)


_NT = (((1,), (1,)), ((), ()))


def _attn_kernel(*refs, with_x, tk):
    if with_x:
        q_ref, kc_ref, vc_ref, kx_ref, vx_ref, o_ref = refs
    else:
        q_ref, kc_ref, vc_ref, o_ref = refs
    q = q_ref[0, 0]

    s = lax.dot_general(q, kc_ref[0, 0], _NT, preferred_element_type=F32)
    m = jnp.max(s, axis=-1, keepdims=True)
    acc = _dot(jnp.exp(s - m).astype(BF16), vc_ref[0, 0])
    if with_x:
        for c in range(kx_ref.shape[2] // tk):
            s = lax.dot_general(q, kx_ref[0, 0, c * tk:(c + 1) * tk, :], _NT, preferred_element_type=F32)
            m_new = jnp.maximum(m, jnp.max(s, axis=-1, keepdims=True))
            p = jnp.exp(s - m_new).astype(BF16)
            acc = acc * jnp.exp(m - m_new) + _dot(p, vx_ref[0, 0, c * tk:(c + 1) * tk, :])
            m = m_new
    o_ref[0] = (acc[:, :V_HEAD] / acc[:, V_HEAD:]).astype(o_ref.dtype)


def _attention(q, kc, vc, kx=None, vx=None):
    B, H, nq, _ = q.shape
    nc = kc.shape[2]
    with_x = kx is not None
    nk = nc + (kx.shape[2] if with_x else 0)
    tq = min(1024, nq)
    tk = min(512, nq)
    q_spec = pl.BlockSpec((1, 1, tq, QK_W), lambda b, h, i: (b, h, i, 0))

    def full(arr):
        return pl.BlockSpec((1, 1) + arr.shape[2:], lambda b, h, i: (b, h, 0, 0))

    args = [q, kc, vc] + ([kx, vx] if with_x else [])
    blocks = _nbytes((tq, QK_W), BF16) + _nbytes((nk, QK_W + 2 * V_HEAD), BF16) + _nbytes((tq, V_HEAD), BF16)
    return pl.pallas_call(
        functools.partial(_attn_kernel, with_x=with_x, tk=tk),
        out_shape=jax.ShapeDtypeStruct((B, nq, H * V_HEAD), BF16),
        grid=(B, H, nq // tq),
        in_specs=[q_spec] + [full(a) for a in args[1:]],
        out_specs=pl.BlockSpec((1, tq, V_HEAD), lambda b, h, i: (b, i, h)),
        compiler_params=_compiler_params(("parallel", "parallel", "parallel"), blocks, 0,
                                         6 * _nbytes((tq, tk), F32)),
        name="attention",
    )(*args)


def _fft1_kernel(u_ref, m_ref, tc_ref, ts_ref, o_ref, *, reps):
    rows = FFT_N1 * FFT_N1
    shape3 = (FFT_N1, FFT_N1, u_ref.shape[-1])
    for r in range(u_ref.shape[2]):
        u = u_ref[0, :, r].reshape(rows, u_ref.shape[-1])
        g = _dot(m_ref[...], u)
        gr, gi = g[:rows], g[rows:]
        tc = jnp.concatenate([tc_ref[:, r].reshape(rows, 128)] * reps, axis=1)
        ts = jnp.concatenate([ts_ref[:, r].reshape(rows, 128)] * reps, axis=1)
        o_ref[0, 0, :, r] = (gr * tc + gi * ts).reshape(shape3).astype(o_ref.dtype)
        o_ref[0, 1, :, r] = (gi * tc - gr * ts).reshape(shape3).astype(o_ref.dtype)


def _fft2_kernel(g_ref, fa_ref, fb_ref, o_ref, *, n2):
    for d in range(g_ref.shape[2]):
        y = _dot(fa_ref[...], g_ref[0, 0, d]) + _dot(fb_ref[...], g_ref[0, 1, d])
        o_ref[0, 0, d] = y[:n2].astype(o_ref.dtype)
        o_ref[0, 1, d] = y[n2:].astype(o_ref.dtype)


def _dft_tables(n):
    n1 = FFT_N1
    n2 = n // n1
    a = np.arange(n1)
    ang1 = 2.0 * np.pi * ((a[:, None] * a[None, :]) % n1) / n1
    eye = np.eye(n1)
    m1 = np.concatenate([np.kron(np.cos(ang1), eye), -np.kron(np.sin(ang1), eye)], axis=0)
    b = np.arange(n2)
    angt = 2.0 * np.pi * (a[:, None] * b[None, :]) / n
    shape4 = (n1, n2 // n1, n1, 128)
    tc = np.broadcast_to(np.cos(angt)[:, :, None], (n1, n2, 128)).reshape(shape4)
    ts = np.broadcast_to(np.sin(angt)[:, :, None], (n1, n2, 128)).reshape(shape4)
    ang2 = 2.0 * np.pi * ((b[:, None] * b[None, :]) % n2) / n2
    c2, s2 = np.cos(ang2), np.sin(ang2)
    fa = np.concatenate([c2, -s2], axis=0)
    fb = np.concatenate([s2, c2], axis=0)
    to_bf16 = lambda t: jnp.asarray(t, F32).astype(BF16)
    return to_bf16(m1), jnp.asarray(tc, F32), jnp.asarray(ts, F32), to_bf16(fa), to_bf16(fb)


def _position_dft(p, col_block, C, B, n):
    n1 = FFT_N1
    n2 = n // n1
    nb = n2 // n1
    m1, tc, ts, fa, fb = _dft_tables(n)
    p5 = p.reshape(B, n1, nb, n1, p.shape[1])
    rb = min(FFT_STEP_CHUNKS, nb)
    blocks = (3 * rb * _nbytes((n1 * n1, C), BF16) + _nbytes(m1.shape, BF16)
              + 2 * rb * _nbytes((n1 * n1, 128), F32))
    g = pl.pallas_call(
        functools.partial(_fft1_kernel, reps=C // 128),
        out_shape=jax.ShapeDtypeStruct((B, 2, n1, nb, n1, C), BF16),
        grid=(B, nb // rb),
        in_specs=[pl.BlockSpec((1, n1, rb, n1, C), lambda b, r: (b, 0, r, 0, col_block)),
                  pl.BlockSpec(m1.shape, lambda b, r: (0, 0)),
                  pl.BlockSpec((n1, rb, n1, 128), lambda b, r: (0, r, 0, 0)),
                  pl.BlockSpec((n1, rb, n1, 128), lambda b, r: (0, r, 0, 0))],
        out_specs=pl.BlockSpec((1, 2, n1, rb, n1, C), lambda b, r: (b, 0, 0, r, 0, 0)),
        compiler_params=_compiler_params(("parallel", "parallel"), blocks, 0, 8 * _nbytes((n1 * n1, C), F32)),
        name="position_dft_stage1",
    )(p5, m1, tc, ts)
    g = g.reshape(B, 2, n1, n2, C)
    db = min(FFT_STEP_CHUNKS, n1)
    blocks = 4 * db * _nbytes((n2, C), BF16) + 2 * _nbytes(fa.shape, BF16)
    return pl.pallas_call(
        functools.partial(_fft2_kernel, n2=n2),
        out_shape=jax.ShapeDtypeStruct((B, 2, n1, n2, C), BF16),
        grid=(B, n1 // db),
        in_specs=[pl.BlockSpec((1, 2, db, n2, C), lambda b, d: (b, 0, d, 0, 0)),
                  pl.BlockSpec(fa.shape, lambda b, d: (0, 0)),
                  pl.BlockSpec(fb.shape, lambda b, d: (0, 0))],
        out_specs=pl.BlockSpec((1, 2, db, n2, C), lambda b, d: (b, 0, d, 0, 0)),
        compiler_params=_compiler_params(("parallel", "parallel"), blocks, 0, 4 * _nbytes((n2, C), F32)),
        name="position_dft_stage2",
    )(g, fa, fb)


def _branch_kernel(attn_ref, vr_ref, vi_ref, zA_ref, zB_ref, xC_ref, bC_ref, cC_ref, zC_ref, uD_ref, vD_ref,
                   zD_ref, xCp_ref, cCp_ref, xCn_ref, cCn_ref, ccb_ref, scb_ref, convw_ref, convb_ref, lng_ref,
                   lnb_ref, ws_ref, bsb_ref, ys_ref, fr_ref, fi_ref, *, n, tm, halo, fscale):
    i = pl.program_id(0)

    def gate(ref):
        return _silu(ref[...].astype(F32))

    ys_ref[0] = (attn_ref[...].astype(F32) * gate(zA_ref)).astype(BF16)
    gwf = ccb_ref.shape[0]
    for d in range(FFT_N1):
        rows = pl.ds(d, tm // FFT_N1, stride=FFT_N1)
        vr, vi = vr_ref[0, 0, d].astype(F32), vi_ref[0, 0, d].astype(F32)
        for g in range(fr_ref.shape[0]):
            fr_ref[g, rows, :] = vr[:, g * gwf:(g + 1) * gwf]
            fi_ref[g, rows, :] = vi[:, g * gwf:(g + 1) * gwf]
    four = jnp.concatenate(
        [_dot(fr_ref[g].astype(BF16), ccb_ref[...]) + _dot(fi_ref[g].astype(BF16), scb_ref[...])
         for g in range(fr_ref.shape[0])], axis=1)
    ys_ref[1] = (four * fscale * gate(zB_ref)).astype(BF16)
    u = cC_ref[...].astype(F32) * xC_ref[...].astype(F32)
    first = (i * tm) % n == 0
    last = ((i + 1) * tm) % n == 0
    up = (cCp_ref[...].astype(F32) * xCp_ref[...].astype(F32))[halo - 1:halo, :]
    un = (cCn_ref[...].astype(F32) * xCn_ref[...].astype(F32))[0:1, :]
    up = up * jnp.where(first, 0.0, 1.0)
    un = un * jnp.where(last, 0.0, 1.0)
    row = lax.broadcasted_iota(jnp.int32, u.shape, 0)
    prev = jnp.where(row == 0, up, pltpu.roll(u, 1, axis=0))
    nxt = jnp.where(row == tm - 1, un, pltpu.roll(u, tm - 1, axis=0))
    conv = prev * convw_ref[0:1, :] + u * convw_ref[1:2, :] + nxt * convw_ref[2:3, :] + convb_ref[...]
    ys_ref[2] = (bC_ref[...].astype(F32) * conv * gate(zC_ref)).astype(BF16)
    v = vD_ref[...].astype(F32)
    vc = v - jnp.mean(v, axis=-1, keepdims=True)
    vn = vc * lax.rsqrt(jnp.mean(vc * vc, axis=-1, keepdims=True) + EPS) * lng_ref[...] + lnb_ref[...]
    vn = vn.astype(BF16)
    ug = uD_ref[...].astype(F32) * gate(zD_ref)
    gw = vn.shape[1] // ws_ref.shape[0]
    for g in range(ws_ref.shape[0]):
        for c in range(tm // CHUNK):
            rs, cs = slice(c * CHUNK, (c + 1) * CHUNK), slice(g * gw, (g + 1) * gw)
            mixed = _dot(ws_ref[g], vn[rs, cs]) + bsb_ref[g]
            ys_ref[3, rs, cs] = (ug[rs, cs] * mixed).astype(BF16)


def _mix_kernel(ys_ref, g0_ref, g1_ref, g2_ref, g3_ref, wb_ref, wo_ref, x_ref, mod_ref, postg_ref, o_ref,
                merged_ref, acc_ref, ss_ref, ssprev_ref, *, nj, ne, ni):
    i = pl.program_id(0)
    j = pl.program_id(1)

    @pl.when((j < nj) & (i < ni))
    def _():
        merged = None
        for g, g_ref in enumerate((g0_ref, g1_ref, g2_ref, g3_ref)):
            term = _sigmoid(g_ref[...].astype(F32)) * _dot(ys_ref[g], wb_ref[g])
            merged = term if merged is None else merged + term
        merged_ref[j] = merged.astype(BF16)

    @pl.when((j == nj) & (i > 0))
    def _():
        ssprev_ref[...] = ss_ref[...]

    @pl.when((j >= nj) & (i > 0))
    def _():
        width = acc_ref.shape[0] * acc_ref.shape[2]
        y = acc_ref[j - nj] * lax.rsqrt(ssprev_ref[:, 0:1] / width + EPS) * postg_ref[...]
        o_ref[...] = x_ref[...] + mod_ref[0, 2:3, :] * y

    @pl.when((j >= nj) & (i < ni))
    def _():
        merged = jnp.concatenate([merged_ref[c] for c in range(nj)], axis=1)
        part = _dot(merged, wo_ref[...])
        acc_ref[j - nj] = part
        ss = jnp.broadcast_to(jnp.sum(part * part, axis=-1, keepdims=True), ss_ref.shape)

        @pl.when(j == nj)
        def _():
            ss_ref[...] = ss

        @pl.when(j > nj)
        def _():
            ss_ref[...] += ss


def _mix(p, attn, v, x2, mod3, mod_row, consts, n, tile_rows):
    R, D = x2.shape
    bw = attn.shape[1]

    tb = min(512, n)
    halo = 16
    nt = n // tb
    cl = tb // FFT_N1

    def seg(k):
        return pl.BlockSpec((tb, bw), lambda i: (i, k))

    def halo_prev(k):
        return pl.BlockSpec((halo, bw), lambda i: (jnp.maximum(i * (tb // halo) - 1, 0), k))

    def halo_next(k):
        return pl.BlockSpec((halo, bw), lambda i: (jnp.minimum((i + 1) * (tb // halo), R // halo - 1), k))

    def const(arr):
        nd = arr.ndim
        return pl.BlockSpec(arr.shape, lambda i: (0,) * nd)

    branch_consts = [consts[k] for k in ("ccb", "scb", "conv_w", "conv_b", "ln_g", "ln_b", "w_s", "b_s")]
    gwf = bw // FOURIER_GROUPS
    blocks = (12 * _nbytes((tb, bw), BF16) + 4 * _nbytes((halo, bw), BF16) + N_BRANCH * _nbytes((tb, bw), BF16)
              + sum(_nbytes(a.shape, a.dtype) for a in branch_consts))
    ys = pl.pallas_call(
        functools.partial(_branch_kernel, n=n, tm=tb, halo=halo, fscale=float(n * gwf) ** -0.5),
        out_shape=jax.ShapeDtypeStruct((N_BRANCH, R, bw), BF16),
        grid=(R // tb,),
        in_specs=([pl.BlockSpec((tb, bw), lambda i: (i, 0)),
                   pl.BlockSpec((1, 1, FFT_N1, cl, bw), lambda i: (i // nt, 0, 0, i % nt, 0)),
                   pl.BlockSpec((1, 1, FFT_N1, cl, bw), lambda i: (i // nt, 1, 0, i % nt, 0))]
                  + [seg(k) for k in (0, 2, 3, 4, 5, 6, 7, 8, 9)]
                  + [halo_prev(3), halo_prev(5), halo_next(3), halo_next(5)]
                  + [const(a) for a in branch_consts]),
        out_specs=pl.BlockSpec((N_BRANCH, tb, bw), lambda i: (0, i, 0)),
        scratch_shapes=[pltpu.VMEM((FOURIER_GROUPS, tb, gwf), F32), pltpu.VMEM((FOURIER_GROUPS, tb, gwf), F32)],
        compiler_params=_compiler_params(("parallel",), blocks, 2 * _nbytes((tb, bw), F32),
                                         12 * _nbytes((tb, bw), F32)),
        name="branch_outputs",
    )(attn, v, v, *([p] * 9), *([p] * 4), *branch_consts)

    tm = min(1024, tile_rows)
    tn = min(512, D)
    te = min(512, D)
    nj, ne = D // tn, D // te
    ni = R // tm
    gate0 = N_BRANCH_COLS * bw // tn
    ia = lambda i, j: jnp.minimum(jnp.where(j < nj, i, i + 1), ni - 1)
    ja = lambda j: jnp.where(j < nj, j, 0)
    jb = lambda j: jnp.maximum(j - nj, 0)
    ip = lambda i: jnp.maximum(i - 1, 0)
    jo = lambda i, j: jnp.where(i == 0, 0, jb(j))

    def gates(g):
        return pl.BlockSpec((tm, tn), lambda i, j: (ia(i, j), gate0 + g * nj + ja(j)))

    blocks = (_nbytes((N_BRANCH, tm, bw), BF16) + N_BRANCH * _nbytes((tm, tn), BF16)
              + _nbytes((N_BRANCH, bw, tn), BF16) + _nbytes((D, te), BF16) + 2 * _nbytes((tm, te), F32))
    scratch = _nbytes((tm, D), BF16) + _nbytes((tm, D), F32) + 2 * _nbytes((tm, 128), F32)
    return pl.pallas_call(
        functools.partial(_mix_kernel, nj=nj, ne=ne, ni=ni),
        out_shape=jax.ShapeDtypeStruct((R, D), F32),
        grid=(ni + 1, nj + ne),
        in_specs=([pl.BlockSpec((N_BRANCH, tm, bw), lambda i, j: (0, ia(i, j), 0))]
                  + [gates(g) for g in range(N_BRANCH)]
                  + [pl.BlockSpec((N_BRANCH, bw, tn), lambda i, j: (0, 0, ja(j))),
                     pl.BlockSpec((D, te), lambda i, j: (0, jb(j))),
                     pl.BlockSpec((tm, te), lambda i, j: (ip(i), jb(j))),
                     pl.BlockSpec((1, 3, te), lambda i, j: (mod_row(ip(i) * tm), 0, jb(j))),
                     pl.BlockSpec((1, te), lambda i, j: (0, jb(j)))]),
        out_specs=pl.BlockSpec((tm, te), lambda i, j: (ip(i), jo(i, j))),
        scratch_shapes=[pltpu.VMEM((nj, tm, tn), BF16), pltpu.VMEM((ne, tm, te), F32),
                        pltpu.VMEM((tm, 128), F32), pltpu.VMEM((tm, 128), F32)],
        compiler_params=_compiler_params(("arbitrary", "arbitrary"), blocks, scratch,
                                         _nbytes((tm, D), BF16) + 6 * _nbytes((tm, tn), F32)),
        name="branch_mix",
    )(ys, p, p, p, p, consts["w_branch"], consts["w_out"], x2, mod3, consts["post_g"])


def _swap_halves(w):
    h = w.shape[-1] // 2
    return jnp.concatenate([w[..., h:], w[..., :h]], axis=-1)


def _relayout_w_in(w):
    rope_end = ROPE_OFF + QK_ROPE
    head_end = rope_end + HEAD_W - CQ_OFF
    rope = w[:, ROPE_OFF:rope_end]
    pad = jnp.zeros((w.shape[0], CQ_OFF - ROPE_OFF - 2 * QK_ROPE), w.dtype)
    head = jnp.concatenate([w[:, :ROPE_OFF], rope, _swap_halves(rope), pad, w[:, rope_end:head_end]], axis=1)
    return head.astype(BF16), w[:, head_end:].astype(BF16)


def _relayout_w_uq(w):
    w = w.reshape(w.shape[0], MLA_HEADS, QK_NOPE + QK_ROPE)
    rope = w[..., QK_NOPE:]
    return jnp.concatenate([w[..., :QK_NOPE], rope, _swap_halves(rope)], axis=-1).reshape(w.shape[0], -1).astype(BF16)


def _rope_table(n, rotate):
    half = QK_ROPE // 2
    if rotate:
        pos = np.arange(n)
        inv = ROPE_THETA ** (-np.arange(0, half, 2, dtype=np.float64) / half)
        ang = np.concatenate([(pos // GRID_W)[:, None] * inv, (pos % GRID_W)[:, None] * inv], axis=-1)
    else:
        ang = np.zeros((n, half))
    cos, sin = np.cos(ang), np.sin(ang)
    return jnp.asarray(np.concatenate([cos, cos, -sin, sin], axis=-1), F32)


def _channel_dft(bw):
    gw = bw // FOURIER_GROUPS
    a = np.arange(gw)
    ang = 2.0 * np.pi * ((a[:, None] * a[None, :]) % gw) / gw
    return jnp.asarray(np.cos(ang), F32).astype(BF16), jnp.asarray(np.sin(ang), F32).astype(BF16)


def kernel(x, c, ctx, c_ctx, w_mod, b_mod, pre_g, post_g, w_in, q_norm_g, kv_norm_g, w_uq, w_ukv,
           conv_w, conv_b, sgu_ln_g, sgu_ln_b, sgu_w, sgu_b, w_branch, w_out):
    B, n, D = x.shape
    nc = ctx.shape[1]
    depth = w_in.shape[0]
    bw = w_branch.shape[2]
    assert w_in.shape[2] == ROPE_OFF + QK_ROPE + (HEAD_W - CQ_OFF) + N_BRANCH_COLS * bw + N_BRANCH * D
    assert kv_norm_g.shape[1] == ROPE_OFF and q_norm_g.shape[1] == HEAD_W - CQ_OFF and B <= 4

    cc = jnp.zeros((8, D), F32).at[:B].set(c).at[B].set(c_ctx)
    mod = _modulation(cc, w_mod, b_mod).reshape(depth, 8, 3, D)

    cs_x, cs_c = _rope_table(n, True), _rope_table(nc, False)
    ccb, scb = _channel_dft(bw)
    x2 = x.reshape(B * n, D)
    c2 = ctx.reshape(B * nc, D)
    row_x = lambda r: r // n
    row_c = lambda r: B

    for l in range(depth):
        last = l == depth - 1
        w_head, w_rest = _relayout_w_in(w_in[l])
        w_q2 = _relayout_w_uq(w_uq[l])
        w_ukv_l = w_ukv[l].astype(BF16)
        consts = dict(
            ccb=ccb, scb=scb, conv_w=conv_w[l], conv_b=conv_b[l].reshape(1, bw),
            ln_g=sgu_ln_g[l].reshape(1, bw), ln_b=sgu_ln_b[l].reshape(1, bw),
            w_s=sgu_w[l].astype(BF16),
            b_s=jnp.broadcast_to(sgu_b[l][:, :, None], sgu_b.shape[1:] + (bw // sgu_w.shape[1],)),
            w_branch=w_branch[l].astype(BF16), w_out=w_out[l].astype(BF16), post_g=post_g[l].reshape(1, D))

        hx, px = _in_projection(x2, mod[l], row_x, pre_g[l], w_head, w_rest, min(1024, n))
        hc, *pc = _in_projection(c2, mod[l], row_c, pre_g[l], w_head, None if last else w_rest,
                                 min(1024, B * nc))
        q_c, k_c, v_c = _mla_prep(hc, cs_c, kv_norm_g[l], q_norm_g[l], w_ukv_l, w_q2, B, nc)
        q_x, k_x, v_x = _mla_prep(hx, cs_x, kv_norm_g[l], q_norm_g[l], w_ukv_l, w_q2, B, n)

        attn_x = _attention(q_x, k_c, v_c, k_x, v_x).reshape(B * n, -1)
        four_x = _position_dft(px, 1, bw, B, n)
        new_x = _mix(px, attn_x, four_x, x2, mod[l], row_x, consts, n, n)
        if not last:
            attn_c = _attention(q_c, k_c, v_c).reshape(B * nc, -1)
            four_c = _position_dft(pc[0], 1, bw, B, nc)
            c2 = _mix(pc[0], attn_c, four_c, c2, mod[l], row_c, consts, nc, B * nc)
        x2 = new_x
    return x2.reshape(B, n, D)
```

```python
import functools

import jax
import jax.numpy as jnp
import numpy as np
from jax import lax
from jax.experimental import pallas as pl
from jax.experimental.pallas import tpu as pltpu

F32 = jnp.float32
BF16 = jnp.bfloat16

MLA_HEADS = 4
QK_NOPE = 128
QK_ROPE = 64
V_HEAD = 128
GRID_W = 64
ROPE_THETA = 10000.0
FOURIER_GROUPS = 4
CHUNK = 128
N_BRANCH = 4
N_BRANCH_COLS = 10
EPS = 1e-6

ROPE_OFF = 256
CQ_OFF = 512
HEAD_W = 1024
QK_W = QK_NOPE + 2 * QK_ROPE

FFT_N1 = 16
FFT_STEP_CHUNKS = 4

V7X_VMEM_BYTES = 64 * 2**20
VMEM_CAP_BYTES = V7X_VMEM_BYTES - 8 * 2**20


def _nbytes(shape, dtype):
    return int(np.prod(shape)) * jnp.dtype(dtype).itemsize


def _compiler_params(semantics, block_bytes, scratch_bytes=0, temp_bytes=0):
    need = 2 * block_bytes + scratch_bytes + temp_bytes + 2 * 2**20
    return pltpu.CompilerParams(dimension_semantics=semantics,
                                vmem_limit_bytes=int(min(max(need, 16 * 2**20), VMEM_CAP_BYTES)))


def _sigmoid(v):
    return 0.5 * jnp.tanh(0.5 * v) + 0.5


def _silu(v):
    return v * _sigmoid(v)


def _rms(v, g):
    return v * lax.rsqrt(jnp.mean(v * v, axis=-1, keepdims=True) + EPS) * g


def _dot(a, b):
    return jnp.dot(a, b, preferred_element_type=F32)


def _mod_kernel(c_ref, w_ref, b_ref, o_ref):
    @pl.when(pl.program_id(1) == 0)
    def _():
        o_ref[0] = jnp.broadcast_to(b_ref[0], o_ref.shape[1:])

    s = _silu(c_ref[...]).astype(BF16)
    o_ref[0] += _dot(s, w_ref[0].astype(BF16))


def _modulation(cc, w_mod, b_mod):
    L, D, W = w_mod.shape
    tk = 256 if D % 256 == 0 else D
    blocks = _nbytes((tk, W), F32) + _nbytes((8, tk), F32) + 2 * _nbytes((8, W), F32)
    return pl.pallas_call(
        _mod_kernel,
        out_shape=jax.ShapeDtypeStruct((L, 8, W), F32),
        grid=(L, D // tk),
        in_specs=[pl.BlockSpec((8, tk), lambda l, k: (0, k)),
                  pl.BlockSpec((1, tk, W), lambda l, k: (l, k, 0)),
                  pl.BlockSpec((1, 1, W), lambda l, k: (l, 0, 0))],
        out_specs=pl.BlockSpec((1, 8, W), lambda l, k: (l, 0, 0)),
        compiler_params=_compiler_params(("parallel", "arbitrary"), blocks, 0, _nbytes((tk, W), BF16)),
        name="modulation",
    )(cc, w_mod, b_mod.reshape(L, 1, W))


def _inproj_kernel(*refs, with_rest):
    if with_rest:
        x_ref, mod_ref, g_ref, wh_ref, wr_ref, oh_ref, or_ref, h_ref = refs
    else:
        x_ref, mod_ref, g_ref, wh_ref, oh_ref, h_ref = refs
    j = pl.program_id(1)

    @pl.when(j == 0)
    def _():
        y = _rms(x_ref[...], g_ref[...])
        h_ref[...] = (y * (1.0 + mod_ref[0, 1:2, :]) + mod_ref[0, 0:1, :]).astype(BF16)
        oh_ref[...] = _dot(h_ref[...], wh_ref[...]).astype(oh_ref.dtype)

    if with_rest:
        @pl.when(j > 0)
        def _():
            or_ref[...] = _dot(h_ref[...], wr_ref[...]).astype(or_ref.dtype)


def _in_projection(x2, mod3, mod_row, pre_g, w_head, w_rest, tm):
    R, D = x2.shape
    tn = w_head.shape[1]
    with_rest = w_rest is not None
    nj = 1 + (w_rest.shape[1] // tn if with_rest else 0)
    rest_col = lambda j: jnp.maximum(j - 1, 0)
    in_specs = [pl.BlockSpec((tm, D), lambda i, j: (i, 0)),
                pl.BlockSpec((1, 3, D), lambda i, j: (mod_row(i * tm), 0, 0)),
                pl.BlockSpec((1, D), lambda i, j: (0, 0)),
                pl.BlockSpec((D, tn), lambda i, j: (0, 0))]
    out_specs = [pl.BlockSpec((tm, tn), lambda i, j: (i, 0))]
    out_shape = [jax.ShapeDtypeStruct((R, tn), BF16)]
    args = [x2, mod3, pre_g.reshape(1, D), w_head]
    if with_rest:
        in_specs.append(pl.BlockSpec((D, tn), lambda i, j: (0, rest_col(j))))
        out_specs.append(pl.BlockSpec((tm, tn), lambda i, j: (i, rest_col(j))))
        out_shape.append(jax.ShapeDtypeStruct((R, w_rest.shape[1]), BF16))
        args.append(w_rest)
    blocks = (_nbytes((tm, D), F32) + _nbytes((3, D), F32) + 2 * _nbytes((D, tn), BF16)
              + 2 * _nbytes((tm, tn), BF16))
    return pl.pallas_call(
        functools.partial(_inproj_kernel, with_rest=with_rest),
        out_shape=out_shape,
        grid=(R // tm, nj),
        in_specs=in_specs,
        out_specs=out_specs,
        scratch_shapes=[pltpu.VMEM((tm, D), BF16)],
        compiler_params=_compiler_params(("parallel", "arbitrary"), blocks, _nbytes((tm, D), BF16),
                                         2 * _nbytes((tm, D), F32) + _nbytes((tm, tn), F32)),
        name="in_projection",
    )(*args)


def _rope_pair(group, cs):
    r = group * cs
    return r + pltpu.roll(r, QK_ROPE, axis=1)


def _prep_kernel(p_ref, cs_ref, kvg_ref, qg_ref, wkv_ref, wq_ref, q_ref, k_ref, v_ref, *, scale):
    cs = cs_ref[...]
    kvn = _rms(p_ref[:, 0:ROPE_OFF].astype(F32), kvg_ref[...]).astype(BF16)
    kv = _dot(kvn, wkv_ref[...])
    kr = _rope_pair(p_ref[:, ROPE_OFF:ROPE_OFF + 2 * QK_ROPE].astype(F32), cs)
    lane = lax.broadcasted_iota(jnp.int32, kr.shape, 1)
    kr = jnp.where(lane < QK_ROPE, kr, 0.0).astype(BF16)
    cqn = _rms(p_ref[:, CQ_OFF:HEAD_W].astype(F32), qg_ref[...]).astype(BF16)
    q = _dot(cqn, wq_ref[...]) * scale
    kvw = QK_NOPE + V_HEAD
    for h in range(MLA_HEADS):
        k_ref[0, h, :, 0:QK_NOPE] = kv[:, h * kvw:h * kvw + QK_NOPE].astype(BF16)
        k_ref[0, h, :, QK_NOPE:QK_W] = kr
        v_ref[0, h, :, 0:V_HEAD] = kv[:, h * kvw + QK_NOPE:(h + 1) * kvw].astype(BF16)
        v_ref[0, h, :, V_HEAD:2 * V_HEAD] = jnp.ones((kv.shape[0], V_HEAD), BF16)
        q_ref[0, h, :, 0:QK_NOPE] = q[:, h * QK_W:h * QK_W + QK_NOPE].astype(BF16)
        q_ref[0, h, :, QK_NOPE:QK_W] = _rope_pair(q[:, h * QK_W + QK_NOPE:(h + 1) * QK_W], cs).astype(BF16)


def _mla_prep(p, cs, kv_norm_g, q_norm_g, w_ukv, w_q2, B, n):
    tm = min(512, n)
    nt = n // tm
    H = MLA_HEADS
    kv_lora, q_lora = w_ukv.shape[0], w_q2.shape[0]
    blocks = (_nbytes((tm, HEAD_W), BF16) + _nbytes((tm, 128), F32) + _nbytes(w_ukv.shape, BF16)
              + _nbytes(w_q2.shape, BF16) + 2 * _nbytes((H, tm, QK_W), BF16) + _nbytes((H, tm, 2 * V_HEAD), BF16))
    qk_spec = pl.BlockSpec((1, H, tm, QK_W), lambda b, i: (b, 0, i, 0))
    return pl.pallas_call(
        functools.partial(_prep_kernel, scale=float(QK_NOPE + QK_ROPE) ** -0.5),
        out_shape=(jax.ShapeDtypeStruct((B, H, n, QK_W), BF16),
                   jax.ShapeDtypeStruct((B, H, n, QK_W), BF16),
                   jax.ShapeDtypeStruct((B, H, n, 2 * V_HEAD), BF16)),
        grid=(B, nt),
        in_specs=[pl.BlockSpec((tm, HEAD_W), lambda b, i: (b * nt + i, 0)),
                  pl.BlockSpec((tm, 128), lambda b, i: (i, 0)),
                  pl.BlockSpec((1, kv_lora), lambda b, i: (0, 0)),
                  pl.BlockSpec((1, q_lora), lambda b, i: (0, 0)),
                  pl.BlockSpec(w_ukv.shape, lambda b, i: (0, 0)),
                  pl.BlockSpec(w_q2.shape, lambda b, i: (0, 0))],
        out_specs=(qk_spec, qk_spec, pl.BlockSpec((1, H, tm, 2 * V_HEAD), lambda b, i: (b, 0, i, 0))),
        compiler_params=_compiler_params(("parallel", "parallel"), blocks, 0, 6 * _nbytes((tm, HEAD_W), F32)),
        name="mla_prep",
    )(p, cs, kv_norm_g.reshape(1, -1), q_norm_g.reshape(1, -1), w_ukv, w_q2)


_NT = (((1,), (1,)), ((), ()))


def _attn_kernel(*refs, with_x, tk):
    if with_x:
        q_ref, kc_ref, vc_ref, kx_ref, vx_ref, o_ref = refs
    else:
        q_ref, kc_ref, vc_ref, o_ref = refs
    q = q_ref[0, 0]

    s = lax.dot_general(q, kc_ref[0, 0], _NT, preferred_element_type=F32)
    m = jnp.max(s, axis=-1, keepdims=True)
    acc = _dot(jnp.exp(s - m).astype(BF16), vc_ref[0, 0])
    if with_x:
        for c in range(kx_ref.shape[2] // tk):
            s = lax.dot_general(q, kx_ref[0, 0, c * tk:(c + 1) * tk, :], _NT, preferred_element_type=F32)
            m_new = jnp.maximum(m, jnp.max(s, axis=-1, keepdims=True))
            p = jnp.exp(s - m_new).astype(BF16)
            acc = acc * jnp.exp(m - m_new) + _dot(p, vx_ref[0, 0, c * tk:(c + 1) * tk, :])
            m = m_new
    o_ref[0] = (acc[:, :V_HEAD] / acc[:, V_HEAD:]).astype(o_ref.dtype)


def _attention(q, kc, vc, kx=None, vx=None):
    B, H, nq, _ = q.shape
    nc = kc.shape[2]
    with_x = kx is not None
    nk = nc + (kx.shape[2] if with_x else 0)
    tq = min(1024, nq)
    tk = min(1024, nq)
    q_spec = pl.BlockSpec((1, 1, tq, QK_W), lambda b, h, i: (b, h, i, 0))

    def full(arr):
        return pl.BlockSpec((1, 1) + arr.shape[2:], lambda b, h, i: (b, h, 0, 0))

    args = [q, kc, vc] + ([kx, vx] if with_x else [])
    blocks = _nbytes((tq, QK_W), BF16) + _nbytes((nk, QK_W + 2 * V_HEAD), BF16) + _nbytes((tq, V_HEAD), BF16)
    return pl.pallas_call(
        functools.partial(_attn_kernel, with_x=with_x, tk=tk),
        out_shape=jax.ShapeDtypeStruct((B, nq, H * V_HEAD), BF16),
        grid=(B, H, nq // tq),
        in_specs=[q_spec] + [full(a) for a in args[1:]],
        out_specs=pl.BlockSpec((1, tq, V_HEAD), lambda b, h, i: (b, i, h)),
        compiler_params=_compiler_params(("parallel", "parallel", "parallel"), blocks, 0,
                                         6 * _nbytes((tq, tk), F32)),
        name="attention",
    )(*args)


def _fft1_kernel(u_ref, m_ref, tc_ref, ts_ref, o_ref, *, reps):
    rows = FFT_N1 * FFT_N1
    shape3 = (FFT_N1, FFT_N1, u_ref.shape[-1])
    for r in range(u_ref.shape[2]):
        u = u_ref[0, :, r].reshape(rows, u_ref.shape[-1])
        g = _dot(m_ref[...], u)
        gr, gi = g[:rows], g[rows:]
        tc = jnp.concatenate([tc_ref[:, r].reshape(rows, 128)] * reps, axis=1)
        ts = jnp.concatenate([ts_ref[:, r].reshape(rows, 128)] * reps, axis=1)
        o_ref[0, 0, :, r] = (gr * tc + gi * ts).reshape(shape3).astype(o_ref.dtype)
        o_ref[0, 1, :, r] = (gi * tc - gr * ts).reshape(shape3).astype(o_ref.dtype)


def _fft2_kernel(g_ref, fa_ref, fb_ref, o_ref, *, n2):
    for d in range(g_ref.shape[2]):
        y = _dot(fa_ref[...], g_ref[0, 0, d]) + _dot(fb_ref[...], g_ref[0, 1, d])
        o_ref[0, 0, d] = y[:n2].astype(o_ref.dtype)
        o_ref[0, 1, d] = y[n2:].astype(o_ref.dtype)


def _dft_tables(n):
    n1 = FFT_N1
    n2 = n // n1
    a = np.arange(n1)
    ang1 = 2.0 * np.pi * ((a[:, None] * a[None, :]) % n1) / n1
    eye = np.eye(n1)
    m1 = np.concatenate([np.kron(np.cos(ang1), eye), -np.kron(np.sin(ang1), eye)], axis=0)
    b = np.arange(n2)
    angt = 2.0 * np.pi * (a[:, None] * b[None, :]) / n
    shape4 = (n1, n2 // n1, n1, 128)
    tc = np.broadcast_to(np.cos(angt)[:, :, None], (n1, n2, 128)).reshape(shape4)
    ts = np.broadcast_to(np.sin(angt)[:, :, None], (n1, n2, 128)).reshape(shape4)
    ang2 = 2.0 * np.pi * ((b[:, None] * b[None, :]) % n2) / n2
    c2, s2 = np.cos(ang2), np.sin(ang2)
    fa = np.concatenate([c2, -s2], axis=0)
    fb = np.concatenate([s2, c2], axis=0)
    to_bf16 = lambda t: jnp.asarray(t, F32).astype(BF16)
    return to_bf16(m1), jnp.asarray(tc, F32), jnp.asarray(ts, F32), to_bf16(fa), to_bf16(fb)


def _position_dft(p, col_block, C, B, n):
    n1 = FFT_N1
    n2 = n // n1
    nb = n2 // n1
    m1, tc, ts, fa, fb = _dft_tables(n)
    p5 = p.reshape(B, n1, nb, n1, p.shape[1])
    rb = min(FFT_STEP_CHUNKS, nb)
    blocks = (3 * rb * _nbytes((n1 * n1, C), BF16) + _nbytes(m1.shape, BF16)
              + 2 * rb * _nbytes((n1 * n1, 128), F32))
    g = pl.pallas_call(
        functools.partial(_fft1_kernel, reps=C // 128),
        out_shape=jax.ShapeDtypeStruct((B, 2, n1, nb, n1, C), BF16),
        grid=(B, nb // rb),
        in_specs=[pl.BlockSpec((1, n1, rb, n1, C), lambda b, r: (b, 0, r, 0, col_block)),
                  pl.BlockSpec(m1.shape, lambda b, r: (0, 0)),
                  pl.BlockSpec((n1, rb, n1, 128), lambda b, r: (0, r, 0, 0)),
                  pl.BlockSpec((n1, rb, n1, 128), lambda b, r: (0, r, 0, 0))],
        out_specs=pl.BlockSpec((1, 2, n1, rb, n1, C), lambda b, r: (b, 0, 0, r, 0, 0)),
        compiler_params=_compiler_params(("parallel", "parallel"), blocks, 0, 8 * _nbytes((n1 * n1, C), F32)),
        name="position_dft_stage1",
    )(p5, m1, tc, ts)
    g = g.reshape(B, 2, n1, n2, C)
    db = min(FFT_STEP_CHUNKS, n1)
    blocks = 4 * db * _nbytes((n2, C), BF16) + 2 * _nbytes(fa.shape, BF16)
    return pl.pallas_call(
        functools.partial(_fft2_kernel, n2=n2),
        out_shape=jax.ShapeDtypeStruct((B, 2, n1, n2, C), BF16),
        grid=(B, n1 // db),
        in_specs=[pl.BlockSpec((1, 2, db, n2, C), lambda b, d: (b, 0, d, 0, 0)),
                  pl.BlockSpec(fa.shape, lambda b, d: (0, 0)),
                  pl.BlockSpec(fb.shape, lambda b, d: (0, 0))],
        out_specs=pl.BlockSpec((1, 2, db, n2, C), lambda b, d: (b, 0, d, 0, 0)),
        compiler_params=_compiler_params(("parallel", "parallel"), blocks, 0, 4 * _nbytes((n2, C), F32)),
        name="position_dft_stage2",
    )(g, fa, fb)


def _branch_kernel(attn_ref, vr_ref, vi_ref, zA_ref, zB_ref, xC_ref, bC_ref, cC_ref, zC_ref, uD_ref, vD_ref,
                   zD_ref, xCp_ref, cCp_ref, xCn_ref, cCn_ref, ccb_ref, scb_ref, convw_ref, convb_ref, lng_ref,
                   lnb_ref, ws_ref, bsb_ref, ys_ref, fr_ref, fi_ref, *, n, tm, halo, fscale):
    i = pl.program_id(0)

    def gate(ref):
        return _silu(ref[...].astype(F32))

    ys_ref[0] = (attn_ref[...].astype(F32) * gate(zA_ref)).astype(BF16)
    gwf = ccb_ref.shape[0]
    for d in range(FFT_N1):
        rows = pl.ds(d, tm // FFT_N1, stride=FFT_N1)
        vr, vi = vr_ref[0, 0, d].astype(F32), vi_ref[0, 0, d].astype(F32)
        for g in range(fr_ref.shape[0]):
            fr_ref[g, rows, :] = vr[:, g * gwf:(g + 1) * gwf]
            fi_ref[g, rows, :] = vi[:, g * gwf:(g + 1) * gwf]
    four = jnp.concatenate(
        [_dot(fr_ref[g].astype(BF16), ccb_ref[...]) + _dot(fi_ref[g].astype(BF16), scb_ref[...])
         for g in range(fr_ref.shape[0])], axis=1)
    ys_ref[1] = (four * fscale * gate(zB_ref)).astype(BF16)
    u = cC_ref[...].astype(F32) * xC_ref[...].astype(F32)
    first = (i * tm) % n == 0
    last = ((i + 1) * tm) % n == 0
    up = (cCp_ref[...].astype(F32) * xCp_ref[...].astype(F32))[halo - 1:halo, :]
    un = (cCn_ref[...].astype(F32) * xCn_ref[...].astype(F32))[0:1, :]
    up = up * jnp.where(first, 0.0, 1.0)
    un = un * jnp.where(last, 0.0, 1.0)
    row = lax.broadcasted_iota(jnp.int32, u.shape, 0)
    prev = jnp.where(row == 0, up, pltpu.roll(u, 1, axis=0))
    nxt = jnp.where(row == tm - 1, un, pltpu.roll(u, tm - 1, axis=0))
    conv = prev * convw_ref[0:1, :] + u * convw_ref[1:2, :] + nxt * convw_ref[2:3, :] + convb_ref[...]
    ys_ref[2] = (bC_ref[...].astype(F32) * conv * gate(zC_ref)).astype(BF16)
    v = vD_ref[...].astype(F32)
    vc = v - jnp.mean(v, axis=-1, keepdims=True)
    vn = vc * lax.rsqrt(jnp.mean(vc * vc, axis=-1, keepdims=True) + EPS) * lng_ref[...] + lnb_ref[...]
    vn = vn.astype(BF16)
    ug = uD_ref[...].astype(F32) * gate(zD_ref)
    gw = vn.shape[1] // ws_ref.shape[0]
    for g in range(ws_ref.shape[0]):
        for c in range(tm // CHUNK):
            rs, cs = slice(c * CHUNK, (c + 1) * CHUNK), slice(g * gw, (g + 1) * gw)
            mixed = _dot(ws_ref[g], vn[rs, cs]) + bsb_ref[g]
            ys_ref[3, rs, cs] = (ug[rs, cs] * mixed).astype(BF16)


def _mix_kernel(ys_ref, g0_ref, g1_ref, g2_ref, g3_ref, wb_ref, wo_ref, x_ref, mod_ref, postg_ref, o_ref,
                merged_ref, acc_ref, ss_ref, ssprev_ref, *, nj, ne, ni):
    i = pl.program_id(0)
    j = pl.program_id(1)

    @pl.when((j < nj) & (i < ni))
    def _():
        merged = None
        for g, g_ref in enumerate((g0_ref, g1_ref, g2_ref, g3_ref)):
            term = _sigmoid(g_ref[...].astype(F32)) * _dot(ys_ref[g], wb_ref[g])
            merged = term if merged is None else merged + term
        merged_ref[j] = merged.astype(BF16)

    @pl.when((j == nj) & (i > 0))
    def _():
        ssprev_ref[...] = ss_ref[...]

    @pl.when((j >= nj) & (i > 0))
    def _():
        width = acc_ref.shape[0] * acc_ref.shape[2]
        y = acc_ref[j - nj] * lax.rsqrt(ssprev_ref[:, 0:1] / width + EPS) * postg_ref[...]
        o_ref[...] = x_ref[...] + mod_ref[0, 2:3, :] * y

    @pl.when((j >= nj) & (i < ni))
    def _():
        merged = jnp.concatenate([merged_ref[c] for c in range(nj)], axis=1)
        part = _dot(merged, wo_ref[...])
        acc_ref[j - nj] = part
        ss = jnp.broadcast_to(jnp.sum(part * part, axis=-1, keepdims=True), ss_ref.shape)

        @pl.when(j == nj)
        def _():
            ss_ref[...] = ss

        @pl.when(j > nj)
        def _():
            ss_ref[...] += ss


def _mix(p, attn, v, x2, mod3, mod_row, consts, n, tile_rows):
    R, D = x2.shape
    bw = attn.shape[1]

    tb = min(512, n)
    halo = 16
    nt = n // tb
    cl = tb // FFT_N1

    def seg(k):
        return pl.BlockSpec((tb, bw), lambda i: (i, k))

    def halo_prev(k):
        return pl.BlockSpec((halo, bw), lambda i: (jnp.maximum(i * (tb // halo) - 1, 0), k))

    def halo_next(k):
        return pl.BlockSpec((halo, bw), lambda i: (jnp.minimum((i + 1) * (tb // halo), R // halo - 1), k))

    def const(arr):
        nd = arr.ndim
        return pl.BlockSpec(arr.shape, lambda i: (0,) * nd)

    branch_consts = [consts[k] for k in ("ccb", "scb", "conv_w", "conv_b", "ln_g", "ln_b", "w_s", "b_s")]
    gwf = bw // FOURIER_GROUPS
    blocks = (12 * _nbytes((tb, bw), BF16) + 4 * _nbytes((halo, bw), BF16) + N_BRANCH * _nbytes((tb, bw), BF16)
              + sum(_nbytes(a.shape, a.dtype) for a in branch_consts))
    ys = pl.pallas_call(
        functools.partial(_branch_kernel, n=n, tm=tb, halo=halo, fscale=float(n * gwf) ** -0.5),
        out_shape=jax.ShapeDtypeStruct((N_BRANCH, R, bw), BF16),
        grid=(R // tb,),
        in_specs=([pl.BlockSpec((tb, bw), lambda i: (i, 0)),
                   pl.BlockSpec((1, 1, FFT_N1, cl, bw), lambda i: (i // nt, 0, 0, i % nt, 0)),
                   pl.BlockSpec((1, 1, FFT_N1, cl, bw), lambda i: (i // nt, 1, 0, i % nt, 0))]
                  + [seg(k) for k in (0, 2, 3, 4, 5, 6, 7, 8, 9)]
                  + [halo_prev(3), halo_prev(5), halo_next(3), halo_next(5)]
                  + [const(a) for a in branch_consts]),
        out_specs=pl.BlockSpec((N_BRANCH, tb, bw), lambda i: (0, i, 0)),
        scratch_shapes=[pltpu.VMEM((FOURIER_GROUPS, tb, gwf), F32), pltpu.VMEM((FOURIER_GROUPS, tb, gwf), F32)],
        compiler_params=_compiler_params(("parallel",), blocks, 2 * _nbytes((tb, bw), F32),
                                         12 * _nbytes((tb, bw), F32)),
        name="branch_outputs",
    )(attn, v, v, *([p] * 9), *([p] * 4), *branch_consts)

    tm = min(1024, tile_rows)
    tn = min(512, D)
    te = min(512, D)
    nj, ne = D // tn, D // te
    ni = R // tm
    gate0 = N_BRANCH_COLS * bw // tn
    ia = lambda i, j: jnp.minimum(jnp.where(j < nj, i, i + 1), ni - 1)
    ja = lambda j: jnp.where(j < nj, j, 0)
    jb = lambda j: jnp.maximum(j - nj, 0)
    ip = lambda i: jnp.maximum(i - 1, 0)
    jo = lambda i, j: jnp.where(i == 0, 0, jb(j))

    def gates(g):
        return pl.BlockSpec((tm, tn), lambda i, j: (ia(i, j), gate0 + g * nj + ja(j)))

    blocks = (_nbytes((N_BRANCH, tm, bw), BF16) + N_BRANCH * _nbytes((tm, tn), BF16)
              + _nbytes((N_BRANCH, bw, tn), BF16) + _nbytes((D, te), BF16) + 2 * _nbytes((tm, te), F32))
    scratch = _nbytes((tm, D), BF16) + _nbytes((tm, D), F32) + 2 * _nbytes((tm, 128), F32)
    return pl.pallas_call(
        functools.partial(_mix_kernel, nj=nj, ne=ne, ni=ni),
        out_shape=jax.ShapeDtypeStruct((R, D), F32),
        grid=(ni + 1, nj + ne),
        in_specs=([pl.BlockSpec((N_BRANCH, tm, bw), lambda i, j: (0, ia(i, j), 0))]
                  + [gates(g) for g in range(N_BRANCH)]
                  + [pl.BlockSpec((N_BRANCH, bw, tn), lambda i, j: (0, 0, ja(j))),
                     pl.BlockSpec((D, te), lambda i, j: (0, jb(j))),
                     pl.BlockSpec((tm, te), lambda i, j: (ip(i), jb(j))),
                     pl.BlockSpec((1, 3, te), lambda i, j: (mod_row(ip(i) * tm), 0, jb(j))),
                     pl.BlockSpec((1, te), lambda i, j: (0, jb(j)))]),
        out_specs=pl.BlockSpec((tm, te), lambda i, j: (ip(i), jo(i, j))),
        scratch_shapes=[pltpu.VMEM((nj, tm, tn), BF16), pltpu.VMEM((ne, tm, te), F32),
                        pltpu.VMEM((tm, 128), F32), pltpu.VMEM((tm, 128), F32)],
        compiler_params=_compiler_params(("arbitrary", "arbitrary"), blocks, scratch,
                                         _nbytes((tm, D), BF16) + 6 * _nbytes((tm, tn), F32)),
        name="branch_mix",
    )(ys, p, p, p, p, consts["w_branch"], consts["w_out"], x2, mod3, consts["post_g"])


def _swap_halves(w):
    h = w.shape[-1] // 2
    return jnp.concatenate([w[..., h:], w[..., :h]], axis=-1)


def _relayout_w_in(w):
    rope_end = ROPE_OFF + QK_ROPE
    head_end = rope_end + HEAD_W - CQ_OFF
    rope = w[:, ROPE_OFF:rope_end]
    pad = jnp.zeros((w.shape[0], CQ_OFF - ROPE_OFF - 2 * QK_ROPE), w.dtype)
    head = jnp.concatenate([w[:, :ROPE_OFF], rope, _swap_halves(rope), pad, w[:, rope_end:head_end]], axis=1)
    return head.astype(BF16), w[:, head_end:].astype(BF16)


def _relayout_w_uq(w):
    w = w.reshape(w.shape[0], MLA_HEADS, QK_NOPE + QK_ROPE)
    rope = w[..., QK_NOPE:]
    return jnp.concatenate([w[..., :QK_NOPE], rope, _swap_halves(rope)], axis=-1).reshape(w.shape[0], -1).astype(BF16)


def _rope_table(n, rotate):
    half = QK_ROPE // 2
    if rotate:
        pos = np.arange(n)
        inv = ROPE_THETA ** (-np.arange(0, half, 2, dtype=np.float64) / half)
        ang = np.concatenate([(pos // GRID_W)[:, None] * inv, (pos % GRID_W)[:, None] * inv], axis=-1)
    else:
        ang = np.zeros((n, half))
    cos, sin = np.cos(ang), np.sin(ang)
    return jnp.asarray(np.concatenate([cos, cos, -sin, sin], axis=-1), F32)


def _channel_dft(bw):
    gw = bw // FOURIER_GROUPS
    a = np.arange(gw)
    ang = 2.0 * np.pi * ((a[:, None] * a[None, :]) % gw) / gw
    return jnp.asarray(np.cos(ang), F32).astype(BF16), jnp.asarray(np.sin(ang), F32).astype(BF16)


def kernel(x, c, ctx, c_ctx, w_mod, b_mod, pre_g, post_g, w_in, q_norm_g, kv_norm_g, w_uq, w_ukv,
           conv_w, conv_b, sgu_ln_g, sgu_ln_b, sgu_w, sgu_b, w_branch, w_out):
    B, n, D = x.shape
    nc = ctx.shape[1]
    depth = w_in.shape[0]
    bw = w_branch.shape[2]
    assert w_in.shape[2] == ROPE_OFF + QK_ROPE + (HEAD_W - CQ_OFF) + N_BRANCH_COLS * bw + N_BRANCH * D
    assert kv_norm_g.shape[1] == ROPE_OFF and q_norm_g.shape[1] == HEAD_W - CQ_OFF and B <= 4

    cc = jnp.zeros((8, D), F32).at[:B].set(c).at[B].set(c_ctx)
    mod = _modulation(cc, w_mod, b_mod).reshape(depth, 8, 3, D)

    cs_x, cs_c = _rope_table(n, True), _rope_table(nc, False)
    ccb, scb = _channel_dft(bw)
    x2 = x.reshape(B * n, D)
    c2 = ctx.reshape(B * nc, D)
    row_x = lambda r: r // n
    row_c = lambda r: B

    for l in range(depth):
        last = l == depth - 1
        w_head, w_rest = _relayout_w_in(w_in[l])
        w_q2 = _relayout_w_uq(w_uq[l])
        w_ukv_l = w_ukv[l].astype(BF16)
        consts = dict(
            ccb=ccb, scb=scb, conv_w=conv_w[l], conv_b=conv_b[l].reshape(1, bw),
            ln_g=sgu_ln_g[l].reshape(1, bw), ln_b=sgu_ln_b[l].reshape(1, bw),
            w_s=sgu_w[l].astype(BF16),
            b_s=jnp.broadcast_to(sgu_b[l][:, :, None], sgu_b.shape[1:] + (bw // sgu_w.shape[1],)),
            w_branch=w_branch[l].astype(BF16), w_out=w_out[l].astype(BF16), post_g=post_g[l].reshape(1, D))

        hx, px = _in_projection(x2, mod[l], row_x, pre_g[l], w_head, w_rest, min(1024, n))
        hc, *pc = _in_projection(c2, mod[l], row_c, pre_g[l], w_head, None if last else w_rest,
                                 min(1024, B * nc))
        q_c, k_c, v_c = _mla_prep(hc, cs_c, kv_norm_g[l], q_norm_g[l], w_ukv_l, w_q2, B, nc)
        q_x, k_x, v_x = _mla_prep(hx, cs_x, kv_norm_g[l], q_norm_g[l], w_ukv_l, w_q2, B, n)

        attn_x = _attention(q_x, k_c, v_c, k_x, v_x).reshape(B * n, -1)
        four_x = _position_dft(px, 1, bw, B, n)
        new_x = _mix(px, attn_x, four_x, x2, mod[l], row_x, consts, n, n)
        if not last:
            attn_c = _attention(q_c, k_c, v_c).reshape(B * nc, -1)
            four_c = _position_dft(pc[0], 1, bw, B, nc)
            c2 = _mix(pc[0], attn_c, four_c, c2, mod[l], row_c, consts, nc, B * nc)
        x2 = new_x
    return x2.reshape(B, n, D)
```

```python
import functools

import jax
import jax.numpy as jnp
import numpy as np
from jax import lax
from jax.experimental import pallas as pl
from jax.experimental.pallas import tpu as pltpu

F32 = jnp.float32
BF16 = jnp.bfloat16

MLA_HEADS = 4
QK_NOPE = 128
QK_ROPE = 64
V_HEAD = 128
GRID_W = 64
ROPE_THETA = 10000.0
FOURIER_GROUPS = 4
CHUNK = 128
N_BRANCH = 4
N_BRANCH_COLS = 10
EPS = 1e-6

ROPE_OFF = 256
CQ_OFF = 512
HEAD_W = 1024
QK_W = QK_NOPE + 2 * QK_ROPE

FFT_N1 = 16
FFT_STEP_CHUNKS = 4

V7X_VMEM_BYTES = 64 * 2**20
VMEM_CAP_BYTES = V7X_VMEM_BYTES - 8 * 2**20


def _nbytes(shape, dtype):
    return int(np.prod(shape)) * jnp.dtype(dtype).itemsize


def _compiler_params(semantics, block_bytes, scratch_bytes=0, temp_bytes=0):
    need = 2 * block_bytes + scratch_bytes + temp_bytes + 2 * 2**20
    return pltpu.CompilerParams(dimension_semantics=semantics,
                                vmem_limit_bytes=int(min(max(need, 16 * 2**20), VMEM_CAP_BYTES)))


def _sigmoid(v):
    return 0.5 * jnp.tanh(0.5 * v) + 0.5


def _silu(v):
    return v * _sigmoid(v)


def _rms(v, g):
    return v * lax.rsqrt(jnp.mean(v * v, axis=-1, keepdims=True) + EPS) * g


def _dot(a, b):
    return jnp.dot(a, b, preferred_element_type=F32)


MOD_STREAMS = 4


def _mod_kernel(c_ref, *refs):
    w_refs, (b_ref, o_ref) = refs[:MOD_STREAMS], refs[MOD_STREAMS:]

    @pl.when(pl.program_id(1) == 0)
    def _():
        o_ref[0] = jnp.broadcast_to(b_ref[0], o_ref.shape[1:])

    s = _silu(c_ref[...]).astype(BF16)
    rows = w_refs[0].shape[1]
    acc = None
    for q, w_ref in enumerate(w_refs):
        part = _dot(s[:, q * rows:(q + 1) * rows], w_ref[0].astype(BF16))
        acc = part if acc is None else acc + part
    o_ref[0] += acc


def _modulation(cc, w_mod, b_mod):
    L, D, W = w_mod.shape
    tr = 128
    tk = MOD_STREAMS * tr
    assert D % tk == 0
    blocks = _nbytes((tk, W), F32) + _nbytes((8, tk), F32) + 2 * _nbytes((8, W), F32)
    w_specs = [pl.BlockSpec((1, tr, W), lambda l, k, q=q: (l, MOD_STREAMS * k + q, 0)) for q in range(MOD_STREAMS)]
    return pl.pallas_call(
        _mod_kernel,
        out_shape=jax.ShapeDtypeStruct((L, 8, W), F32),
        grid=(L, D // tk),
        in_specs=[pl.BlockSpec((8, tk), lambda l, k: (0, k))] + w_specs
                 + [pl.BlockSpec((1, 1, W), lambda l, k: (l, 0, 0))],
        out_specs=pl.BlockSpec((1, 8, W), lambda l, k: (l, 0, 0)),
        compiler_params=_compiler_params(("parallel", "arbitrary"), blocks, 0, _nbytes((tk, W), BF16)),
        name="modulation",
    )(cc, *([w_mod] * MOD_STREAMS), b_mod.reshape(L, 1, W))


def _inproj_kernel(*refs, with_rest):
    if with_rest:
        x_ref, mod_ref, g_ref, wh_ref, wr_ref, oh_ref, or_ref, h_ref = refs
    else:
        x_ref, mod_ref, g_ref, wh_ref, oh_ref, h_ref = refs
    j = pl.program_id(1)

    @pl.when(j == 0)
    def _():
        y = _rms(x_ref[...], g_ref[...])
        h_ref[...] = (y * (1.0 + mod_ref[0, 1:2, :]) + mod_ref[0, 0:1, :]).astype(BF16)
        oh_ref[...] = _dot(h_ref[...], wh_ref[...]).astype(oh_ref.dtype)

    if with_rest:
        @pl.when(j > 0)
        def _():
            or_ref[...] = _dot(h_ref[...], wr_ref[...]).astype(or_ref.dtype)


def _in_projection(x2, mod3, mod_row, pre_g, w_head, w_rest, tm):
    R, D = x2.shape
    tn = w_head.shape[1]
    with_rest = w_rest is not None
    nj = 1 + (w_rest.shape[1] // tn if with_rest else 0)
    rest_col = lambda j: jnp.maximum(j - 1, 0)
    in_specs = [pl.BlockSpec((tm, D), lambda i, j: (i, 0)),
                pl.BlockSpec((1, 3, D), lambda i, j: (mod_row(i * tm), 0, 0)),
                pl.BlockSpec((1, D), lambda i, j: (0, 0)),
                pl.BlockSpec((D, tn), lambda i, j: (0, 0))]
    out_specs = [pl.BlockSpec((tm, tn), lambda i, j: (i, 0))]
    out_shape = [jax.ShapeDtypeStruct((R, tn), BF16)]
    args = [x2, mod3, pre_g.reshape(1, D), w_head]
    if with_rest:
        in_specs.append(pl.BlockSpec((D, tn), lambda i, j: (0, rest_col(j))))
        out_specs.append(pl.BlockSpec((tm, tn), lambda i, j: (i, rest_col(j))))
        out_shape.append(jax.ShapeDtypeStruct((R, w_rest.shape[1]), BF16))
        args.append(w_rest)
    blocks = (_nbytes((tm, D), F32) + _nbytes((3, D), F32) + 2 * _nbytes((D, tn), BF16)
              + 2 * _nbytes((tm, tn), BF16))
    return pl.pallas_call(
        functools.partial(_inproj_kernel, with_rest=with_rest),
        out_shape=out_shape,
        grid=(R // tm, nj),
        in_specs=in_specs,
        out_specs=out_specs,
        scratch_shapes=[pltpu.VMEM((tm, D), BF16)],
        compiler_params=_compiler_params(("parallel", "arbitrary"), blocks, _nbytes((tm, D), BF16),
                                         2 * _nbytes((tm, D), F32) + _nbytes((tm, tn), F32)),
        name="in_projection",
    )(*args)


def _rope_pair(group, cs):
    r = group * cs
    return r + pltpu.roll(r, QK_ROPE, axis=1)


def _prep_kernel(p_ref, cs_ref, kvg_ref, qg_ref, wkv_ref, wq_ref, q_ref, k_ref, v_ref, *, scale):
    cs = cs_ref[...]
    kvn = _rms(p_ref[:, 0:ROPE_OFF].astype(F32), kvg_ref[...]).astype(BF16)
    kv = _dot(kvn, wkv_ref[...])
    kr = _rope_pair(p_ref[:, ROPE_OFF:ROPE_OFF + 2 * QK_ROPE].astype(F32), cs)
    lane = lax.broadcasted_iota(jnp.int32, kr.shape, 1)
    kr = jnp.where(lane < QK_ROPE, kr, 0.0).astype(BF16)
    cqn = _rms(p_ref[:, CQ_OFF:HEAD_W].astype(F32), qg_ref[...]).astype(BF16)
    q = _dot(cqn, wq_ref[...]) * scale
    kvw = QK_NOPE + V_HEAD
    for h in range(MLA_HEADS):
        k_ref[0, h, :, 0:QK_NOPE] = kv[:, h * kvw:h * kvw + QK_NOPE].astype(BF16)
        k_ref[0, h, :, QK_NOPE:QK_W] = kr
        v_ref[0, h, :, 0:V_HEAD] = kv[:, h * kvw + QK_NOPE:(h + 1) * kvw].astype(BF16)
        v_ref[0, h, :, V_HEAD:2 * V_HEAD] = jnp.ones((kv.shape[0], V_HEAD), BF16)
        q_ref[0, h, :, 0:QK_NOPE] = q[:, h * QK_W:h * QK_W + QK_NOPE].astype(BF16)
        q_ref[0, h, :, QK_NOPE:QK_W] = _rope_pair(q[:, h * QK_W + QK_NOPE:(h + 1) * QK_W], cs).astype(BF16)


def _mla_prep(p, cs, kv_norm_g, q_norm_g, w_ukv, w_q2, B, n):
    tm = min(512, n)
    nt = n // tm
    H = MLA_HEADS
    kv_lora, q_lora = w_ukv.shape[0], w_q2.shape[0]
    blocks = (_nbytes((tm, HEAD_W), BF16) + _nbytes((tm, 128), F32) + _nbytes(w_ukv.shape, BF16)
              + _nbytes(w_q2.shape, BF16) + 2 * _nbytes((H, tm, QK_W), BF16) + _nbytes((H, tm, 2 * V_HEAD), BF16))
    qk_spec = pl.BlockSpec((1, H, tm, QK_W), lambda b, i: (b, 0, i, 0))
    return pl.pallas_call(
        functools.partial(_prep_kernel, scale=float(QK_NOPE + QK_ROPE) ** -0.5),
        out_shape=(jax.ShapeDtypeStruct((B, H, n, QK_W), BF16),
                   jax.ShapeDtypeStruct((B, H, n, QK_W), BF16),
                   jax.ShapeDtypeStruct((B, H, n, 2 * V_HEAD), BF16)),
        grid=(B, nt),
        in_specs=[pl.BlockSpec((tm, HEAD_W), lambda b, i: (b * nt + i, 0)),
                  pl.BlockSpec((tm, 128), lambda b, i: (i, 0)),
                  pl.BlockSpec((1, kv_lora), lambda b, i: (0, 0)),
                  pl.BlockSpec((1, q_lora), lambda b, i: (0, 0)),
                  pl.BlockSpec(w_ukv.shape, lambda b, i: (0, 0)),
                  pl.BlockSpec(w_q2.shape, lambda b, i: (0, 0))],
        out_specs=(qk_spec, qk_spec, pl.BlockSpec((1, H, tm, 2 * V_HEAD), lambda b, i: (b, 0, i, 0))),
        compiler_params=_compiler_params(("parallel", "parallel"), blocks, 0, 6 * _nbytes((tm, HEAD_W), F32)),
        name="mla_prep",
    )(p, cs, kv_norm_g.reshape(1, -1), q_norm_g.reshape(1, -1), w_ukv, w_q2)


_NT = (((1,), (1,)), ((), ()))


def _attn_kernel(*refs, with_x, tk):
    if with_x:
        q_ref, kc_ref, vc_ref, kx_ref, vx_ref, o_ref = refs
    else:
        q_ref, kc_ref, vc_ref, o_ref = refs
    q = q_ref[0, 0]

    s = lax.dot_general(q, kc_ref[0, 0], _NT, preferred_element_type=F32)
    m = jnp.max(s, axis=-1, keepdims=True)
    acc = _dot(jnp.exp(s - m).astype(BF16), vc_ref[0, 0])
    if with_x:
        for c in range(kx_ref.shape[2] // tk):
            s = lax.dot_general(q, kx_ref[0, 0, c * tk:(c + 1) * tk, :], _NT, preferred_element_type=F32)
            m_new = jnp.maximum(m, jnp.max(s, axis=-1, keepdims=True))
            p = jnp.exp(s - m_new).astype(BF16)
            acc = acc * jnp.exp(m - m_new) + _dot(p, vx_ref[0, 0, c * tk:(c + 1) * tk, :])
            m = m_new
    o_ref[0] = (acc[:, :V_HEAD] / acc[:, V_HEAD:]).astype(o_ref.dtype)


def _attention(q, kc, vc, kx=None, vx=None):
    B, H, nq, _ = q.shape
    nc = kc.shape[2]
    with_x = kx is not None
    nk = nc + (kx.shape[2] if with_x else 0)
    tq = min(1024, nq)
    tk = min(1024, nq)
    q_spec = pl.BlockSpec((1, 1, tq, QK_W), lambda b, h, i: (b, h, i, 0))

    def full(arr):
        return pl.BlockSpec((1, 1) + arr.shape[2:], lambda b, h, i: (b, h, 0, 0))

    args = [q, kc, vc] + ([kx, vx] if with_x else [])
    blocks = _nbytes((tq, QK_W), BF16) + _nbytes((nk, QK_W + 2 * V_HEAD), BF16) + _nbytes((tq, V_HEAD), BF16)
    return pl.pallas_call(
        functools.partial(_attn_kernel, with_x=with_x, tk=tk),
        out_shape=jax.ShapeDtypeStruct((B, nq, H * V_HEAD), BF16),
        grid=(B, H, nq // tq),
        in_specs=[q_spec] + [full(a) for a in args[1:]],
        out_specs=pl.BlockSpec((1, tq, V_HEAD), lambda b, h, i: (b, i, h)),
        compiler_params=_compiler_params(("parallel", "parallel", "parallel"), blocks, 0,
                                         6 * _nbytes((tq, tk), F32)),
        name="attention",
    )(*args)


def _fft1_kernel(u_ref, m_ref, tc_ref, ts_ref, o_ref, *, reps):
    rows = FFT_N1 * FFT_N1
    shape3 = (FFT_N1, FFT_N1, u_ref.shape[-1])
    for r in range(u_ref.shape[2]):
        u = u_ref[0, :, r].reshape(rows, u_ref.shape[-1])
        g = _dot(m_ref[...], u)
        gr, gi = g[:rows], g[rows:]
        tc = jnp.concatenate([tc_ref[:, r].reshape(rows, 128)] * reps, axis=1)
        ts = jnp.concatenate([ts_ref[:, r].reshape(rows, 128)] * reps, axis=1)
        o_ref[0, 0, :, r] = (gr * tc + gi * ts).reshape(shape3).astype(o_ref.dtype)
        o_ref[0, 1, :, r] = (gi * tc - gr * ts).reshape(shape3).astype(o_ref.dtype)


def _fft2_kernel(g_ref, fa_ref, fb_ref, o_ref, *, n2):
    for d in range(g_ref.shape[2]):
        y = _dot(fa_ref[...], g_ref[0, 0, d]) + _dot(fb_ref[...], g_ref[0, 1, d])
        o_ref[0, 0, d] = y[:n2].astype(o_ref.dtype)
        o_ref[0, 1, d] = y[n2:].astype(o_ref.dtype)


def _dft_tables(n):
    n1 = FFT_N1
    n2 = n // n1
    a = np.arange(n1)
    ang1 = 2.0 * np.pi * ((a[:, None] * a[None, :]) % n1) / n1
    eye = np.eye(n1)
    m1 = np.concatenate([np.kron(np.cos(ang1), eye), -np.kron(np.sin(ang1), eye)], axis=0)
    b = np.arange(n2)
    angt = 2.0 * np.pi * (a[:, None] * b[None, :]) / n
    shape4 = (n1, n2 // n1, n1, 128)
    tc = np.broadcast_to(np.cos(angt)[:, :, None], (n1, n2, 128)).reshape(shape4)
    ts = np.broadcast_to(np.sin(angt)[:, :, None], (n1, n2, 128)).reshape(shape4)
    ang2 = 2.0 * np.pi * ((b[:, None] * b[None, :]) % n2) / n2
    c2, s2 = np.cos(ang2), np.sin(ang2)
    fa = np.concatenate([c2, -s2], axis=0)
    fb = np.concatenate([s2, c2], axis=0)
    to_bf16 = lambda t: jnp.asarray(t, F32).astype(BF16)
    return to_bf16(m1), jnp.asarray(tc, F32), jnp.asarray(ts, F32), to_bf16(fa), to_bf16(fb)


def _position_dft(p, col_block, C, B, n):
    n1 = FFT_N1
    n2 = n // n1
    nb = n2 // n1
    m1, tc, ts, fa, fb = _dft_tables(n)
    p5 = p.reshape(B, n1, nb, n1, p.shape[1])
    rb = min(FFT_STEP_CHUNKS, nb)
    blocks = (3 * rb * _nbytes((n1 * n1, C), BF16) + _nbytes(m1.shape, BF16)
              + 2 * rb * _nbytes((n1 * n1, 128), F32))
    g = pl.pallas_call(
        functools.partial(_fft1_kernel, reps=C // 128),
        out_shape=jax.ShapeDtypeStruct((B, 2, n1, nb, n1, C), BF16),
        grid=(B, nb // rb),
        in_specs=[pl.BlockSpec((1, n1, rb, n1, C), lambda b, r: (b, 0, r, 0, col_block)),
                  pl.BlockSpec(m1.shape, lambda b, r: (0, 0)),
                  pl.BlockSpec((n1, rb, n1, 128), lambda b, r: (0, r, 0, 0)),
                  pl.BlockSpec((n1, rb, n1, 128), lambda b, r: (0, r, 0, 0))],
        out_specs=pl.BlockSpec((1, 2, n1, rb, n1, C), lambda b, r: (b, 0, 0, r, 0, 0)),
        compiler_params=_compiler_params(("parallel", "parallel"), blocks, 0, 8 * _nbytes((n1 * n1, C), F32)),
        name="position_dft_stage1",
    )(p5, m1, tc, ts)
    g = g.reshape(B, 2, n1, n2, C)
    db = min(FFT_STEP_CHUNKS, n1)
    blocks = 4 * db * _nbytes((n2, C), BF16) + 2 * _nbytes(fa.shape, BF16)
    return pl.pallas_call(
        functools.partial(_fft2_kernel, n2=n2),
        out_shape=jax.ShapeDtypeStruct((B, 2, n1, n2, C), BF16),
        grid=(B, n1 // db),
        in_specs=[pl.BlockSpec((1, 2, db, n2, C), lambda b, d: (b, 0, d, 0, 0)),
                  pl.BlockSpec(fa.shape, lambda b, d: (0, 0)),
                  pl.BlockSpec(fb.shape, lambda b, d: (0, 0))],
        out_specs=pl.BlockSpec((1, 2, db, n2, C), lambda b, d: (b, 0, d, 0, 0)),
        compiler_params=_compiler_params(("parallel", "parallel"), blocks, 0, 4 * _nbytes((n2, C), F32)),
        name="position_dft_stage2",
    )(g, fa, fb)


def _branch_kernel(attn_ref, vr_ref, vi_ref, zA_ref, zB_ref, xC_ref, bC_ref, cC_ref, zC_ref, uD_ref, vD_ref,
                   zD_ref, xCp_ref, cCp_ref, xCn_ref, cCn_ref, ccb_ref, scb_ref, convw_ref, convb_ref, lng_ref,
                   lnb_ref, ws_ref, bsb_ref, ys_ref, fr_ref, fi_ref, *, n, tm, halo, fscale):
    i = pl.program_id(0)

    def gate(ref):
        return _silu(ref[...].astype(F32))

    ys_ref[0] = (attn_ref[...].astype(F32) * gate(zA_ref)).astype(BF16)
    gwf = ccb_ref.shape[0]
    for d in range(FFT_N1):
        rows = pl.ds(d, tm // FFT_N1, stride=FFT_N1)
        vr, vi = vr_ref[0, 0, d].astype(F32), vi_ref[0, 0, d].astype(F32)
        for g in range(fr_ref.shape[0]):
            fr_ref[g, rows, :] = vr[:, g * gwf:(g + 1) * gwf]
            fi_ref[g, rows, :] = vi[:, g * gwf:(g + 1) * gwf]
    four = jnp.concatenate(
        [_dot(fr_ref[g].astype(BF16), ccb_ref[...]) + _dot(fi_ref[g].astype(BF16), scb_ref[...])
         for g in range(fr_ref.shape[0])], axis=1)
    ys_ref[1] = (four * fscale * gate(zB_ref)).astype(BF16)
    u = cC_ref[...].astype(F32) * xC_ref[...].astype(F32)
    first = (i * tm) % n == 0
    last = ((i + 1) * tm) % n == 0
    up = (cCp_ref[...].astype(F32) * xCp_ref[...].astype(F32))[halo - 1:halo, :]
    un = (cCn_ref[...].astype(F32) * xCn_ref[...].astype(F32))[0:1, :]
    up = up * jnp.where(first, 0.0, 1.0)
    un = un * jnp.where(last, 0.0, 1.0)
    row = lax.broadcasted_iota(jnp.int32, u.shape, 0)
    prev = jnp.where(row == 0, up, pltpu.roll(u, 1, axis=0))
    nxt = jnp.where(row == tm - 1, un, pltpu.roll(u, tm - 1, axis=0))
    conv = prev * convw_ref[0:1, :] + u * convw_ref[1:2, :] + nxt * convw_ref[2:3, :] + convb_ref[...]
    ys_ref[2] = (bC_ref[...].astype(F32) * conv * gate(zC_ref)).astype(BF16)
    v = vD_ref[...].astype(F32)
    vc = v - jnp.mean(v, axis=-1, keepdims=True)
    vn = vc * lax.rsqrt(jnp.mean(vc * vc, axis=-1, keepdims=True) + EPS) * lng_ref[...] + lnb_ref[...]
    vn = vn.astype(BF16)
    ug = uD_ref[...].astype(F32) * gate(zD_ref)
    gw = vn.shape[1] // ws_ref.shape[0]
    for g in range(ws_ref.shape[0]):
        for c in range(tm // CHUNK):
            rs, cs = slice(c * CHUNK, (c + 1) * CHUNK), slice(g * gw, (g + 1) * gw)
            mixed = _dot(ws_ref[g], vn[rs, cs]) + bsb_ref[g]
            ys_ref[3, rs, cs] = (ug[rs, cs] * mixed).astype(BF16)


def _mix_kernel(ys_ref, g0_ref, g1_ref, g2_ref, g3_ref, wb_ref, wo_ref, x_ref, mod_ref, postg_ref, o_ref,
                merged_ref, acc_ref, ss_ref, ssprev_ref, *, nj, ne, ni):
    i = pl.program_id(0)
    j = pl.program_id(1)

    @pl.when((j < nj) & (i < ni))
    def _():
        merged = None
        for g, g_ref in enumerate((g0_ref, g1_ref, g2_ref, g3_ref)):
            term = _sigmoid(g_ref[...].astype(F32)) * _dot(ys_ref[g], wb_ref[g])
            merged = term if merged is None else merged + term
        merged_ref[j] = merged.astype(BF16)

    @pl.when((j == nj) & (i > 0))
    def _():
        ssprev_ref[...] = ss_ref[...]

    @pl.when((j >= nj) & (i > 0))
    def _():
        width = acc_ref.shape[0] * acc_ref.shape[2]
        y = acc_ref[j - nj] * lax.rsqrt(ssprev_ref[:, 0:1] / width + EPS) * postg_ref[...]
        o_ref[...] = x_ref[...] + mod_ref[0, 2:3, :] * y

    @pl.when((j >= nj) & (i < ni))
    def _():
        merged = jnp.concatenate([merged_ref[c] for c in range(nj)], axis=1)
        part = _dot(merged, wo_ref[...])
        acc_ref[j - nj] = part
        ss = jnp.broadcast_to(jnp.sum(part * part, axis=-1, keepdims=True), ss_ref.shape)

        @pl.when(j == nj)
        def _():
            ss_ref[...] = ss

        @pl.when(j > nj)
        def _():
            ss_ref[...] += ss


def _mix(p, attn, v, x2, mod3, mod_row, consts, n, tile_rows):
    R, D = x2.shape
    bw = attn.shape[1]

    tb = min(512, n)
    halo = 16
    nt = n // tb
    cl = tb // FFT_N1

    def seg(k):
        return pl.BlockSpec((tb, bw), lambda i: (i, k))

    def halo_prev(k):
        return pl.BlockSpec((halo, bw), lambda i: (jnp.maximum(i * (tb // halo) - 1, 0), k))

    def halo_next(k):
        return pl.BlockSpec((halo, bw), lambda i: (jnp.minimum((i + 1) * (tb // halo), R // halo - 1), k))

    def const(arr):
        nd = arr.ndim
        return pl.BlockSpec(arr.shape, lambda i: (0,) * nd)

    branch_consts = [consts[k] for k in ("ccb", "scb", "conv_w", "conv_b", "ln_g", "ln_b", "w_s", "b_s")]
    gwf = bw // FOURIER_GROUPS
    blocks = (12 * _nbytes((tb, bw), BF16) + 4 * _nbytes((halo, bw), BF16) + N_BRANCH * _nbytes((tb, bw), BF16)
              + sum(_nbytes(a.shape, a.dtype) for a in branch_consts))
    ys = pl.pallas_call(
        functools.partial(_branch_kernel, n=n, tm=tb, halo=halo, fscale=float(n * gwf) ** -0.5),
        out_shape=jax.ShapeDtypeStruct((N_BRANCH, R, bw), BF16),
        grid=(R // tb,),
        in_specs=([pl.BlockSpec((tb, bw), lambda i: (i, 0)),
                   pl.BlockSpec((1, 1, FFT_N1, cl, bw), lambda i: (i // nt, 0, 0, i % nt, 0)),
                   pl.BlockSpec((1, 1, FFT_N1, cl, bw), lambda i: (i // nt, 1, 0, i % nt, 0))]
                  + [seg(k) for k in (0, 2, 3, 4, 5, 6, 7, 8, 9)]
                  + [halo_prev(3), halo_prev(5), halo_next(3), halo_next(5)]
                  + [const(a) for a in branch_consts]),
        out_specs=pl.BlockSpec((N_BRANCH, tb, bw), lambda i: (0, i, 0)),
        scratch_shapes=[pltpu.VMEM((FOURIER_GROUPS, tb, gwf), F32), pltpu.VMEM((FOURIER_GROUPS, tb, gwf), F32)],
        compiler_params=_compiler_params(("parallel",), blocks, 2 * _nbytes((tb, bw), F32),
                                         12 * _nbytes((tb, bw), F32)),
        name="branch_outputs",
    )(attn, v, v, *([p] * 9), *([p] * 4), *branch_consts)

    tm = min(1024, tile_rows)
    tn = min(512, D)
    te = min(512, D)
    nj, ne = D // tn, D // te
    ni = R // tm
    gate0 = N_BRANCH_COLS * bw // tn
    ia = lambda i, j: jnp.minimum(jnp.where(j < nj, i, i + 1), ni - 1)
    ja = lambda j: jnp.where(j < nj, j, 0)
    jb = lambda j: jnp.maximum(j - nj, 0)
    ip = lambda i: jnp.maximum(i - 1, 0)
    jo = lambda i, j: jnp.where(i == 0, 0, jb(j))

    def gates(g):
        return pl.BlockSpec((tm, tn), lambda i, j: (ia(i, j), gate0 + g * nj + ja(j)))

    blocks = (_nbytes((N_BRANCH, tm, bw), BF16) + N_BRANCH * _nbytes((tm, tn), BF16)
              + _nbytes((N_BRANCH, bw, tn), BF16) + _nbytes((D, te), BF16) + 2 * _nbytes((tm, te), F32))
    scratch = _nbytes((tm, D), BF16) + _nbytes((tm, D), F32) + 2 * _nbytes((tm, 128), F32)
    return pl.pallas_call(
        functools.partial(_mix_kernel, nj=nj, ne=ne, ni=ni),
        out_shape=jax.ShapeDtypeStruct((R, D), F32),
        grid=(ni + 1, nj + ne),
        in_specs=([pl.BlockSpec((N_BRANCH, tm, bw), lambda i, j: (0, ia(i, j), 0))]
                  + [gates(g) for g in range(N_BRANCH)]
                  + [pl.BlockSpec((N_BRANCH, bw, tn), lambda i, j: (0, 0, ja(j))),
                     pl.BlockSpec((D, te), lambda i, j: (0, jb(j))),
                     pl.BlockSpec((tm, te), lambda i, j: (ip(i), jb(j))),
                     pl.BlockSpec((1, 3, te), lambda i, j: (mod_row(ip(i) * tm), 0, jb(j))),
                     pl.BlockSpec((1, te), lambda i, j: (0, jb(j)))]),
        out_specs=pl.BlockSpec((tm, te), lambda i, j: (ip(i), jo(i, j))),
        scratch_shapes=[pltpu.VMEM((nj, tm, tn), BF16), pltpu.VMEM((ne, tm, te), F32),
                        pltpu.VMEM((tm, 128), F32), pltpu.VMEM((tm, 128), F32)],
        compiler_params=_compiler_params(("arbitrary", "arbitrary"), blocks, scratch,
                                         _nbytes((tm, D), BF16) + 6 * _nbytes((tm, tn), F32)),
        name="branch_mix",
    )(ys, p, p, p, p, consts["w_branch"], consts["w_out"], x2, mod3, consts["post_g"])


def _swap_halves(w):
    h = w.shape[-1] // 2
    return jnp.concatenate([w[..., h:], w[..., :h]], axis=-1)


def _relayout_w_in(w):
    rope_end = ROPE_OFF + QK_ROPE
    head_end = rope_end + HEAD_W - CQ_OFF
    rope = w[:, ROPE_OFF:rope_end]
    pad = jnp.zeros((w.shape[0], CQ_OFF - ROPE_OFF - 2 * QK_ROPE), w.dtype)
    head = jnp.concatenate([w[:, :ROPE_OFF], rope, _swap_halves(rope), pad, w[:, rope_end:head_end]], axis=1)
    return head.astype(BF16), w[:, head_end:].astype(BF16)


def _relayout_w_uq(w):
    w = w.reshape(w.shape[0], MLA_HEADS, QK_NOPE + QK_ROPE)
    rope = w[..., QK_NOPE:]
    return jnp.concatenate([w[..., :QK_NOPE], rope, _swap_halves(rope)], axis=-1).reshape(w.shape[0], -1).astype(BF16)


def _rope_table(n, rotate):
    half = QK_ROPE // 2
    if rotate:
        pos = np.arange(n)
        inv = ROPE_THETA ** (-np.arange(0, half, 2, dtype=np.float64) / half)
        ang = np.concatenate([(pos // GRID_W)[:, None] * inv, (pos % GRID_W)[:, None] * inv], axis=-1)
    else:
        ang = np.zeros((n, half))
    cos, sin = np.cos(ang), np.sin(ang)
    return jnp.asarray(np.concatenate([cos, cos, -sin, sin], axis=-1), F32)


def _channel_dft(bw):
    gw = bw // FOURIER_GROUPS
    a = np.arange(gw)
    ang = 2.0 * np.pi * ((a[:, None] * a[None, :]) % gw) / gw
    return jnp.asarray(np.cos(ang), F32).astype(BF16), jnp.asarray(np.sin(ang), F32).astype(BF16)


def kernel(x, c, ctx, c_ctx, w_mod, b_mod, pre_g, post_g, w_in, q_norm_g, kv_norm_g, w_uq, w_ukv,
           conv_w, conv_b, sgu_ln_g, sgu_ln_b, sgu_w, sgu_b, w_branch, w_out):
    B, n, D = x.shape
    nc = ctx.shape[1]
    depth = w_in.shape[0]
    bw = w_branch.shape[2]
    assert w_in.shape[2] == ROPE_OFF + QK_ROPE + (HEAD_W - CQ_OFF) + N_BRANCH_COLS * bw + N_BRANCH * D
    assert kv_norm_g.shape[1] == ROPE_OFF and q_norm_g.shape[1] == HEAD_W - CQ_OFF and B <= 4

    cc = jnp.zeros((8, D), F32).at[:B].set(c).at[B].set(c_ctx)
    mod = _modulation(cc, w_mod, b_mod).reshape(depth, 8, 3, D)

    cs_x, cs_c = _rope_table(n, True), _rope_table(nc, False)
    ccb, scb = _channel_dft(bw)
    x2 = x.reshape(B * n, D)
    c2 = ctx.reshape(B * nc, D)
    row_x = lambda r: r // n
    row_c = lambda r: B

    for l in range(depth):
        last = l == depth - 1
        w_head, w_rest = _relayout_w_in(w_in[l])
        w_q2 = _relayout_w_uq(w_uq[l])
        w_ukv_l = w_ukv[l].astype(BF16)
        consts = dict(
            ccb=ccb, scb=scb, conv_w=conv_w[l], conv_b=conv_b[l].reshape(1, bw),
            ln_g=sgu_ln_g[l].reshape(1, bw), ln_b=sgu_ln_b[l].reshape(1, bw),
            w_s=sgu_w[l].astype(BF16),
            b_s=jnp.broadcast_to(sgu_b[l][:, :, None], sgu_b.shape[1:] + (bw // sgu_w.shape[1],)),
            w_branch=w_branch[l].astype(BF16), w_out=w_out[l].astype(BF16), post_g=post_g[l].reshape(1, D))

        hx, px = _in_projection(x2, mod[l], row_x, pre_g[l], w_head, w_rest, min(1024, n))
        hc, *pc = _in_projection(c2, mod[l], row_c, pre_g[l], w_head, None if last else w_rest,
                                 min(1024, B * nc))
        q_c, k_c, v_c = _mla_prep(hc, cs_c, kv_norm_g[l], q_norm_g[l], w_ukv_l, w_q2, B, nc)
        q_x, k_x, v_x = _mla_prep(hx, cs_x, kv_norm_g[l], q_norm_g[l], w_ukv_l, w_q2, B, n)

        attn_x = _attention(q_x, k_c, v_c, k_x, v_x).reshape(B * n, -1)
        four_x = _position_dft(px, 1, bw, B, n)
        new_x = _mix(px, attn_x, four_x, x2, mod[l], row_x, consts, n, n)
        if not last:
            attn_c = _attention(q_c, k_c, v_c).reshape(B * nc, -1)
            four_c = _position_dft(pc[0], 1, bw, B, nc)
            c2 = _mix(pc[0], attn_c, four_c, c2, mod[l], row_c, consts, nc, B * nc)
        x2 = new_x
    return x2.reshape(B, n, D)
```

```python
import functools

import jax
import jax.numpy as jnp
import numpy as np
from jax import lax
from jax.experimental import pallas as pl
from jax.experimental.pallas import tpu as pltpu

F32 = jnp.float32
BF16 = jnp.bfloat16

MLA_HEADS = 4
QK_NOPE = 128
QK_ROPE = 64
V_HEAD = 128
GRID_W = 64
ROPE_THETA = 10000.0
FOURIER_GROUPS = 4
CHUNK = 128
N_BRANCH = 4
N_BRANCH_COLS = 10
EPS = 1e-6
LOG2_E = 1.4426950408889634

ROPE_OFF = 256
CQ_OFF = 512
HEAD_W = 1024
QK_W = QK_NOPE + 2 * QK_ROPE

FFT_N1 = 16
FFT_STEP_CHUNKS = 4

V7X_VMEM_BYTES = 64 * 2**20
VMEM_CAP_BYTES = V7X_VMEM_BYTES - 8 * 2**20


def _nbytes(shape, dtype):
    return int(np.prod(shape)) * jnp.dtype(dtype).itemsize


def _compiler_params(semantics, block_bytes, scratch_bytes=0, temp_bytes=0):
    need = 2 * block_bytes + scratch_bytes + temp_bytes + 2 * 2**20
    return pltpu.CompilerParams(dimension_semantics=semantics,
                                vmem_limit_bytes=int(min(max(need, 16 * 2**20), VMEM_CAP_BYTES)))


def _sigmoid(v):
    return 0.5 * jnp.tanh(0.5 * v) + 0.5


def _silu(v):
    return v * _sigmoid(v)


def _rms(v, g):
    return v * lax.rsqrt(jnp.mean(v * v, axis=-1, keepdims=True) + EPS) * g


def _dot(a, b):
    return jnp.dot(a, b, preferred_element_type=F32)


def _mod_kernel(c_ref, w_ref, b_ref, o_ref):
    @pl.when(pl.program_id(1) == 0)
    def _():
        o_ref[0] = jnp.broadcast_to(b_ref[0], o_ref.shape[1:])

    s = _silu(c_ref[...]).astype(BF16)
    o_ref[0] += _dot(s, w_ref[0].astype(BF16))


def _modulation(cc, w_mod, b_mod):
    L, D, W = w_mod.shape
    tk = 256 if D % 256 == 0 else D
    blocks = _nbytes((tk, W), F32) + _nbytes((8, tk), F32) + 2 * _nbytes((8, W), F32)
    return pl.pallas_call(
        _mod_kernel,
        out_shape=jax.ShapeDtypeStruct((L, 8, W), F32),
        grid=(L, D // tk),
        in_specs=[pl.BlockSpec((8, tk), lambda l, k: (0, k)),
                  pl.BlockSpec((1, tk, W), lambda l, k: (l, k, 0)),
                  pl.BlockSpec((1, 1, W), lambda l, k: (l, 0, 0))],
        out_specs=pl.BlockSpec((1, 8, W), lambda l, k: (l, 0, 0)),
        compiler_params=_compiler_params(("parallel", "arbitrary"), blocks, 0, _nbytes((tk, W), BF16)),
        name="modulation",
    )(cc, w_mod, b_mod.reshape(L, 1, W))


def _inproj_kernel(*refs, with_rest):
    if with_rest:
        x_ref, mod_ref, g_ref, wh_ref, wr_ref, oh_ref, or_ref, h_ref = refs
    else:
        x_ref, mod_ref, g_ref, wh_ref, oh_ref, h_ref = refs
    j = pl.program_id(1)

    @pl.when(j == 0)
    def _():
        y = _rms(x_ref[...], g_ref[...])
        h_ref[...] = (y * (1.0 + mod_ref[0, 1:2, :]) + mod_ref[0, 0:1, :]).astype(BF16)
        oh_ref[...] = _dot(h_ref[...], wh_ref[...]).astype(oh_ref.dtype)

    if with_rest:
        @pl.when(j > 0)
        def _():
            or_ref[...] = _dot(h_ref[...], wr_ref[...]).astype(or_ref.dtype)


def _in_projection(x2, mod3, mod_row, pre_g, w_head, w_rest, tm):
    R, D = x2.shape
    tn = w_head.shape[1]
    with_rest = w_rest is not None
    nj = 1 + (w_rest.shape[1] // tn if with_rest else 0)
    rest_col = lambda j: jnp.maximum(j - 1, 0)
    in_specs = [pl.BlockSpec((tm, D), lambda i, j: (i, 0)),
                pl.BlockSpec((1, 3, D), lambda i, j: (mod_row(i * tm), 0, 0)),
                pl.BlockSpec((1, D), lambda i, j: (0, 0)),
                pl.BlockSpec((D, tn), lambda i, j: (0, 0))]
    out_specs = [pl.BlockSpec((tm, tn), lambda i, j: (i, 0))]
    out_shape = [jax.ShapeDtypeStruct((R, tn), BF16)]
    args = [x2, mod3, pre_g.reshape(1, D), w_head]
    if with_rest:
        in_specs.append(pl.BlockSpec((D, tn), lambda i, j: (0, rest_col(j))))
        out_specs.append(pl.BlockSpec((tm, tn), lambda i, j: (i, rest_col(j))))
        out_shape.append(jax.ShapeDtypeStruct((R, w_rest.shape[1]), BF16))
        args.append(w_rest)
    blocks = (_nbytes((tm, D), F32) + _nbytes((3, D), F32) + 2 * _nbytes((D, tn), BF16)
              + 2 * _nbytes((tm, tn), BF16))
    return pl.pallas_call(
        functools.partial(_inproj_kernel, with_rest=with_rest),
        out_shape=out_shape,
        grid=(R // tm, nj),
        in_specs=in_specs,
        out_specs=out_specs,
        scratch_shapes=[pltpu.VMEM((tm, D), BF16)],
        compiler_params=_compiler_params(("parallel", "arbitrary"), blocks, _nbytes((tm, D), BF16),
                                         2 * _nbytes((tm, D), F32) + _nbytes((tm, tn), F32)),
        name="in_projection",
    )(*args)


def _rope_pair(group, cs):
    r = group * cs
    return r + pltpu.roll(r, QK_ROPE, axis=1)


def _prep_kernel(p_ref, cs_ref, kvg_ref, qg_ref, wkv_ref, wq_ref, q_ref, k_ref, v_ref, *, scale):
    cs = cs_ref[...]
    kvn = _rms(p_ref[:, 0:ROPE_OFF].astype(F32), kvg_ref[...]).astype(BF16)
    kv = _dot(kvn, wkv_ref[...])
    kr = _rope_pair(p_ref[:, ROPE_OFF:ROPE_OFF + 2 * QK_ROPE].astype(F32), cs)
    lane = lax.broadcasted_iota(jnp.int32, kr.shape, 1)
    kr = jnp.where(lane < QK_ROPE, kr, 0.0).astype(BF16)
    cqn = _rms(p_ref[:, CQ_OFF:HEAD_W].astype(F32), qg_ref[...]).astype(BF16)
    q = _dot(cqn, wq_ref[...]) * scale
    kvw = QK_NOPE + V_HEAD
    for h in range(MLA_HEADS):
        k_ref[0, h, :, 0:QK_NOPE] = kv[:, h * kvw:h * kvw + QK_NOPE].astype(BF16)
        k_ref[0, h, :, QK_NOPE:QK_W] = kr
        v_ref[0, h, :, 0:V_HEAD] = kv[:, h * kvw + QK_NOPE:(h + 1) * kvw].astype(BF16)
        v_ref[0, h, :, V_HEAD:2 * V_HEAD] = jnp.ones((kv.shape[0], V_HEAD), BF16)
        q_ref[0, h, :, 0:QK_NOPE] = q[:, h * QK_W:h * QK_W + QK_NOPE].astype(BF16)
        q_ref[0, h, :, QK_NOPE:QK_W] = _rope_pair(q[:, h * QK_W + QK_NOPE:(h + 1) * QK_W], cs).astype(BF16)


def _mla_prep(p, cs, kv_norm_g, q_norm_g, w_ukv, w_q2, B, n):
    tm = min(512, n)
    nt = n // tm
    H = MLA_HEADS
    kv_lora, q_lora = w_ukv.shape[0], w_q2.shape[0]
    blocks = (_nbytes((tm, HEAD_W), BF16) + _nbytes((tm, 128), F32) + _nbytes(w_ukv.shape, BF16)
              + _nbytes(w_q2.shape, BF16) + 2 * _nbytes((H, tm, QK_W), BF16) + _nbytes((H, tm, 2 * V_HEAD), BF16))
    qk_spec = pl.BlockSpec((1, H, tm, QK_W), lambda b, i: (b, 0, i, 0))
    return pl.pallas_call(
        functools.partial(_prep_kernel, scale=float(QK_NOPE + QK_ROPE) ** -0.5 * LOG2_E),
        out_shape=(jax.ShapeDtypeStruct((B, H, n, QK_W), BF16),
                   jax.ShapeDtypeStruct((B, H, n, QK_W), BF16),
                   jax.ShapeDtypeStruct((B, H, n, 2 * V_HEAD), BF16)),
        grid=(B, nt),
        in_specs=[pl.BlockSpec((tm, HEAD_W), lambda b, i: (b * nt + i, 0)),
                  pl.BlockSpec((tm, 128), lambda b, i: (i, 0)),
                  pl.BlockSpec((1, kv_lora), lambda b, i: (0, 0)),
                  pl.BlockSpec((1, q_lora), lambda b, i: (0, 0)),
                  pl.BlockSpec(w_ukv.shape, lambda b, i: (0, 0)),
                  pl.BlockSpec(w_q2.shape, lambda b, i: (0, 0))],
        out_specs=(qk_spec, qk_spec, pl.BlockSpec((1, H, tm, 2 * V_HEAD), lambda b, i: (b, 0, i, 0))),
        compiler_params=_compiler_params(("parallel", "parallel"), blocks, 0, 6 * _nbytes((tm, HEAD_W), F32)),
        name="mla_prep",
    )(p, cs, kv_norm_g.reshape(1, -1), q_norm_g.reshape(1, -1), w_ukv, w_q2)


_NT = (((1,), (1,)), ((), ()))


def _attn_kernel(*refs, with_x, tk):
    if with_x:
        q_ref, kc_ref, vc_ref, kx_ref, vx_ref, o_ref = refs
    else:
        q_ref, kc_ref, vc_ref, o_ref = refs
    q = q_ref[0, 0]

    s = lax.dot_general(q, kc_ref[0, 0], _NT, preferred_element_type=F32)
    m = jnp.max(s, axis=-1, keepdims=True)
    acc = _dot(jnp.exp2(s - m).astype(BF16), vc_ref[0, 0])
    if with_x:
        for c in range(kx_ref.shape[2] // tk):
            s = lax.dot_general(q, kx_ref[0, 0, c * tk:(c + 1) * tk, :], _NT, preferred_element_type=F32)
            m_new = jnp.maximum(m, jnp.max(s, axis=-1, keepdims=True))
            p = jnp.exp2(s - m_new).astype(BF16)
            acc = acc * jnp.exp2(m - m_new) + _dot(p, vx_ref[0, 0, c * tk:(c + 1) * tk, :])
            m = m_new
    o_ref[0] = (acc[:, :V_HEAD] / acc[:, V_HEAD:]).astype(o_ref.dtype)


def _attention(q, kc, vc, kx=None, vx=None):
    B, H, nq, _ = q.shape
    nc = kc.shape[2]
    with_x = kx is not None
    nk = nc + (kx.shape[2] if with_x else 0)
    tq = min(1024, nq)
    tk = min(1024, nq)
    q_spec = pl.BlockSpec((1, 1, tq, QK_W), lambda b, h, i: (b, h, i, 0))

    def full(arr):
        return pl.BlockSpec((1, 1) + arr.shape[2:], lambda b, h, i: (b, h, 0, 0))

    args = [q, kc, vc] + ([kx, vx] if with_x else [])
    blocks = _nbytes((tq, QK_W), BF16) + _nbytes((nk, QK_W + 2 * V_HEAD), BF16) + _nbytes((tq, V_HEAD), BF16)
    return pl.pallas_call(
        functools.partial(_attn_kernel, with_x=with_x, tk=tk),
        out_shape=jax.ShapeDtypeStruct((B, nq, H * V_HEAD), BF16),
        grid=(B, H, nq // tq),
        in_specs=[q_spec] + [full(a) for a in args[1:]],
        out_specs=pl.BlockSpec((1, tq, V_HEAD), lambda b, h, i: (b, i, h)),
        compiler_params=_compiler_params(("parallel", "parallel", "parallel"), blocks, 0,
                                         6 * _nbytes((tq, tk), F32)),
        name="attention",
    )(*args)


def _fft1_kernel(u_ref, m_ref, tc_ref, ts_ref, o_ref, *, reps):
    rows = FFT_N1 * FFT_N1
    shape3 = (FFT_N1, FFT_N1, u_ref.shape[-1])
    for r in range(u_ref.shape[2]):
        u = u_ref[0, :, r].reshape(rows, u_ref.shape[-1])
        g = _dot(m_ref[...], u)
        gr, gi = g[:rows], g[rows:]
        tc = jnp.concatenate([tc_ref[:, r].reshape(rows, 128)] * reps, axis=1)
        ts = jnp.concatenate([ts_ref[:, r].reshape(rows, 128)] * reps, axis=1)
        o_ref[0, 0, :, r] = (gr * tc + gi * ts).reshape(shape3).astype(o_ref.dtype)
        o_ref[0, 1, :, r] = (gi * tc - gr * ts).reshape(shape3).astype(o_ref.dtype)


def _fft2_kernel(g_ref, fa_ref, fb_ref, o_ref, *, n2):
    for d in range(g_ref.shape[2]):
        y = _dot(fa_ref[...], g_ref[0, 0, d]) + _dot(fb_ref[...], g_ref[0, 1, d])
        o_ref[0, 0, d] = y[:n2].astype(o_ref.dtype)
        o_ref[0, 1, d] = y[n2:].astype(o_ref.dtype)


def _dft_tables(n):
    n1 = FFT_N1
    n2 = n // n1
    a = np.arange(n1)
    ang1 = 2.0 * np.pi * ((a[:, None] * a[None, :]) % n1) / n1
    eye = np.eye(n1)
    m1 = np.concatenate([np.kron(np.cos(ang1), eye), -np.kron(np.sin(ang1), eye)], axis=0)
    b = np.arange(n2)
    angt = 2.0 * np.pi * (a[:, None] * b[None, :]) / n
    shape4 = (n1, n2 // n1, n1, 128)
    tc = np.broadcast_to(np.cos(angt)[:, :, None], (n1, n2, 128)).reshape(shape4)
    ts = np.broadcast_to(np.sin(angt)[:, :, None], (n1, n2, 128)).reshape(shape4)
    ang2 = 2.0 * np.pi * ((b[:, None] * b[None, :]) % n2) / n2
    c2, s2 = np.cos(ang2), np.sin(ang2)
    fa = np.concatenate([c2, -s2], axis=0)
    fb = np.concatenate([s2, c2], axis=0)
    to_bf16 = lambda t: jnp.asarray(t, F32).astype(BF16)
    return to_bf16(m1), jnp.asarray(tc, F32), jnp.asarray(ts, F32), to_bf16(fa), to_bf16(fb)


def _position_dft(p, col_block, C, B, n):
    n1 = FFT_N1
    n2 = n // n1
    nb = n2 // n1
    m1, tc, ts, fa, fb = _dft_tables(n)
    p5 = p.reshape(B, n1, nb, n1, p.shape[1])
    rb = min(FFT_STEP_CHUNKS, nb)
    blocks = (3 * rb * _nbytes((n1 * n1, C), BF16) + _nbytes(m1.shape, BF16)
              + 2 * rb * _nbytes((n1 * n1, 128), F32))
    g = pl.pallas_call(
        functools.partial(_fft1_kernel, reps=C // 128),
        out_shape=jax.ShapeDtypeStruct((B, 2, n1, nb, n1, C), BF16),
        grid=(B, nb // rb),
        in_specs=[pl.BlockSpec((1, n1, rb, n1, C), lambda b, r: (b, 0, r, 0, col_block)),
                  pl.BlockSpec(m1.shape, lambda b, r: (0, 0)),
                  pl.BlockSpec((n1, rb, n1, 128), lambda b, r: (0, r, 0, 0)),
                  pl.BlockSpec((n1, rb, n1, 128), lambda b, r: (0, r, 0, 0))],
        out_specs=pl.BlockSpec((1, 2, n1, rb, n1, C), lambda b, r: (b, 0, 0, r, 0, 0)),
        compiler_params=_compiler_params(("parallel", "parallel"), blocks, 0, 8 * _nbytes((n1 * n1, C), F32)),
        name="position_dft_stage1",
    )(p5, m1, tc, ts)
    g = g.reshape(B, 2, n1, n2, C)
    db = min(FFT_STEP_CHUNKS, n1)
    blocks = 4 * db * _nbytes((n2, C), BF16) + 2 * _nbytes(fa.shape, BF16)
    return pl.pallas_call(
        functools.partial(_fft2_kernel, n2=n2),
        out_shape=jax.ShapeDtypeStruct((B, 2, n1, n2, C), BF16),
        grid=(B, n1 // db),
        in_specs=[pl.BlockSpec((1, 2, db, n2, C), lambda b, d: (b, 0, d, 0, 0)),
                  pl.BlockSpec(fa.shape, lambda b, d: (0, 0)),
                  pl.BlockSpec(fb.shape, lambda b, d: (0, 0))],
        out_specs=pl.BlockSpec((1, 2, db, n2, C), lambda b, d: (b, 0, d, 0, 0)),
        compiler_params=_compiler_params(("parallel", "parallel"), blocks, 0, 4 * _nbytes((n2, C), F32)),
        name="position_dft_stage2",
    )(g, fa, fb)


def _branch_kernel(attn_ref, vr_ref, vi_ref, zA_ref, zB_ref, xC_ref, bC_ref, cC_ref, zC_ref, uD_ref, vD_ref,
                   zD_ref, xCp_ref, cCp_ref, xCn_ref, cCn_ref, ccb_ref, scb_ref, convw_ref, convb_ref, lng_ref,
                   lnb_ref, ws_ref, bsb_ref, ys_ref, fr_ref, fi_ref, *, n, tm, halo, fscale):
    i = pl.program_id(0)

    def gate(ref):
        return _silu(ref[...].astype(F32))

    ys_ref[0] = (attn_ref[...].astype(F32) * gate(zA_ref)).astype(BF16)
    gwf = ccb_ref.shape[0]
    for d in range(FFT_N1):
        rows = pl.ds(d, tm // FFT_N1, stride=FFT_N1)
        vr, vi = vr_ref[0, 0, d].astype(F32), vi_ref[0, 0, d].astype(F32)
        for g in range(fr_ref.shape[0]):
            fr_ref[g, rows, :] = vr[:, g * gwf:(g + 1) * gwf]
            fi_ref[g, rows, :] = vi[:, g * gwf:(g + 1) * gwf]
    four = jnp.concatenate(
        [_dot(fr_ref[g].astype(BF16), ccb_ref[...]) + _dot(fi_ref[g].astype(BF16), scb_ref[...])
         for g in range(fr_ref.shape[0])], axis=1)
    ys_ref[1] = (four * fscale * gate(zB_ref)).astype(BF16)
    u = cC_ref[...].astype(F32) * xC_ref[...].astype(F32)
    first = (i * tm) % n == 0
    last = ((i + 1) * tm) % n == 0
    up = (cCp_ref[...].astype(F32) * xCp_ref[...].astype(F32))[halo - 1:halo, :]
    un = (cCn_ref[...].astype(F32) * xCn_ref[...].astype(F32))[0:1, :]
    up = up * jnp.where(first, 0.0, 1.0)
    un = un * jnp.where(last, 0.0, 1.0)
    row = lax.broadcasted_iota(jnp.int32, u.shape, 0)
    prev = jnp.where(row == 0, up, pltpu.roll(u, 1, axis=0))
    nxt = jnp.where(row == tm - 1, un, pltpu.roll(u, tm - 1, axis=0))
    conv = prev * convw_ref[0:1, :] + u * convw_ref[1:2, :] + nxt * convw_ref[2:3, :] + convb_ref[...]
    ys_ref[2] = (bC_ref[...].astype(F32) * conv * gate(zC_ref)).astype(BF16)
    v = vD_ref[...].astype(F32)
    vc = v - jnp.mean(v, axis=-1, keepdims=True)
    vn = vc * lax.rsqrt(jnp.mean(vc * vc, axis=-1, keepdims=True) + EPS) * lng_ref[...] + lnb_ref[...]
    vn = vn.astype(BF16)
    ug = uD_ref[...].astype(F32) * gate(zD_ref)
    gw = vn.shape[1] // ws_ref.shape[0]
    for g in range(ws_ref.shape[0]):
        for c in range(tm // CHUNK):
            rs, cs = slice(c * CHUNK, (c + 1) * CHUNK), slice(g * gw, (g + 1) * gw)
            mixed = _dot(ws_ref[g], vn[rs, cs]) + bsb_ref[g]
            ys_ref[3, rs, cs] = (ug[rs, cs] * mixed).astype(BF16)


def _mix_kernel(ys_ref, g0_ref, g1_ref, g2_ref, g3_ref, wb_ref, wo_ref, x_ref, mod_ref, postg_ref, o_ref,
                merged_ref, acc_ref, ss_ref, ssprev_ref, *, nj, ne, ni):
    i = pl.program_id(0)
    j = pl.program_id(1)

    @pl.when((j < nj) & (i < ni))
    def _():
        merged = None
        for g, g_ref in enumerate((g0_ref, g1_ref, g2_ref, g3_ref)):
            term = _sigmoid(g_ref[...].astype(F32)) * _dot(ys_ref[g], wb_ref[g])
            merged = term if merged is None else merged + term
        merged_ref[j] = merged.astype(BF16)

    @pl.when((j == nj) & (i > 0))
    def _():
        ssprev_ref[...] = ss_ref[...]

    @pl.when((j >= nj) & (i > 0))
    def _():
        width = acc_ref.shape[0] * acc_ref.shape[2]
        y = acc_ref[j - nj] * lax.rsqrt(ssprev_ref[:, 0:1] / width + EPS) * postg_ref[...]
        o_ref[...] = x_ref[...] + mod_ref[0, 2:3, :] * y

    @pl.when((j >= nj) & (i < ni))
    def _():
        merged = jnp.concatenate([merged_ref[c] for c in range(nj)], axis=1)
        part = _dot(merged, wo_ref[...])
        acc_ref[j - nj] = part
        ss = jnp.broadcast_to(jnp.sum(part * part, axis=-1, keepdims=True), ss_ref.shape)

        @pl.when(j == nj)
        def _():
            ss_ref[...] = ss

        @pl.when(j > nj)
        def _():
            ss_ref[...] += ss


def _mix(p, attn, v, x2, mod3, mod_row, consts, n, tile_rows):
    R, D = x2.shape
    bw = attn.shape[1]

    tb = min(512, n)
    halo = 16
    nt = n // tb
    cl = tb // FFT_N1

    def seg(k):
        return pl.BlockSpec((tb, bw), lambda i: (i, k))

    def halo_prev(k):
        return pl.BlockSpec((halo, bw), lambda i: (jnp.maximum(i * (tb // halo) - 1, 0), k))

    def halo_next(k):
        return pl.BlockSpec((halo, bw), lambda i: (jnp.minimum((i + 1) * (tb // halo), R // halo - 1), k))

    def const(arr):
        nd = arr.ndim
        return pl.BlockSpec(arr.shape, lambda i: (0,) * nd)

    branch_consts = [consts[k] for k in ("ccb", "scb", "conv_w", "conv_b", "ln_g", "ln_b", "w_s", "b_s")]
    gwf = bw // FOURIER_GROUPS
    blocks = (12 * _nbytes((tb, bw), BF16) + 4 * _nbytes((halo, bw), BF16) + N_BRANCH * _nbytes((tb, bw), BF16)
              + sum(_nbytes(a.shape, a.dtype) for a in branch_consts))
    ys = pl.pallas_call(
        functools.partial(_branch_kernel, n=n, tm=tb, halo=halo, fscale=float(n * gwf) ** -0.5),
        out_shape=jax.ShapeDtypeStruct((N_BRANCH, R, bw), BF16),
        grid=(R // tb,),
        in_specs=([pl.BlockSpec((tb, bw), lambda i: (i, 0)),
                   pl.BlockSpec((1, 1, FFT_N1, cl, bw), lambda i: (i // nt, 0, 0, i % nt, 0)),
                   pl.BlockSpec((1, 1, FFT_N1, cl, bw), lambda i: (i // nt, 1, 0, i % nt, 0))]
                  + [seg(k) for k in (0, 2, 3, 4, 5, 6, 7, 8, 9)]
                  + [halo_prev(3), halo_prev(5), halo_next(3), halo_next(5)]
                  + [const(a) for a in branch_consts]),
        out_specs=pl.BlockSpec((N_BRANCH, tb, bw), lambda i: (0, i, 0)),
        scratch_shapes=[pltpu.VMEM((FOURIER_GROUPS, tb, gwf), F32), pltpu.VMEM((FOURIER_GROUPS, tb, gwf), F32)],
        compiler_params=_compiler_params(("parallel",), blocks, 2 * _nbytes((tb, bw), F32),
                                         12 * _nbytes((tb, bw), F32)),
        name="branch_outputs",
    )(attn, v, v, *([p] * 9), *([p] * 4), *branch_consts)

    tm = min(1024, tile_rows)
    tn = min(512, D)
    te = min(512, D)
    nj, ne = D // tn, D // te
    ni = R // tm
    gate0 = N_BRANCH_COLS * bw // tn
    ia = lambda i, j: jnp.minimum(jnp.where(j < nj, i, i + 1), ni - 1)
    ja = lambda j: jnp.where(j < nj, j, 0)
    jb = lambda j: jnp.maximum(j - nj, 0)
    ip = lambda i: jnp.maximum(i - 1, 0)
    jo = lambda i, j: jnp.where(i == 0, 0, jb(j))

    def gates(g):
        return pl.BlockSpec((tm, tn), lambda i, j: (ia(i, j), gate0 + g * nj + ja(j)))

    blocks = (_nbytes((N_BRANCH, tm, bw), BF16) + N_BRANCH * _nbytes((tm, tn), BF16)
              + _nbytes((N_BRANCH, bw, tn), BF16) + _nbytes((D, te), BF16) + 2 * _nbytes((tm, te), F32))
    scratch = _nbytes((tm, D), BF16) + _nbytes((tm, D), F32) + 2 * _nbytes((tm, 128), F32)
    return pl.pallas_call(
        functools.partial(_mix_kernel, nj=nj, ne=ne, ni=ni),
        out_shape=jax.ShapeDtypeStruct((R, D), F32),
        grid=(ni + 1, nj + ne),
        in_specs=([pl.BlockSpec((N_BRANCH, tm, bw), lambda i, j: (0, ia(i, j), 0))]
                  + [gates(g) for g in range(N_BRANCH)]
                  + [pl.BlockSpec((N_BRANCH, bw, tn), lambda i, j: (0, 0, ja(j))),
                     pl.BlockSpec((D, te), lambda i, j: (0, jb(j))),
                     pl.BlockSpec((tm, te), lambda i, j: (ip(i), jb(j))),
                     pl.BlockSpec((1, 3, te), lambda i, j: (mod_row(ip(i) * tm), 0, jb(j))),
                     pl.BlockSpec((1, te), lambda i, j: (0, jb(j)))]),
        out_specs=pl.BlockSpec((tm, te), lambda i, j: (ip(i), jo(i, j))),
        scratch_shapes=[pltpu.VMEM((nj, tm, tn), BF16), pltpu.VMEM((ne, tm, te), F32),
                        pltpu.VMEM((tm, 128), F32), pltpu.VMEM((tm, 128), F32)],
        compiler_params=_compiler_params(("arbitrary", "arbitrary"), blocks, scratch,
                                         _nbytes((tm, D), BF16) + 6 * _nbytes((tm, tn), F32)),
        name="branch_mix",
    )(ys, p, p, p, p, consts["w_branch"], consts["w_out"], x2, mod3, consts["post_g"])


def _swap_halves(w):
    h = w.shape[-1] // 2
    return jnp.concatenate([w[..., h:], w[..., :h]], axis=-1)


def _relayout_w_in(w):
    rope_end = ROPE_OFF + QK_ROPE
    head_end = rope_end + HEAD_W - CQ_OFF
    rope = w[:, ROPE_OFF:rope_end]
    pad = jnp.zeros((w.shape[0], CQ_OFF - ROPE_OFF - 2 * QK_ROPE), w.dtype)
    head = jnp.concatenate([w[:, :ROPE_OFF], rope, _swap_halves(rope), pad, w[:, rope_end:head_end]], axis=1)
    return head.astype(BF16), w[:, head_end:].astype(BF16)


def _relayout_w_uq(w):
    w = w.reshape(w.shape[0], MLA_HEADS, QK_NOPE + QK_ROPE)
    rope = w[..., QK_NOPE:]
    return jnp.concatenate([w[..., :QK_NOPE], rope, _swap_halves(rope)], axis=-1).reshape(w.shape[0], -1).astype(BF16)


def _rope_table(n, rotate):
    half = QK_ROPE // 2
    if rotate:
        pos = np.arange(n)
        inv = ROPE_THETA ** (-np.arange(0, half, 2, dtype=np.float64) / half)
        ang = np.concatenate([(pos // GRID_W)[:, None] * inv, (pos % GRID_W)[:, None] * inv], axis=-1)
    else:
        ang = np.zeros((n, half))
    cos, sin = np.cos(ang), np.sin(ang)
    return jnp.asarray(np.concatenate([cos, cos, -sin, sin], axis=-1), F32)


def _channel_dft(bw):
    gw = bw // FOURIER_GROUPS
    a = np.arange(gw)
    ang = 2.0 * np.pi * ((a[:, None] * a[None, :]) % gw) / gw
    return jnp.asarray(np.cos(ang), F32).astype(BF16), jnp.asarray(np.sin(ang), F32).astype(BF16)


def kernel(x, c, ctx, c_ctx, w_mod, b_mod, pre_g, post_g, w_in, q_norm_g, kv_norm_g, w_uq, w_ukv,
           conv_w, conv_b, sgu_ln_g, sgu_ln_b, sgu_w, sgu_b, w_branch, w_out):
    B, n, D = x.shape
    nc = ctx.shape[1]
    depth = w_in.shape[0]
    bw = w_branch.shape[2]
    assert w_in.shape[2] == ROPE_OFF + QK_ROPE + (HEAD_W - CQ_OFF) + N_BRANCH_COLS * bw + N_BRANCH * D
    assert kv_norm_g.shape[1] == ROPE_OFF and q_norm_g.shape[1] == HEAD_W - CQ_OFF and B <= 4

    cc = jnp.zeros((8, D), F32).at[:B].set(c).at[B].set(c_ctx)
    mod = _modulation(cc, w_mod, b_mod).reshape(depth, 8, 3, D)

    cs_x, cs_c = _rope_table(n, True), _rope_table(nc, False)
    ccb, scb = _channel_dft(bw)
    x2 = x.reshape(B * n, D)
    c2 = ctx.reshape(B * nc, D)
    row_x = lambda r: r // n
    row_c = lambda r: B

    for l in range(depth):
        last = l == depth - 1
        w_head, w_rest = _relayout_w_in(w_in[l])
        w_q2 = _relayout_w_uq(w_uq[l])
        w_ukv_l = w_ukv[l].astype(BF16)
        consts = dict(
            ccb=ccb, scb=scb, conv_w=conv_w[l], conv_b=conv_b[l].reshape(1, bw),
            ln_g=sgu_ln_g[l].reshape(1, bw), ln_b=sgu_ln_b[l].reshape(1, bw),
            w_s=sgu_w[l].astype(BF16),
            b_s=jnp.broadcast_to(sgu_b[l][:, :, None], sgu_b.shape[1:] + (bw // sgu_w.shape[1],)),
            w_branch=w_branch[l].astype(BF16), w_out=w_out[l].astype(BF16), post_g=post_g[l].reshape(1, D))

        hx, px = _in_projection(x2, mod[l], row_x, pre_g[l], w_head, w_rest, min(1024, n))
        hc, *pc = _in_projection(c2, mod[l], row_c, pre_g[l], w_head, None if last else w_rest,
                                 min(1024, B * nc))
        q_c, k_c, v_c = _mla_prep(hc, cs_c, kv_norm_g[l], q_norm_g[l], w_ukv_l, w_q2, B, nc)
        q_x, k_x, v_x = _mla_prep(hx, cs_x, kv_norm_g[l], q_norm_g[l], w_ukv_l, w_q2, B, n)

        attn_x = _attention(q_x, k_c, v_c, k_x, v_x).reshape(B * n, -1)
        four_x = _position_dft(px, 1, bw, B, n)
        new_x = _mix(px, attn_x, four_x, x2, mod[l], row_x, consts, n, n)
        if not last:
            attn_c = _attention(q_c, k_c, v_c).reshape(B * nc, -1)
            four_c = _position_dft(pc[0], 1, bw, B, nc)
            c2 = _mix(pc[0], attn_c, four_c, c2, mod[l], row_c, consts, nc, B * nc)
        x2 = new_x
    return x2.reshape(B, n, D)
```

```python
import functools

import jax
import jax.numpy as jnp
import numpy as np
from jax import lax
from jax.experimental import pallas as pl
from jax.experimental.pallas import tpu as pltpu

F32 = jnp.float32
BF16 = jnp.bfloat16

MLA_HEADS = 4
QK_NOPE = 128
QK_ROPE = 64
V_HEAD = 128
GRID_W = 64
ROPE_THETA = 10000.0
FOURIER_GROUPS = 4
CHUNK = 128
N_BRANCH = 4
N_BRANCH_COLS = 10
EPS = 1e-6
LOG2_E = 1.4426950408889634

ROPE_OFF = 256
CQ_OFF = 512
HEAD_W = 1024
QK_W = QK_NOPE + 2 * QK_ROPE

FFT_N1 = 16
FFT_STEP_CHUNKS = 4

V7X_VMEM_BYTES = 64 * 2**20
VMEM_CAP_BYTES = V7X_VMEM_BYTES - 8 * 2**20


def _nbytes(shape, dtype):
    return int(np.prod(shape)) * jnp.dtype(dtype).itemsize


def _compiler_params(semantics, block_bytes, scratch_bytes=0, temp_bytes=0):
    need = 2 * block_bytes + scratch_bytes + temp_bytes + 2 * 2**20
    return pltpu.CompilerParams(dimension_semantics=semantics,
                                vmem_limit_bytes=int(min(max(need, 16 * 2**20), VMEM_CAP_BYTES)))


def _sigmoid(v):
    return 0.5 * jnp.tanh(0.5 * v) + 0.5


def _silu(v):
    return v * _sigmoid(v)


def _rms(v, g):
    return v * lax.rsqrt(jnp.mean(v * v, axis=-1, keepdims=True) + EPS) * g


def _dot(a, b):
    return jnp.dot(a, b, preferred_element_type=F32)


def _mod_kernel(c_ref, w_ref, b_ref, o_ref):
    @pl.when(pl.program_id(1) == 0)
    def _():
        o_ref[0] = jnp.broadcast_to(b_ref[0], o_ref.shape[1:])

    s = _silu(c_ref[...]).astype(BF16)
    o_ref[0] += _dot(s, w_ref[0].astype(BF16))


def _modulation(cc, w_mod, b_mod):
    L, D, W = w_mod.shape
    tk = 256 if D % 256 == 0 else D
    blocks = _nbytes((tk, W), F32) + _nbytes((8, tk), F32) + 2 * _nbytes((8, W), F32)
    return pl.pallas_call(
        _mod_kernel,
        out_shape=jax.ShapeDtypeStruct((L, 8, W), F32),
        grid=(L, D // tk),
        in_specs=[pl.BlockSpec((8, tk), lambda l, k: (0, k)),
                  pl.BlockSpec((1, tk, W), lambda l, k: (l, k, 0)),
                  pl.BlockSpec((1, 1, W), lambda l, k: (l, 0, 0))],
        out_specs=pl.BlockSpec((1, 8, W), lambda l, k: (l, 0, 0)),
        compiler_params=_compiler_params(("parallel", "arbitrary"), blocks, 0, _nbytes((tk, W), BF16)),
        name="modulation",
    )(cc, w_mod, b_mod.reshape(L, 1, W))


def _inproj_kernel(*refs, with_rest):
    if with_rest:
        x_ref, mod_ref, g_ref, wh_ref, wr_ref, oh_ref, or_ref, h_ref = refs
    else:
        x_ref, mod_ref, g_ref, wh_ref, oh_ref, h_ref = refs
    j = pl.program_id(1)

    @pl.when(j == 0)
    def _():
        y = _rms(x_ref[...], g_ref[...])
        h_ref[...] = (y * (1.0 + mod_ref[0, 1:2, :]) + mod_ref[0, 0:1, :]).astype(BF16)
        oh_ref[...] = _dot(h_ref[...], wh_ref[...]).astype(oh_ref.dtype)

    if with_rest:
        @pl.when(j > 0)
        def _():
            or_ref[...] = _dot(h_ref[...], wr_ref[...]).astype(or_ref.dtype)


def _in_projection(x2, mod3, mod_row, pre_g, w_head, w_rest, tm):
    R, D = x2.shape
    tn = w_head.shape[1]
    with_rest = w_rest is not None
    nj = 1 + (w_rest.shape[1] // tn if with_rest else 0)
    rest_col = lambda j: jnp.maximum(j - 1, 0)
    in_specs = [pl.BlockSpec((tm, D), lambda i, j: (i, 0)),
                pl.BlockSpec((1, 3, D), lambda i, j: (mod_row(i * tm), 0, 0)),
                pl.BlockSpec((1, D), lambda i, j: (0, 0)),
                pl.BlockSpec((D, tn), lambda i, j: (0, 0))]
    out_specs = [pl.BlockSpec((tm, tn), lambda i, j: (i, 0))]
    out_shape = [jax.ShapeDtypeStruct((R, tn), BF16)]
    args = [x2, mod3, pre_g.reshape(1, D), w_head]
    if with_rest:
        in_specs.append(pl.BlockSpec((D, tn), lambda i, j: (0, rest_col(j))))
        out_specs.append(pl.BlockSpec((tm, tn), lambda i, j: (i, rest_col(j))))
        out_shape.append(jax.ShapeDtypeStruct((R, w_rest.shape[1]), BF16))
        args.append(w_rest)
    blocks = (_nbytes((tm, D), F32) + _nbytes((3, D), F32) + 2 * _nbytes((D, tn), BF16)
              + 2 * _nbytes((tm, tn), BF16))
    return pl.pallas_call(
        functools.partial(_inproj_kernel, with_rest=with_rest),
        out_shape=out_shape,
        grid=(R // tm, nj),
        in_specs=in_specs,
        out_specs=out_specs,
        scratch_shapes=[pltpu.VMEM((tm, D), BF16)],
        compiler_params=_compiler_params(("parallel", "arbitrary"), blocks, _nbytes((tm, D), BF16),
                                         2 * _nbytes((tm, D), F32) + _nbytes((tm, tn), F32)),
        name="in_projection",
    )(*args)


def _rope_pair(group, cs):
    r = group * cs
    return r + pltpu.roll(r, QK_ROPE, axis=1)


def _prep_kernel(p_ref, cs_ref, kvg_ref, qg_ref, wkv_ref, wq_ref, q_ref, k_ref, v_ref, *, scale):
    cs = cs_ref[...]
    kvn = _rms(p_ref[:, 0:ROPE_OFF].astype(F32), kvg_ref[...]).astype(BF16)
    kv = _dot(kvn, wkv_ref[...])
    kr = _rope_pair(p_ref[:, ROPE_OFF:ROPE_OFF + 2 * QK_ROPE].astype(F32), cs)
    lane = lax.broadcasted_iota(jnp.int32, kr.shape, 1)
    kr = jnp.where(lane < QK_ROPE, kr, 0.0).astype(BF16)
    cqn = _rms(p_ref[:, CQ_OFF:HEAD_W].astype(F32), qg_ref[...]).astype(BF16)
    q = _dot(cqn, wq_ref[...]) * scale
    kvw = QK_NOPE + V_HEAD
    for h in range(MLA_HEADS):
        k_ref[0, h, :, 0:QK_NOPE] = kv[:, h * kvw:h * kvw + QK_NOPE].astype(BF16)
        k_ref[0, h, :, QK_NOPE:QK_W] = kr
        v_ref[0, h, :, 0:V_HEAD] = kv[:, h * kvw + QK_NOPE:(h + 1) * kvw].astype(BF16)
        v_ref[0, h, :, V_HEAD:2 * V_HEAD] = jnp.ones((kv.shape[0], V_HEAD), BF16)
        q_ref[0, h, :, 0:QK_NOPE] = q[:, h * QK_W:h * QK_W + QK_NOPE].astype(BF16)
        q_ref[0, h, :, QK_NOPE:QK_W] = _rope_pair(q[:, h * QK_W + QK_NOPE:(h + 1) * QK_W], cs).astype(BF16)


def _mla_prep(p, cs, kv_norm_g, q_norm_g, w_ukv, w_q2, B, n):
    tm = min(1024, n)
    nt = n // tm
    H = MLA_HEADS
    kv_lora, q_lora = w_ukv.shape[0], w_q2.shape[0]
    blocks = (_nbytes((tm, HEAD_W), BF16) + _nbytes((tm, 128), F32) + _nbytes(w_ukv.shape, BF16)
              + _nbytes(w_q2.shape, BF16) + 2 * _nbytes((H, tm, QK_W), BF16) + _nbytes((H, tm, 2 * V_HEAD), BF16))
    qk_spec = pl.BlockSpec((1, H, tm, QK_W), lambda b, i: (b, 0, i, 0))
    return pl.pallas_call(
        functools.partial(_prep_kernel, scale=float(QK_NOPE + QK_ROPE) ** -0.5 * LOG2_E),
        out_shape=(jax.ShapeDtypeStruct((B, H, n, QK_W), BF16),
                   jax.ShapeDtypeStruct((B, H, n, QK_W), BF16),
                   jax.ShapeDtypeStruct((B, H, n, 2 * V_HEAD), BF16)),
        grid=(B, nt),
        in_specs=[pl.BlockSpec((tm, HEAD_W), lambda b, i: (b * nt + i, 0)),
                  pl.BlockSpec((tm, 128), lambda b, i: (i, 0)),
                  pl.BlockSpec((1, kv_lora), lambda b, i: (0, 0)),
                  pl.BlockSpec((1, q_lora), lambda b, i: (0, 0)),
                  pl.BlockSpec(w_ukv.shape, lambda b, i: (0, 0)),
                  pl.BlockSpec(w_q2.shape, lambda b, i: (0, 0))],
        out_specs=(qk_spec, qk_spec, pl.BlockSpec((1, H, tm, 2 * V_HEAD), lambda b, i: (b, 0, i, 0))),
        compiler_params=_compiler_params(("parallel", "parallel"), blocks, 0, 6 * _nbytes((tm, HEAD_W), F32)),
        name="mla_prep",
    )(p, cs, kv_norm_g.reshape(1, -1), q_norm_g.reshape(1, -1), w_ukv, w_q2)


_NT = (((1,), (1,)), ((), ()))


def _attn_kernel(*refs, with_x, tk):
    if with_x:
        q_ref, kc_ref, vc_ref, kx_ref, vx_ref, o_ref = refs
    else:
        q_ref, kc_ref, vc_ref, o_ref = refs
    q = q_ref[0, 0]

    s = lax.dot_general(q, kc_ref[0, 0], _NT, preferred_element_type=F32)
    m = jnp.max(s, axis=-1, keepdims=True)
    acc = _dot(jnp.exp2(s - m).astype(BF16), vc_ref[0, 0])
    if with_x:
        for c in range(kx_ref.shape[2] // tk):
            s = lax.dot_general(q, kx_ref[0, 0, c * tk:(c + 1) * tk, :], _NT, preferred_element_type=F32)
            m_new = jnp.maximum(m, jnp.max(s, axis=-1, keepdims=True))
            p = jnp.exp2(s - m_new).astype(BF16)
            acc = acc * jnp.exp2(m - m_new) + _dot(p, vx_ref[0, 0, c * tk:(c + 1) * tk, :])
            m = m_new
    o_ref[0] = (acc[:, :V_HEAD] / acc[:, V_HEAD:]).astype(o_ref.dtype)


def _attention(q, kc, vc, kx=None, vx=None):
    B, H, nq, _ = q.shape
    nc = kc.shape[2]
    with_x = kx is not None
    nk = nc + (kx.shape[2] if with_x else 0)
    tq = min(1024, nq)
    tk = min(1024, nq)
    q_spec = pl.BlockSpec((1, 1, tq, QK_W), lambda b, h, i: (b, h, i, 0))

    def full(arr):
        return pl.BlockSpec((1, 1) + arr.shape[2:], lambda b, h, i: (b, h, 0, 0))

    args = [q, kc, vc] + ([kx, vx] if with_x else [])
    blocks = _nbytes((tq, QK_W), BF16) + _nbytes((nk, QK_W + 2 * V_HEAD), BF16) + _nbytes((tq, V_HEAD), BF16)
    return pl.pallas_call(
        functools.partial(_attn_kernel, with_x=with_x, tk=tk),
        out_shape=jax.ShapeDtypeStruct((B, nq, H * V_HEAD), BF16),
        grid=(B, H, nq // tq),
        in_specs=[q_spec] + [full(a) for a in args[1:]],
        out_specs=pl.BlockSpec((1, tq, V_HEAD), lambda b, h, i: (b, i, h)),
        compiler_params=_compiler_params(("parallel", "parallel", "parallel"), blocks, 0,
                                         6 * _nbytes((tq, tk), F32)),
        name="attention",
    )(*args)


def _fft1_kernel(u_ref, m_ref, tc_ref, ts_ref, o_ref, *, reps):
    rows = FFT_N1 * FFT_N1
    shape3 = (FFT_N1, FFT_N1, u_ref.shape[-1])
    for r in range(u_ref.shape[2]):
        u = u_ref[0, :, r].reshape(rows, u_ref.shape[-1])
        g = _dot(m_ref[...], u)
        gr, gi = g[:rows], g[rows:]
        tc = jnp.concatenate([tc_ref[:, r].reshape(rows, 128)] * reps, axis=1)
        ts = jnp.concatenate([ts_ref[:, r].reshape(rows, 128)] * reps, axis=1)
        o_ref[0, 0, :, r] = (gr * tc + gi * ts).reshape(shape3).astype(o_ref.dtype)
        o_ref[0, 1, :, r] = (gi * tc - gr * ts).reshape(shape3).astype(o_ref.dtype)


def _fft2_kernel(g_ref, fa_ref, fb_ref, o_ref, *, n2):
    for d in range(g_ref.shape[2]):
        y = _dot(fa_ref[...], g_ref[0, 0, d]) + _dot(fb_ref[...], g_ref[0, 1, d])
        o_ref[0, 0, d] = y[:n2].astype(o_ref.dtype)
        o_ref[0, 1, d] = y[n2:].astype(o_ref.dtype)


def _dft_tables(n):
    n1 = FFT_N1
    n2 = n // n1
    a = np.arange(n1)
    ang1 = 2.0 * np.pi * ((a[:, None] * a[None, :]) % n1) / n1
    eye = np.eye(n1)
    m1 = np.concatenate([np.kron(np.cos(ang1), eye), -np.kron(np.sin(ang1), eye)], axis=0)
    b = np.arange(n2)
    angt = 2.0 * np.pi * (a[:, None] * b[None, :]) / n
    shape4 = (n1, n2 // n1, n1, 128)
    tc = np.broadcast_to(np.cos(angt)[:, :, None], (n1, n2, 128)).reshape(shape4)
    ts = np.broadcast_to(np.sin(angt)[:, :, None], (n1, n2, 128)).reshape(shape4)
    ang2 = 2.0 * np.pi * ((b[:, None] * b[None, :]) % n2) / n2
    c2, s2 = np.cos(ang2), np.sin(ang2)
    fa = np.concatenate([c2, -s2], axis=0)
    fb = np.concatenate([s2, c2], axis=0)
    to_bf16 = lambda t: jnp.asarray(t, F32).astype(BF16)
    return to_bf16(m1), jnp.asarray(tc, F32), jnp.asarray(ts, F32), to_bf16(fa), to_bf16(fb)


def _position_dft(p, col_block, C, B, n):
    n1 = FFT_N1
    n2 = n // n1
    nb = n2 // n1
    m1, tc, ts, fa, fb = _dft_tables(n)
    p5 = p.reshape(B, n1, nb, n1, p.shape[1])
    rb = min(FFT_STEP_CHUNKS, nb)
    blocks = (3 * rb * _nbytes((n1 * n1, C), BF16) + _nbytes(m1.shape, BF16)
              + 2 * rb * _nbytes((n1 * n1, 128), F32))
    g = pl.pallas_call(
        functools.partial(_fft1_kernel, reps=C // 128),
        out_shape=jax.ShapeDtypeStruct((B, 2, n1, nb, n1, C), BF16),
        grid=(B, nb // rb),
        in_specs=[pl.BlockSpec((1, n1, rb, n1, C), lambda b, r: (b, 0, r, 0, col_block)),
                  pl.BlockSpec(m1.shape, lambda b, r: (0, 0)),
                  pl.BlockSpec((n1, rb, n1, 128), lambda b, r: (0, r, 0, 0)),
                  pl.BlockSpec((n1, rb, n1, 128), lambda b, r: (0, r, 0, 0))],
        out_specs=pl.BlockSpec((1, 2, n1, rb, n1, C), lambda b, r: (b, 0, 0, r, 0, 0)),
        compiler_params=_compiler_params(("parallel", "parallel"), blocks, 0, 8 * _nbytes((n1 * n1, C), F32)),
        name="position_dft_stage1",
    )(p5, m1, tc, ts)
    g = g.reshape(B, 2, n1, n2, C)
    db = min(FFT_STEP_CHUNKS, n1)
    blocks = 4 * db * _nbytes((n2, C), BF16) + 2 * _nbytes(fa.shape, BF16)
    return pl.pallas_call(
        functools.partial(_fft2_kernel, n2=n2),
        out_shape=jax.ShapeDtypeStruct((B, 2, n1, n2, C), BF16),
        grid=(B, n1 // db),
        in_specs=[pl.BlockSpec((1, 2, db, n2, C), lambda b, d: (b, 0, d, 0, 0)),
                  pl.BlockSpec(fa.shape, lambda b, d: (0, 0)),
                  pl.BlockSpec(fb.shape, lambda b, d: (0, 0))],
        out_specs=pl.BlockSpec((1, 2, db, n2, C), lambda b, d: (b, 0, d, 0, 0)),
        compiler_params=_compiler_params(("parallel", "parallel"), blocks, 0, 4 * _nbytes((n2, C), F32)),
        name="position_dft_stage2",
    )(g, fa, fb)


def _branch_kernel(attn_ref, vr_ref, vi_ref, zA_ref, zB_ref, xC_ref, bC_ref, cC_ref, zC_ref, uD_ref, vD_ref,
                   zD_ref, xCp_ref, cCp_ref, xCn_ref, cCn_ref, ccb_ref, scb_ref, convw_ref, convb_ref, lng_ref,
                   lnb_ref, ws_ref, bsb_ref, ys_ref, fr_ref, fi_ref, *, n, tm, halo, fscale):
    i = pl.program_id(0)

    def gate(ref):
        return _silu(ref[...].astype(F32))

    ys_ref[0] = (attn_ref[...].astype(F32) * gate(zA_ref)).astype(BF16)
    gwf = ccb_ref.shape[0]
    for d in range(FFT_N1):
        rows = pl.ds(d, tm // FFT_N1, stride=FFT_N1)
        vr, vi = vr_ref[0, 0, d].astype(F32), vi_ref[0, 0, d].astype(F32)
        for g in range(fr_ref.shape[0]):
            fr_ref[g, rows, :] = vr[:, g * gwf:(g + 1) * gwf]
            fi_ref[g, rows, :] = vi[:, g * gwf:(g + 1) * gwf]
    four = jnp.concatenate(
        [_dot(fr_ref[g].astype(BF16), ccb_ref[...]) + _dot(fi_ref[g].astype(BF16), scb_ref[...])
         for g in range(fr_ref.shape[0])], axis=1)
    ys_ref[1] = (four * fscale * gate(zB_ref)).astype(BF16)
    u = cC_ref[...].astype(F32) * xC_ref[...].astype(F32)
    first = (i * tm) % n == 0
    last = ((i + 1) * tm) % n == 0
    up = (cCp_ref[...].astype(F32) * xCp_ref[...].astype(F32))[halo - 1:halo, :]
    un = (cCn_ref[...].astype(F32) * xCn_ref[...].astype(F32))[0:1, :]
    up = up * jnp.where(first, 0.0, 1.0)
    un = un * jnp.where(last, 0.0, 1.0)
    row = lax.broadcasted_iota(jnp.int32, u.shape, 0)
    prev = jnp.where(row == 0, up, pltpu.roll(u, 1, axis=0))
    nxt = jnp.where(row == tm - 1, un, pltpu.roll(u, tm - 1, axis=0))
    conv = prev * convw_ref[0:1, :] + u * convw_ref[1:2, :] + nxt * convw_ref[2:3, :] + convb_ref[...]
    ys_ref[2] = (bC_ref[...].astype(F32) * conv * gate(zC_ref)).astype(BF16)
    v = vD_ref[...].astype(F32)
    vc = v - jnp.mean(v, axis=-1, keepdims=True)
    vn = vc * lax.rsqrt(jnp.mean(vc * vc, axis=-1, keepdims=True) + EPS) * lng_ref[...] + lnb_ref[...]
    vn = vn.astype(BF16)
    ug = uD_ref[...].astype(F32) * gate(zD_ref)
    gw = vn.shape[1] // ws_ref.shape[0]
    for g in range(ws_ref.shape[0]):
        for c in range(tm // CHUNK):
            rs, cs = slice(c * CHUNK, (c + 1) * CHUNK), slice(g * gw, (g + 1) * gw)
            mixed = _dot(ws_ref[g], vn[rs, cs]) + bsb_ref[g]
            ys_ref[3, rs, cs] = (ug[rs, cs] * mixed).astype(BF16)


def _mix_kernel(ys_ref, g0_ref, g1_ref, g2_ref, g3_ref, wb_ref, wo_ref, x_ref, mod_ref, postg_ref, o_ref,
                merged_ref, acc_ref, ss_ref, ssprev_ref, *, nj, ne, ni):
    i = pl.program_id(0)
    j = pl.program_id(1)

    @pl.when((j < nj) & (i < ni))
    def _():
        merged = None
        for g, g_ref in enumerate((g0_ref, g1_ref, g2_ref, g3_ref)):
            term = _sigmoid(g_ref[...].astype(F32)) * _dot(ys_ref[g], wb_ref[g])
            merged = term if merged is None else merged + term
        merged_ref[j] = merged.astype(BF16)

    @pl.when((j == nj) & (i > 0))
    def _():
        ssprev_ref[...] = ss_ref[...]

    @pl.when((j >= nj) & (i > 0))
    def _():
        width = acc_ref.shape[0] * acc_ref.shape[2]
        y = acc_ref[j - nj] * lax.rsqrt(ssprev_ref[:, 0:1] / width + EPS) * postg_ref[...]
        o_ref[...] = x_ref[...] + mod_ref[0, 2:3, :] * y

    @pl.when((j >= nj) & (i < ni))
    def _():
        merged = jnp.concatenate([merged_ref[c] for c in range(nj)], axis=1)
        part = _dot(merged, wo_ref[...])
        acc_ref[j - nj] = part
        ss = jnp.broadcast_to(jnp.sum(part * part, axis=-1, keepdims=True), ss_ref.shape)

        @pl.when(j == nj)
        def _():
            ss_ref[...] = ss

        @pl.when(j > nj)
        def _():
            ss_ref[...] += ss


def _mix(p, attn, v, x2, mod3, mod_row, consts, n, tile_rows):
    R, D = x2.shape
    bw = attn.shape[1]

    tb = min(512, n)
    halo = 16
    nt = n // tb
    cl = tb // FFT_N1

    def seg(k):
        return pl.BlockSpec((tb, bw), lambda i: (i, k))

    def halo_prev(k):
        return pl.BlockSpec((halo, bw), lambda i: (jnp.maximum(i * (tb // halo) - 1, 0), k))

    def halo_next(k):
        return pl.BlockSpec((halo, bw), lambda i: (jnp.minimum((i + 1) * (tb // halo), R // halo - 1), k))

    def const(arr):
        nd = arr.ndim
        return pl.BlockSpec(arr.shape, lambda i: (0,) * nd)

    branch_consts = [consts[k] for k in ("ccb", "scb", "conv_w", "conv_b", "ln_g", "ln_b", "w_s", "b_s")]
    gwf = bw // FOURIER_GROUPS
    blocks = (12 * _nbytes((tb, bw), BF16) + 4 * _nbytes((halo, bw), BF16) + N_BRANCH * _nbytes((tb, bw), BF16)
              + sum(_nbytes(a.shape, a.dtype) for a in branch_consts))
    ys = pl.pallas_call(
        functools.partial(_branch_kernel, n=n, tm=tb, halo=halo, fscale=float(n * gwf) ** -0.5),
        out_shape=jax.ShapeDtypeStruct((N_BRANCH, R, bw), BF16),
        grid=(R // tb,),
        in_specs=([pl.BlockSpec((tb, bw), lambda i: (i, 0)),
                   pl.BlockSpec((1, 1, FFT_N1, cl, bw), lambda i: (i // nt, 0, 0, i % nt, 0)),
                   pl.BlockSpec((1, 1, FFT_N1, cl, bw), lambda i: (i // nt, 1, 0, i % nt, 0))]
                  + [seg(k) for k in (0, 2, 3, 4, 5, 6, 7, 8, 9)]
                  + [halo_prev(3), halo_prev(5), halo_next(3), halo_next(5)]
                  + [const(a) for a in branch_consts]),
        out_specs=pl.BlockSpec((N_BRANCH, tb, bw), lambda i: (0, i, 0)),
        scratch_shapes=[pltpu.VMEM((FOURIER_GROUPS, tb, gwf), F32), pltpu.VMEM((FOURIER_GROUPS, tb, gwf), F32)],
        compiler_params=_compiler_params(("parallel",), blocks, 2 * _nbytes((tb, bw), F32),
                                         12 * _nbytes((tb, bw), F32)),
        name="branch_outputs",
    )(attn, v, v, *([p] * 9), *([p] * 4), *branch_consts)

    tm = min(1024, tile_rows)
    tn = min(512, D)
    te = min(512, D)
    nj, ne = D // tn, D // te
    ni = R // tm
    gate0 = N_BRANCH_COLS * bw // tn
    ia = lambda i, j: jnp.minimum(jnp.where(j < nj, i, i + 1), ni - 1)
    ja = lambda j: jnp.where(j < nj, j, 0)
    jb = lambda j: jnp.maximum(j - nj, 0)
    ip = lambda i: jnp.maximum(i - 1, 0)
    jo = lambda i, j: jnp.where(i == 0, 0, jb(j))

    def gates(g):
        return pl.BlockSpec((tm, tn), lambda i, j: (ia(i, j), gate0 + g * nj + ja(j)))

    blocks = (_nbytes((N_BRANCH, tm, bw), BF16) + N_BRANCH * _nbytes((tm, tn), BF16)
              + _nbytes((N_BRANCH, bw, tn), BF16) + _nbytes((D, te), BF16) + 2 * _nbytes((tm, te), F32))
    scratch = _nbytes((tm, D), BF16) + _nbytes((tm, D), F32) + 2 * _nbytes((tm, 128), F32)
    return pl.pallas_call(
        functools.partial(_mix_kernel, nj=nj, ne=ne, ni=ni),
        out_shape=jax.ShapeDtypeStruct((R, D), F32),
        grid=(ni + 1, nj + ne),
        in_specs=([pl.BlockSpec((N_BRANCH, tm, bw), lambda i, j: (0, ia(i, j), 0))]
                  + [gates(g) for g in range(N_BRANCH)]
                  + [pl.BlockSpec((N_BRANCH, bw, tn), lambda i, j: (0, 0, ja(j))),
                     pl.BlockSpec((D, te), lambda i, j: (0, jb(j))),
                     pl.BlockSpec((tm, te), lambda i, j: (ip(i), jb(j))),
                     pl.BlockSpec((1, 3, te), lambda i, j: (mod_row(ip(i) * tm), 0, jb(j))),
                     pl.BlockSpec((1, te), lambda i, j: (0, jb(j)))]),
        out_specs=pl.BlockSpec((tm, te), lambda i, j: (ip(i), jo(i, j))),
        scratch_shapes=[pltpu.VMEM((nj, tm, tn), BF16), pltpu.VMEM((ne, tm, te), F32),
                        pltpu.VMEM((tm, 128), F32), pltpu.VMEM((tm, 128), F32)],
        compiler_params=_compiler_params(("arbitrary", "arbitrary"), blocks, scratch,
                                         _nbytes((tm, D), BF16) + 6 * _nbytes((tm, tn), F32)),
        name="branch_mix",
    )(ys, p, p, p, p, consts["w_branch"], consts["w_out"], x2, mod3, consts["post_g"])


def _swap_halves(w):
    h = w.shape[-1] // 2
    return jnp.concatenate([w[..., h:], w[..., :h]], axis=-1)


def _relayout_w_in(w):
    rope_end = ROPE_OFF + QK_ROPE
    head_end = rope_end + HEAD_W - CQ_OFF
    rope = w[:, ROPE_OFF:rope_end]
    pad = jnp.zeros((w.shape[0], CQ_OFF - ROPE_OFF - 2 * QK_ROPE), w.dtype)
    head = jnp.concatenate([w[:, :ROPE_OFF], rope, _swap_halves(rope), pad, w[:, rope_end:head_end]], axis=1)
    return head.astype(BF16), w[:, head_end:].astype(BF16)


def _relayout_w_uq(w):
    w = w.reshape(w.shape[0], MLA_HEADS, QK_NOPE + QK_ROPE)
    rope = w[..., QK_NOPE:]
    return jnp.concatenate([w[..., :QK_NOPE], rope, _swap_halves(rope)], axis=-1).reshape(w.shape[0], -1).astype(BF16)


def _rope_table(n, rotate):
    half = QK_ROPE // 2
    if rotate:
        pos = np.arange(n)
        inv = ROPE_THETA ** (-np.arange(0, half, 2, dtype=np.float64) / half)
        ang = np.concatenate([(pos // GRID_W)[:, None] * inv, (pos % GRID_W)[:, None] * inv], axis=-1)
    else:
        ang = np.zeros((n, half))
    cos, sin = np.cos(ang), np.sin(ang)
    return jnp.asarray(np.concatenate([cos, cos, -sin, sin], axis=-1), F32)


def _channel_dft(bw):
    gw = bw // FOURIER_GROUPS
    a = np.arange(gw)
    ang = 2.0 * np.pi * ((a[:, None] * a[None, :]) % gw) / gw
    return jnp.asarray(np.cos(ang), F32).astype(BF16), jnp.asarray(np.sin(ang), F32).astype(BF16)


def kernel(x, c, ctx, c_ctx, w_mod, b_mod, pre_g, post_g, w_in, q_norm_g, kv_norm_g, w_uq, w_ukv,
           conv_w, conv_b, sgu_ln_g, sgu_ln_b, sgu_w, sgu_b, w_branch, w_out):
    B, n, D = x.shape
    nc = ctx.shape[1]
    depth = w_in.shape[0]
    bw = w_branch.shape[2]
    assert w_in.shape[2] == ROPE_OFF + QK_ROPE + (HEAD_W - CQ_OFF) + N_BRANCH_COLS * bw + N_BRANCH * D
    assert kv_norm_g.shape[1] == ROPE_OFF and q_norm_g.shape[1] == HEAD_W - CQ_OFF and B <= 4

    cc = jnp.zeros((8, D), F32).at[:B].set(c).at[B].set(c_ctx)
    mod = _modulation(cc, w_mod, b_mod).reshape(depth, 8, 3, D)

    cs_x, cs_c = _rope_table(n, True), _rope_table(nc, False)
    ccb, scb = _channel_dft(bw)
    x2 = x.reshape(B * n, D)
    c2 = ctx.reshape(B * nc, D)
    row_x = lambda r: r // n
    row_c = lambda r: B

    for l in range(depth):
        last = l == depth - 1
        w_head, w_rest = _relayout_w_in(w_in[l])
        w_q2 = _relayout_w_uq(w_uq[l])
        w_ukv_l = w_ukv[l].astype(BF16)
        consts = dict(
            ccb=ccb, scb=scb, conv_w=conv_w[l], conv_b=conv_b[l].reshape(1, bw),
            ln_g=sgu_ln_g[l].reshape(1, bw), ln_b=sgu_ln_b[l].reshape(1, bw),
            w_s=sgu_w[l].astype(BF16),
            b_s=jnp.broadcast_to(sgu_b[l][:, :, None], sgu_b.shape[1:] + (bw // sgu_w.shape[1],)),
            w_branch=w_branch[l].astype(BF16), w_out=w_out[l].astype(BF16), post_g=post_g[l].reshape(1, D))

        hx, px = _in_projection(x2, mod[l], row_x, pre_g[l], w_head, w_rest, min(1024, n))
        hc, *pc = _in_projection(c2, mod[l], row_c, pre_g[l], w_head, None if last else w_rest,
                                 min(1024, B * nc))
        q_c, k_c, v_c = _mla_prep(hc, cs_c, kv_norm_g[l], q_norm_g[l], w_ukv_l, w_q2, B, nc)
        q_x, k_x, v_x = _mla_prep(hx, cs_x, kv_norm_g[l], q_norm_g[l], w_ukv_l, w_q2, B, n)

        attn_x = _attention(q_x, k_c, v_c, k_x, v_x).reshape(B * n, -1)
        four_x = _position_dft(px, 1, bw, B, n)
        new_x = _mix(px, attn_x, four_x, x2, mod[l], row_x, consts, n, n)
        if not last:
            attn_c = _attention(q_c, k_c, v_c).reshape(B * nc, -1)
            four_c = _position_dft(pc[0], 1, bw, B, nc)
            c2 = _mix(pc[0], attn_c, four_c, c2, mod[l], row_c, consts, nc, B * nc)
        x2 = new_x
    return x2.reshape(B, n, D)
```
